```python
import math
import jax, jax.numpy as jnp
from jax import lax
import numpy as np

D_MODEL = 1024
BATCH = 8
SEQ = 2048
DEPTH = 4

N_MIXERS = 3
DN_ALPHA = (2 * DEPTH) ** 0.25
DN_BETA = (8 * DEPTH) ** -0.25
LN_EPS = 1e-5
RMS_EPS = 1e-6

FOX_HEADS = 16
FOX_HEAD_DIM = D_MODEL // FOX_HEADS
FOX_Q_BLOCK = 128

GLA_HEADS = 4
GLA_DK = D_MODEL // 2
GLA_DV = D_MODEL
GLA_HK = GLA_DK // GLA_HEADS
GLA_HV = GLA_DV // GLA_HEADS
GLA_GATE_RANK = 16
GLA_GATE_TAU = 16.0
GLA_CHUNK = 64

SSD_D_INNER = 2 * D_MODEL
SSD_HEAD_DIM = 64
SSD_HEADS = SSD_D_INNER // SSD_HEAD_DIM
SSD_GROUPS = 8
SSD_STATE = 128
SSD_CONV = 4
SSD_CHUNK = 128
SSD_CONV_DIM = SSD_D_INNER + 2 * SSD_GROUPS * SSD_STATE

N_EXPERTS = 64
TOP_K = 8
N_EXPERT_GROUPS = 8
TOPK_GROUPS = 4
EXPERT_FF = 256
SHARED_FF = 256
ROUTED_SCALE = 2.5
MOE_BLOCK = 128

N_FOX = len(range(0, DEPTH, N_MIXERS))
N_GLA = len(range(1, DEPTH, N_MIXERS))
N_SSD = len(range(2, DEPTH, N_MIXERS))

kernel_name = 'hybrid_fox_gla_ssd_moe_deepnorm'


def layer_norm(x, g, b):
    xf = x.astype(jnp.float32)
    mu = jnp.mean(xf, -1, keepdims=True)
    var = jnp.mean(jnp.square(xf - mu), -1, keepdims=True)
    return ((xf - mu) * lax.rsqrt(var + LN_EPS) * g + b).astype(x.dtype)


def rms_norm(x, g):
    xf = x.astype(jnp.float32)
    return xf * lax.rsqrt(jnp.mean(xf * xf, -1, keepdims=True) + RMS_EPS) * g


def fox_mixer(x, w_in, b_f, w_out):
    B_, S, _ = x.shape
    H, Dh, Q = FOX_HEADS, FOX_HEAD_DIM, FOX_Q_BLOCK
    proj = x @ w_in
    q, k, v, f_logit = jnp.split(proj, [D_MODEL, 2 * D_MODEL, 3 * D_MODEL], axis=-1)
    q = q.reshape(B_, S, H, Dh)
    k = k.reshape(B_, S, H, Dh)
    v = v.reshape(B_, S, H, Dh)
    log_f = jax.nn.log_sigmoid(f_logit.astype(jnp.float32) + b_f.astype(jnp.float32))
    c = jnp.cumsum(log_f, axis=1).transpose(0, 2, 1)
    scale = Dh ** -0.5
    outs = []
    for qb in range(S // Q):
        t0, t1 = qb * Q, (qb + 1) * Q
        logits = jnp.einsum('bqhd,bkhd->bhqk', q[:, t0:t1], k[:, :t1],
                            preferred_element_type=jnp.float32) * scale
        logits = logits + c[:, :, t0:t1, None] - c[:, :, None, :t1]
        causal = (t0 + jnp.arange(Q))[:, None] >= jnp.arange(t1)[None, :]
        p = jax.nn.softmax(jnp.where(causal, logits, -jnp.inf), axis=-1)
        outs.append(jnp.einsum('bhqk,bkhd->bqhd', p.astype(v.dtype), v[:, :t1]))
    o = jnp.concatenate(outs, axis=1).reshape(B_, S, D_MODEL)
    return o @ w_out


def gla_mixer(x, w_in, w_gate_up, b_gate, norm_g, w_out):
    B_, S, _ = x.shape
    H, dk, dv, C = GLA_HEADS, GLA_HK, GLA_HV, GLA_CHUNK
    n = S // C
    proj = x @ w_in
    q, k, v, r, g_low = jnp.split(
        proj, [GLA_DK, 2 * GLA_DK, 2 * GLA_DK + GLA_DV, 2 * GLA_DK + 2 * GLA_DV], axis=-1)
    log_a = jax.nn.log_sigmoid((g_low @ w_gate_up + b_gate).astype(jnp.float32)) / GLA_GATE_TAU

    def chunks(t, d):
        return t.reshape(B_, n, C, H, d).astype(jnp.float32)

    qc = chunks(q, dk) * (dk ** -0.5)
    kc = chunks(k, dk)
    vc = chunks(v, dv)
    bcum = jnp.cumsum(chunks(log_a, dk), axis=2)
    b_last = bcum[:, :, -1:]
    q_dec = qc * jnp.exp(bcum)
    k_inv = kc * jnp.exp(-bcum)
    k_to_end = kc * jnp.exp(b_last - bcum)
    att = jnp.einsum('bnthk,bnshk->bnhts', q_dec, k_inv)
    att = jnp.where(jnp.tril(jnp.ones((C, C), bool)), att, 0.0)
    o_intra = jnp.einsum('bnhts,bnshv->bnthv', att, vc)
    chunk_kv = jnp.einsum('bnshk,bnshv->bnhkv', k_to_end, vc)
    decay = jnp.exp(b_last[:, :, 0])

    def step(state, inp):
        kv, dec = inp
        return state * dec[..., None] + kv, state

    s0 = jnp.zeros((B_, H, dk, dv), jnp.float32)
    _, prev = lax.scan(step, s0, (jnp.moveaxis(chunk_kv, 1, 0), jnp.moveaxis(decay, 1, 0)))
    prev = jnp.moveaxis(prev, 0, 1)
    o_inter = jnp.einsum('bnthk,bnhkv->bnthv', q_dec, prev)
    o = rms_norm(o_intra + o_inter, norm_g)
    o = o.reshape(B_, S, GLA_DV).astype(x.dtype) * jax.nn.silu(r)
    return o @ w_out


def ssd_mixer(x, w_in, conv_w, conv_b, dt_bias, a_log, d_skip, norm_g, w_out):
    B_, S, _ = x.shape
    G, Hg, P, N, L = SSD_GROUPS, SSD_HEADS // SSD_GROUPS, SSD_HEAD_DIM, SSD_STATE, SSD_CHUNK
    n = S // L
    proj = x @ w_in
    z, xbc, dt_raw = jnp.split(proj, [SSD_D_INNER, SSD_D_INNER + SSD_CONV_DIM], axis=-1)
    xbc = lax.conv_general_dilated(xbc, conv_w[:, None, :], (1,), [(SSD_CONV - 1, 0)],
                                   dimension_numbers=('NWC', 'WIO', 'NWC'),
                                   feature_group_count=SSD_CONV_DIM)
    xbc = jax.nn.silu(xbc + conv_b).astype(jnp.float32)
    xs, bm, cm = jnp.split(xbc, [SSD_D_INNER, SSD_D_INNER + G * N], axis=-1)
    xs = xs.reshape(B_, n, L, G, Hg, P)
    bm = bm.reshape(B_, n, L, G, N)
    cm = cm.reshape(B_, n, L, G, N)
    dt = jax.nn.softplus(dt_raw.astype(jnp.float32) + dt_bias).reshape(B_, n, L, G, Hg)
    a = dt * (-jnp.exp(a_log.astype(jnp.float32))).reshape(G, Hg)
    acum = jnp.cumsum(a, axis=2)
    xdt = xs * dt[..., None]
    ac = jnp.moveaxis(acum, 2, -1)
    seg = ac[..., :, None] - ac[..., None, :]
    decay = jnp.exp(jnp.where(jnp.tril(jnp.ones((L, L), bool)), seg, -jnp.inf))
    cb = jnp.einsum('bctgn,bcsgn->bcgts', cm, bm)
    y_diag = jnp.einsum('bcghts,bcsghp->bctghp', cb[:, :, :, None] * decay, xdt)
    a_last = acum[:, :, -1]
    to_end = jnp.exp(a_last[:, :, None] - acum)
    states = jnp.einsum('bcsgn,bcsghp->bcghpn', bm, xdt * to_end[..., None])

    def step(h, inp):
        st, dec = inp
        return h * dec[..., None, None] + st, h

    h0 = jnp.zeros((B_, G, Hg, P, N), jnp.float32)
    _, prev = lax.scan(step, h0, (jnp.moveaxis(states, 1, 0), jnp.moveaxis(jnp.exp(a_last), 1, 0)))
    prev = jnp.moveaxis(prev, 0, 1)
    y_off = jnp.einsum('bctgn,bcghpn->bctghp', cm, prev) * jnp.exp(acum)[..., None]
    y = y_diag + y_off + xs * d_skip.astype(jnp.float32).reshape(G, Hg, 1)
    y = y.reshape(B_, S, G, Hg * P) * jax.nn.silu(z.astype(jnp.float32)).reshape(B_, S, G, Hg * P)
    y = rms_norm(y, norm_g.reshape(G, Hg * P))
    return y.reshape(B_, S, SSD_D_INNER).astype(x.dtype) @ w_out


def moe(x, w_router, router_bias, w_gate, w_up, w_down, ws_gate, ws_up, ws_down):
    B_, S, D = x.shape
    T = B_ * S
    xf = x.reshape(T, D)
    scores = jax.nn.sigmoid((xf @ w_router).astype(jnp.float32))
    sel = scores + router_bias.astype(jnp.float32)
    grp_score = lax.top_k(sel.reshape(T, N_EXPERT_GROUPS, -1), 2)[0].sum(-1)
    _, top_grp = lax.top_k(grp_score, TOPK_GROUPS)
    grp_mask = jnp.any(top_grp[..., None] == jnp.arange(N_EXPERT_GROUPS), axis=-2)
    exp_mask = jnp.repeat(grp_mask, N_EXPERTS // N_EXPERT_GROUPS, axis=-1)
    _, top_e = lax.top_k(jnp.where(exp_mask, sel, -jnp.inf), TOP_K)
    w = jnp.take_along_axis(scores, top_e, axis=-1)
    w = w / jnp.sum(w, -1, keepdims=True) * ROUTED_SCALE

    A = T * TOP_K
    e_flat = top_e.reshape(-1)
    tok_flat = jnp.repeat(jnp.arange(T, dtype=jnp.int32), TOP_K)
    w_flat = w.reshape(-1)
    order = jnp.argsort(e_flat)
    e_sorted = e_flat[order]
    counts = jnp.bincount(e_flat, length=N_EXPERTS)
    padded = (counts + MOE_BLOCK - 1) // MOE_BLOCK * MOE_BLOCK
    start = jnp.cumsum(counts) - counts
    pend = jnp.cumsum(padded)
    pstart = pend - padded
    dest = pstart[e_sorted] + jnp.arange(A) - start[e_sorted]
    n_blocks = -(-A // MOE_BLOCK) + N_EXPERTS
    rows_total = n_blocks * MOE_BLOCK
    row_tok = jnp.full((rows_total,), T, jnp.int32).at[dest].set(tok_flat[order])
    row_w = jnp.zeros((rows_total,), jnp.float32).at[dest].set(w_flat[order])
    block_exp = jnp.clip(jnp.searchsorted(pend, jnp.arange(n_blocks) * MOE_BLOCK, side='right'),
                         0, N_EXPERTS - 1)
    x_pad = jnp.concatenate([xf, jnp.zeros((1, D), xf.dtype)], axis=0)

    def body(acc, blk):
        rows, rw, e = blk
        xb = x_pad[rows]
        h = jax.nn.silu(xb @ w_gate[e]) * (xb @ w_up[e])
        yb = (h @ w_down[e]) * rw[:, None].astype(xb.dtype)
        return acc.at[rows].add(yb), None

    acc, _ = lax.scan(body, jnp.zeros((T + 1, D), xf.dtype),
                      (row_tok.reshape(n_blocks, MOE_BLOCK), row_w.reshape(n_blocks, MOE_BLOCK), block_exp))
    shared = (jax.nn.silu(xf @ ws_gate) * (xf @ ws_up)) @ ws_down
    return (acc[:T] + shared).reshape(B_, S, D)


def setup_inputs(seed: int = 0) -> dict:
    key = jax.random.key(seed)
    ks = iter(jax.random.split(key, 32))
    D = D_MODEL

    def nrm(shape, fan_in, scale=1.0):
        return jax.random.normal(next(ks), shape, jnp.float32) * (scale * fan_in ** -0.5)

    def small(shape, s=0.01):
        return jax.random.normal(next(ks), shape, jnp.float32) * s

    x = jax.random.normal(next(ks), (BATCH, SEQ, D), jnp.float32)

    fox_w_in = nrm((N_FOX, D, 3 * D + FOX_HEADS), D)
    fox_w_in = fox_w_in.at[:, :, 2 * D:3 * D].multiply(DN_BETA)
    fox_b_f = jax.random.uniform(next(ks), (N_FOX, FOX_HEADS), jnp.float32, 1.0, 4.0)
    fox_w_out = nrm((N_FOX, D, D), D, DN_BETA)

    gla_w_in = nrm((N_GLA, D, 2 * GLA_DK + 2 * GLA_DV + GLA_GATE_RANK), D)
    gla_w_in = gla_w_in.at[:, :, 2 * GLA_DK:2 * GLA_DK + GLA_DV].multiply(DN_BETA)
    gla_w_gate_up = nrm((N_GLA, GLA_GATE_RANK, GLA_DK), GLA_GATE_RANK)
    gla_b_gate = small((N_GLA, GLA_DK), 0.1)
    gla_norm = 1.0 + small((N_GLA, GLA_HV))
    gla_w_out = nrm((N_GLA, GLA_DV, D), GLA_DV, DN_BETA)

    ssd_w_in = nrm((N_SSD, D, SSD_D_INNER + SSD_CONV_DIM + SSD_HEADS), D)
    ssd_w_in = ssd_w_in.at[:, :, SSD_D_INNER:2 * SSD_D_INNER].multiply(DN_BETA)
    ssd_conv_w = nrm((N_SSD, SSD_CONV, SSD_CONV_DIM), SSD_CONV)
    ssd_conv_b = small((N_SSD, SSD_CONV_DIM), 0.02)
    dt0 = jnp.exp(jax.random.uniform(next(ks), (N_SSD, SSD_HEADS), jnp.float32,
                                     math.log(1e-3), math.log(1e-1)))
    ssd_dt_bias = dt0 + jnp.log(-jnp.expm1(-dt0))
    ssd_a_log = jnp.log(jax.random.uniform(next(ks), (N_SSD, SSD_HEADS), jnp.float32, 1.0, 16.0))
    ssd_d = 1.0 + small((N_SSD, SSD_HEADS), 0.1)
    ssd_norm = 1.0 + small((N_SSD, SSD_D_INNER))
    ssd_w_out = nrm((N_SSD, SSD_D_INNER, D), SSD_D_INNER, DN_BETA)

    ln1_g = 1.0 + small((DEPTH, D))
    ln1_b = small((DEPTH, D))

    moe_router = nrm((DEPTH, D, N_EXPERTS), D)
    moe_router_bias = small((DEPTH, N_EXPERTS), 0.01)
    moe_w_gate = nrm((DEPTH, N_EXPERTS, D, EXPERT_FF), D)
    moe_w_up = nrm((DEPTH, N_EXPERTS, D, EXPERT_FF), D)
    moe_w_down = nrm((DEPTH, N_EXPERTS, EXPERT_FF, D), EXPERT_FF, DN_BETA)
    moe_ws_gate = nrm((DEPTH, D, SHARED_FF), D)
    moe_ws_up = nrm((DEPTH, D, SHARED_FF), D)
    moe_ws_down = nrm((DEPTH, SHARED_FF, D), SHARED_FF, DN_BETA)

    ln2_g = 1.0 + small((DEPTH, D))
    ln2_b = small((DEPTH, D))

    return {'x': x,
            'fox_w_in': fox_w_in, 'fox_b_f': fox_b_f, 'fox_w_out': fox_w_out,
            'gla_w_in': gla_w_in, 'gla_w_gate_up': gla_w_gate_up, 'gla_b_gate': gla_b_gate,
            'gla_norm': gla_norm, 'gla_w_out': gla_w_out,
            'ssd_w_in': ssd_w_in, 'ssd_conv_w': ssd_conv_w, 'ssd_conv_b': ssd_conv_b,
            'ssd_dt_bias': ssd_dt_bias, 'ssd_a_log': ssd_a_log, 'ssd_d': ssd_d,
            'ssd_norm': ssd_norm, 'ssd_w_out': ssd_w_out,
            'ln1_g': ln1_g, 'ln1_b': ln1_b,
            'moe_router': moe_router, 'moe_router_bias': moe_router_bias,
            'moe_w_gate': moe_w_gate, 'moe_w_up': moe_w_up, 'moe_w_down': moe_w_down,
            'moe_ws_gate': moe_ws_gate, 'moe_ws_up': moe_ws_up, 'moe_ws_down': moe_ws_down,
            'ln2_g': ln2_g, 'ln2_b': ln2_b}


def reference(x, fox_w_in, fox_b_f, fox_w_out, gla_w_in, gla_w_gate_up, gla_b_gate, gla_norm,
              gla_w_out, ssd_w_in, ssd_conv_w, ssd_conv_b, ssd_dt_bias, ssd_a_log, ssd_d,
              ssd_norm, ssd_w_out, ln1_g, ln1_b, moe_router, moe_router_bias, moe_w_gate,
              moe_w_up, moe_w_down, moe_ws_gate, moe_ws_up, moe_ws_down, ln2_g, ln2_b):
    for i in range(DEPTH):
        kind, j = i % N_MIXERS, i // N_MIXERS
        if kind == 0:
            y = fox_mixer(x, fox_w_in[j], fox_b_f[j], fox_w_out[j])
        elif kind == 1:
            y = gla_mixer(x, gla_w_in[j], gla_w_gate_up[j], gla_b_gate[j], gla_norm[j], gla_w_out[j])
        else:
            y = ssd_mixer(x, ssd_w_in[j], ssd_conv_w[j], ssd_conv_b[j], ssd_dt_bias[j],
                          ssd_a_log[j], ssd_d[j], ssd_norm[j], ssd_w_out[j])
        x = layer_norm(DN_ALPHA * x + y, ln1_g[i], ln1_b[i])
        y = moe(x, moe_router[i], moe_router_bias[i], moe_w_gate[i], moe_w_up[i], moe_w_down[i],
                moe_ws_gate[i], moe_ws_up[i], moe_ws_down[i])
        x = layer_norm(DN_ALPHA * x + y, ln2_g[i], ln2_b[i])
    return x
```

```python
import functools

import jax
import jax.numpy as jnp
from jax import lax
from jax.experimental import pallas as pl
from jax.experimental.pallas import tpu as pltpu

F32 = jnp.float32
BF16 = jnp.bfloat16
HI = lax.Precision.HIGHEST

D_MODEL = 1024
DEPTH = 4
N_MIXERS = 3
DN_ALPHA = (2 * DEPTH) ** 0.25
LN_EPS = 1e-5
RMS_EPS = 1e-6

FOX_HEADS = 16
FOX_HEAD_DIM = 64
GLA_HEADS = 4
GLA_DK = 512
GLA_DV = 1024
GLA_HK = 128
GLA_HV = 256
GLA_GATE_RANK = 16
GLA_GATE_TAU = 16.0
GLA_CHUNK = 64
SSD_D_INNER = 2048
SSD_HEADS = 32
SSD_GROUPS = 8
SSD_HG = 4
SSD_P = 64
SSD_STATE = 128
SSD_CONV = 4
SSD_CHUNK = 128
N_EXPERTS = 64
TOP_K = 8
N_EXPERT_GROUPS = 8
TOPK_GROUPS = 4
EXPERT_FF = 256
ROUTED_SCALE = 2.5

LANES = 128
SUBLANES = 8
VMEM_LIMIT = 56 * 2 ** 20
MOE_ROWS = 128


def _params(sem):
    return pltpu.CompilerParams(dimension_semantics=sem, vmem_limit_bytes=VMEM_LIMIT)


def _sigmoid(x):
    return 1.0 / (1.0 + jnp.exp(-x))


def _silu(x):
    return x * _sigmoid(x)


def _log_sigmoid(x):
    return jnp.minimum(x, 0.0) - jnp.log(1.0 + jnp.exp(-jnp.abs(x)))


def _softplus(x):
    return jnp.maximum(x, 0.0) + jnp.log(1.0 + jnp.exp(-jnp.abs(x)))


def _dot(a, b, precision=None):
    return jnp.dot(a, b, preferred_element_type=F32, precision=precision)


def _dot_nt(a, b, precision=None):
    return lax.dot_general(a, b, (((1,), (1,)), ((), ())), preferred_element_type=F32,
                           precision=precision)


def _dot_tn(a, b, precision=None):
    return lax.dot_general(a, b, (((0,), (0,)), ((), ())), preferred_element_type=F32,
                           precision=precision)


def _tri(n, lower):
    r = lax.broadcasted_iota(jnp.int32, (n, n), 0)
    c = lax.broadcasted_iota(jnp.int32, (n, n), 1)
    return (r >= c) if lower else (r <= c)


def _mm_body(x_ref, w_ref, o_ref, *, precision):
    o_ref[...] = _dot(x_ref[...], w_ref[...], precision).astype(o_ref.dtype)


def _matmul(x, w, out_dtype, name, tm=512, tn=512, precision=None):
    m, k = x.shape
    n = w.shape[1]
    tm, tn = min(tm, m), min(tn, n)
    return pl.pallas_call(
        functools.partial(_mm_body, precision=precision),
        out_shape=jax.ShapeDtypeStruct((m, n), out_dtype),
        grid=(n // tn, m // tm),
        in_specs=[pl.BlockSpec((tm, k), lambda j, i: (i, 0)),
                  pl.BlockSpec((k, tn), lambda j, i: (0, j))],
        out_specs=pl.BlockSpec((tm, tn), lambda j, i: (i, j)),
        compiler_params=_params(("parallel", "parallel")),
        name=name,
    )(x, w)


def _layer_norm_rows(z, g, b):
    mu = jnp.mean(z, axis=-1, keepdims=True)
    zc = z - mu
    var = jnp.mean(zc * zc, axis=-1, keepdims=True)
    return zc * lax.rsqrt(var + LN_EPS) * g + b


def _proj_ln_body(o_ref, w_ref, x_ref, g_ref, b_ref, xf_ref, xb_ref):
    y = _dot(o_ref[...], w_ref[...])
    out = _layer_norm_rows(DN_ALPHA * x_ref[...] + y, g_ref[...], b_ref[...])
    xf_ref[...] = out
    xb_ref[...] = out.astype(BF16)


def _proj_ln(o, w, x, g, b, name, tm=512):
    m, k = o.shape
    d = w.shape[1]
    tm = min(tm, m)
    return pl.pallas_call(
        _proj_ln_body,
        out_shape=(jax.ShapeDtypeStruct((m, d), F32), jax.ShapeDtypeStruct((m, d), BF16)),
        grid=(m // tm,),
        in_specs=[pl.BlockSpec((tm, k), lambda i: (i, 0)),
                  pl.BlockSpec((k, d), lambda i: (0, 0)),
                  pl.BlockSpec((tm, d), lambda i: (i, 0)),
                  pl.BlockSpec((1, d), lambda i: (0, 0)),
                  pl.BlockSpec((1, d), lambda i: (0, 0))],
        out_specs=(pl.BlockSpec((tm, d), lambda i: (i, 0)),
                   pl.BlockSpec((tm, d), lambda i: (i, 0))),
        compiler_params=_params(("parallel",)),
        name=name,
    )(o, w, x, g.reshape(1, d), b.reshape(1, d))


def _fox_gate_body(x_ref, w_ref, b_ref, c_ref, *, seq):
    f = _dot(x_ref[...], w_ref[...], HI)
    z = f.T[:FOX_HEADS, :] + b_ref[...]
    lf = _log_sigmoid(z)
    upper = _tri(LANES, lower=False).astype(F32)
    carry = jnp.zeros((FOX_HEADS, 1), F32)
    for blk in range(seq // LANES):
        seg = _dot(lf[:, blk * LANES:(blk + 1) * LANES], upper, HI) + carry
        c_ref[0, :, blk * LANES:(blk + 1) * LANES] = seg
        carry = seg[:, LANES - 1:LANES]


def _fox_gate(x, w_f, b_f, batch, seq):
    d = x.shape[1]
    return pl.pallas_call(
        functools.partial(_fox_gate_body, seq=seq),
        out_shape=jax.ShapeDtypeStruct((batch, FOX_HEADS, seq), F32),
        grid=(batch,),
        in_specs=[pl.BlockSpec((seq, d), lambda b: (b, 0)),
                  pl.BlockSpec((d, LANES), lambda b: (0, 0)),
                  pl.BlockSpec((FOX_HEADS, 1), lambda b: (0, 0))],
        out_specs=pl.BlockSpec((1, FOX_HEADS, seq), lambda b: (b, 0, 0)),
        compiler_params=_params(("parallel",)),
        name="fox_gate",
    )(x, w_f, b_f)


def _fox_attn_body(q_ref, k_ref, v_ref, c_ref, o_ref, *, seq, tq):
    lo = lax.broadcasted_iota(jnp.int32, (1, LANES), 1) < FOX_HEAD_DIM
    causal = _tri(tq, lower=True)
    scale = FOX_HEAD_DIM ** -0.5

    def step(j, carry, q_heads, masked):
        r0 = pl.multiple_of(j * tq, tq)
        k = k_ref[pl.ds(r0, tq), :]
        v = v_ref[pl.ds(r0, tq), :]
        out = []
        for h in range(2):
            m, l, acc = carry[h]
            s = _dot_nt(q_heads[h], k) - c_ref[0, h, pl.ds(j, 1), :]
            if masked:
                s = jnp.where(causal, s, -jnp.inf)
            m_new = jnp.maximum(m, jnp.max(s, axis=1, keepdims=True))
            p = jnp.exp(s - m_new)
            a = jnp.exp(m - m_new)
            l = a * l + jnp.sum(p, axis=1, keepdims=True)
            acc = a * acc + _dot(p.astype(BF16), v)
            out.append((m_new, l, acc))
        return tuple(out)

    for qi in range(seq // tq):
        q = q_ref[qi * tq:(qi + 1) * tq, :] * scale
        zero = jnp.zeros_like(q)
        q_heads = (jnp.where(lo, q, zero), jnp.where(lo, zero, q))
        init = tuple((jnp.full((tq, 1), -jnp.inf, F32), jnp.zeros((tq, 1), F32),
                      jnp.zeros((tq, LANES), F32)) for _ in range(2))
        carry = lax.fori_loop(0, qi, lambda j, c: step(j, c, q_heads, False), init)
        (_, l0, a0), (_, l1, a1) = step(qi, carry, q_heads, True)
        o_ref[qi * tq:(qi + 1) * tq, :] = jnp.where(lo, a0 / l0, a1 / l1).astype(o_ref.dtype)


def _fox_attn(qkv, c, batch, seq, tq=256):
    tq = min(tq, seq)
    pairs = FOX_HEADS // 2
    c4 = c.reshape(batch, FOX_HEADS, seq // tq, tq)
    return pl.pallas_call(
        functools.partial(_fox_attn_body, seq=seq, tq=tq),
        out_shape=jax.ShapeDtypeStruct((batch * seq, D_MODEL), BF16),
        grid=(batch, pairs),
        in_specs=[pl.BlockSpec((seq, LANES), lambda b, p: (b, p)),
                  pl.BlockSpec((seq, LANES), lambda b, p: (b, pairs + p)),
                  pl.BlockSpec((seq, LANES), lambda b, p: (b, 2 * pairs + p)),
                  pl.BlockSpec((1, 2, seq // tq, tq), lambda b, p: (b, p, 0, 0))],
        out_specs=pl.BlockSpec((seq, LANES), lambda b, p: (b, p)),
        compiler_params=_params(("parallel", "parallel")),
        name="fox_attn",
    )(qkv, qkv, qkv, c4)


def _fox_layer(xf, xb, w_in, b_f, w_out, ln_g, ln_b, batch, seq):
    d = D_MODEL
    qkv = _matmul(xb, w_in[:, :3 * d].astype(BF16), BF16, "fox_qkv")
    w_f = jnp.pad(w_in[:, 3 * d:], ((0, 0), (0, LANES - FOX_HEADS)))
    c = _fox_gate(xf, w_f, b_f.reshape(FOX_HEADS, 1), batch, seq)
    o = _fox_attn(qkv, c, batch, seq)
    return _proj_ln(o, w_out.astype(BF16), xf, ln_g, ln_b, "fox_out_ln")


def _gla_body(q_ref, k_ref, v_ref, r_ref, gl_ref, wgu_ref, bg_ref, ng_ref, o_ref, st_ref, *, seq):
    ch = GLA_CHUNK
    lower = _tri(ch, lower=True)
    lower_f = lower.astype(F32)
    st_ref[...] = jnp.zeros_like(st_ref)

    def chunk(i, carry):
        rows = pl.ds(pl.multiple_of(i * ch, ch), ch)
        gate = _dot(gl_ref[rows, :], wgu_ref[...], HI) + bg_ref[...]
        log_a = _log_sigmoid(gate) * (1.0 / GLA_GATE_TAU)
        bcum = _dot(lower_f, log_a, HI)
        b_last = bcum[ch - 1:ch, :]
        q = q_ref[rows, :].astype(F32) * (GLA_HK ** -0.5)
        k = k_ref[rows, :].astype(F32)
        v = v_ref[rows, :]
        q_dec = (q * jnp.exp(bcum)).astype(BF16)
        k_inv = (k * jnp.exp(-bcum)).astype(BF16)
        k_end = (k * jnp.exp(b_last - bcum)).astype(BF16)
        att = jnp.where(lower, _dot_nt(q_dec, k_inv), 0.0)
        state_t = st_ref[...]
        o = _dot(att.astype(BF16), v) + _dot_nt(q_dec, state_t.astype(BF16))
        st_ref[...] = state_t * jnp.exp(b_last) + _dot_tn(v, k_end)
        ms = jnp.mean(o * o, axis=-1, keepdims=True)
        on = o * lax.rsqrt(ms + RMS_EPS) * ng_ref[...]
        o_ref[rows, :] = (on * _silu(r_ref[rows, :].astype(F32))).astype(o_ref.dtype)
        return carry

    lax.fori_loop(0, seq // ch, chunk, 0)


def _gla_mix(proj, g_low, w_gu, b_gate, norm_g, batch, seq):
    nq = GLA_DK // GLA_HK
    nv = 2 * GLA_DK // GLA_HV
    nr = (2 * GLA_DK + GLA_DV) // GLA_HV
    return pl.pallas_call(
        functools.partial(_gla_body, seq=seq),
        out_shape=jax.ShapeDtypeStruct((batch * seq, GLA_DV), BF16),
        grid=(batch, GLA_HEADS),
        in_specs=[pl.BlockSpec((seq, GLA_HK), lambda b, h: (b, h)),
                  pl.BlockSpec((seq, GLA_HK), lambda b, h: (b, nq + h)),
                  pl.BlockSpec((seq, GLA_HV), lambda b, h: (b, nv + h)),
                  pl.BlockSpec((seq, GLA_HV), lambda b, h: (b, nr + h)),
                  pl.BlockSpec((seq, LANES), lambda b, h: (b, 0)),
                  pl.BlockSpec((LANES, GLA_HK), lambda b, h: (0, h)),
                  pl.BlockSpec((1, GLA_HK), lambda b, h: (0, h)),
                  pl.BlockSpec((1, GLA_HV), lambda b, h: (0, 0))],
        out_specs=pl.BlockSpec((seq, GLA_HV), lambda b, h: (b, h)),
        scratch_shapes=[pltpu.VMEM((GLA_HV, GLA_HK), F32)],
        compiler_params=_params(("parallel", "parallel")),
        name="gla_mix",
    )(proj, proj, proj, proj, g_low, w_gu, b_gate.reshape(1, GLA_DK), norm_g.reshape(1, GLA_HV))


def _gla_layer(xf, xb, w_in, w_gate_up, b_gate, norm_g, w_out, ln_g, ln_b, batch, seq):
    n_main = 2 * GLA_DK + 2 * GLA_DV
    proj = _matmul(xb, w_in[:, :n_main].astype(BF16), BF16, "gla_proj")
    w_g = jnp.pad(w_in[:, n_main:], ((0, 0), (0, LANES - GLA_GATE_RANK)))
    g_low = _matmul(xf, w_g, F32, "gla_gate", tn=LANES, precision=HI)
    w_gu = jnp.pad(w_gate_up, ((0, LANES - GLA_GATE_RANK), (0, 0)))
    y = _gla_mix(proj, g_low, w_gu, b_gate, norm_g, batch, seq)
    return _proj_ln(y, w_out.astype(BF16), xf, ln_g, ln_b, "gla_out_ln")


def _ssd_gate_body(x_ref, w_ref, bias_ref, alog_ref, dt_ref, ac_ref, dtt_ref, act_ref, *, seq):
    ch = SSD_CHUNK
    raw = _dot(x_ref[...], w_ref[...], HI)
    dt = _softplus(raw + bias_ref[...])
    a = dt * (-jnp.exp(alog_ref[...]))
    lower_f = _tri(ch, lower=True).astype(F32)
    dt_ref[...] = dt
    dtt_ref[0] = dt.T
    for c in range(seq // ch):
        acum = _dot(lower_f, a[c * ch:(c + 1) * ch, :], HI)
        ac_ref[c * ch:(c + 1) * ch, :] = acum
        act_ref[0, :, c * ch:(c + 1) * ch] = acum.T


def _ssd_gate(x, w_dt, dt_bias, a_log, batch, seq):
    d = x.shape[1]
    col = jax.ShapeDtypeStruct((batch * seq, LANES), F32)
    row = jax.ShapeDtypeStruct((batch, LANES, seq), F32)
    return pl.pallas_call(
        functools.partial(_ssd_gate_body, seq=seq),
        out_shape=(col, col, row, row),
        grid=(batch,),
        in_specs=[pl.BlockSpec((seq, d), lambda b: (b, 0)),
                  pl.BlockSpec((d, LANES), lambda b: (0, 0)),
                  pl.BlockSpec((1, LANES), lambda b: (0, 0)),
                  pl.BlockSpec((1, LANES), lambda b: (0, 0))],
        out_specs=(pl.BlockSpec((seq, LANES), lambda b: (b, 0)),
                   pl.BlockSpec((seq, LANES), lambda b: (b, 0)),
                   pl.BlockSpec((1, LANES, seq), lambda b: (b, 0, 0)),
                   pl.BlockSpec((1, LANES, seq), lambda b: (b, 0, 0))),
        compiler_params=_params(("parallel",)),
        name="ssd_gate",
    )(x, w_dt, dt_bias, a_log)


def _expand_heads(cols, width):
    rows = cols.shape[0]
    lane = lax.broadcasted_iota(jnp.int32, (rows, SSD_HG * width), 1)
    out = jnp.broadcast_to(cols[:, SSD_HG - 1:SSD_HG], (rows, SSD_HG * width))
    for h in range(SSD_HG - 2, -1, -1):
        out = jnp.where(lane < (h + 1) * width, jnp.broadcast_to(cols[:, h:h + 1], out.shape), out)
    return out


def _ssd_body(z_ref, xr_ref, br_ref, cr_ref, cwx_ref, cwb_ref, cwc_ref, cbx_ref, cbb_ref, cbc_ref,
              dt4_ref, ac4_ref, dtt_ref, act_ref, dskip_ref, ng_ref, o_ref,
              padx_ref, padb_ref, xs_ref, bm_ref, cm_ref, st_ref, *, seq):
    ch = SSD_CHUNK
    hp = SSD_HG * SSD_P
    pad = SUBLANES

    def conv(raw_ref, pad_ref, w_ref, b_ref, dst_ref):
        pad_ref[0:pad, :] = jnp.zeros((pad, pad_ref.shape[1]), F32)
        pad_ref[pad:pad + seq, :] = raw_ref[...].astype(F32)
        tile = min(256, seq)
        for t0 in range(0, seq, tile):
            acc = jnp.broadcast_to(b_ref[...], (tile, pad_ref.shape[1]))
            for j in range(SSD_CONV):
                off = t0 + pad - (SSD_CONV - 1) + j
                acc = acc + w_ref[j:j + 1, :] * pad_ref[off:off + tile, :]
            dst_ref[t0:t0 + tile, :] = _silu(acc)

    conv(xr_ref, padx_ref, cwx_ref, cbx_ref, xs_ref)
    conv(br_ref, padb_ref, cwb_ref, cbb_ref, bm_ref)
    conv(cr_ref, padb_ref, cwc_ref, cbc_ref, cm_ref)

    lower = _tri(ch, lower=True)
    lane = lax.broadcasted_iota(jnp.int32, (1, hp), 1)
    st_ref[...] = jnp.zeros_like(st_ref)

    def chunk(c, carry):
        rows = pl.ds(pl.multiple_of(c * ch, ch), ch)
        xs = xs_ref[rows, :]
        bm = bm_ref[rows, :].astype(BF16)
        cm = cm_ref[rows, :].astype(BF16)
        dt4 = dt4_ref[0, 0, rows, :]
        ac4 = ac4_ref[0, 0, rows, :]
        to_end = jnp.exp(ac4[ch - 1:ch, :] - ac4)
        w_state = _expand_heads(dt4 * to_end, SSD_P)
        e_acum = _expand_heads(jnp.exp(ac4), SSD_P)
        cb = _dot_nt(cm, bm)
        xs_b = xs.astype(BF16)
        y = xs * dskip_ref[...]
        for h in range(SSD_HG):
            a_col = jnp.broadcast_to(ac4[:, h:h + 1], (ch, ch))
            a_row = act_ref[0, 0, h, pl.ds(c, 1), :]
            d_row = dtt_ref[0, 0, h, pl.ds(c, 1), :]
            decay = jnp.exp(jnp.where(lower, a_col - a_row, -jnp.inf))
            m_h = (cb * decay * d_row).astype(BF16)
            head = (lane >= h * SSD_P) & (lane < (h + 1) * SSD_P)
            y = y + _dot(m_h, jnp.where(head, xs_b, jnp.zeros_like(xs_b)))
        state = st_ref[...]
        y = y + _dot(cm, state.astype(BF16)) * e_acum
        st_ref[...] = state * e_acum[ch - 1:ch, :] + _dot_tn(bm, (xs * w_state).astype(BF16))
        y = y * _silu(z_ref[rows, :].astype(F32))
        ms = jnp.mean(y * y, axis=-1, keepdims=True)
        o_ref[rows, :] = (y * lax.rsqrt(ms + RMS_EPS) * ng_ref[...]).astype(o_ref.dtype)
        return carry

    lax.fori_loop(0, seq // ch, chunk, 0)


def _ssd_mix(proj, conv_w, conv_b, dt4, ac4, dtt, act, d_exp, norm_g, batch, seq):
    g = SSD_GROUPS
    hp = SSD_HG * SSD_P
    n = SSD_STATE
    nz = SSD_D_INNER // hp
    nb = 2 * SSD_D_INNER // n
    nc = nb + g
    cb0 = SSD_D_INNER // n
    nch = seq // SSD_CHUNK
    return pl.pallas_call(
        functools.partial(_ssd_body, seq=seq),
        out_shape=jax.ShapeDtypeStruct((batch * seq, SSD_D_INNER), BF16),
        grid=(batch, g),
        in_specs=[pl.BlockSpec((seq, hp), lambda b, i: (b, i)),
                  pl.BlockSpec((seq, hp), lambda b, i: (b, nz + i)),
                  pl.BlockSpec((seq, n), lambda b, i: (b, nb + i)),
                  pl.BlockSpec((seq, n), lambda b, i: (b, nc + i)),
                  pl.BlockSpec((SSD_CONV, hp), lambda b, i: (0, i)),
                  pl.BlockSpec((SSD_CONV, n), lambda b, i: (0, cb0 + i)),
                  pl.BlockSpec((SSD_CONV, n), lambda b, i: (0, cb0 + g + i)),
                  pl.BlockSpec((1, hp), lambda b, i: (0, i)),
                  pl.BlockSpec((1, n), lambda b, i: (0, cb0 + i)),
                  pl.BlockSpec((1, n), lambda b, i: (0, cb0 + g + i)),
                  pl.BlockSpec((1, 1, seq, SSD_HG), lambda b, i: (b, i, 0, 0)),
                  pl.BlockSpec((1, 1, seq, SSD_HG), lambda b, i: (b, i, 0, 0)),
                  pl.BlockSpec((1, 1, SSD_HG, nch, SSD_CHUNK), lambda b, i: (b, i, 0, 0, 0)),
                  pl.BlockSpec((1, 1, SSD_HG, nch, SSD_CHUNK), lambda b, i: (b, i, 0, 0, 0)),
                  pl.BlockSpec((1, hp), lambda b, i: (0, i)),
                  pl.BlockSpec((1, hp), lambda b, i: (0, i))],
        out_specs=pl.BlockSpec((seq, hp), lambda b, i: (b, i)),
        scratch_shapes=[pltpu.VMEM((seq + SUBLANES, hp), F32),
                        pltpu.VMEM((seq + SUBLANES, n), F32),
                        pltpu.VMEM((seq, hp), F32),
                        pltpu.VMEM((seq, n), F32),
                        pltpu.VMEM((seq, n), F32),
                        pltpu.VMEM((n, hp), F32)],
        compiler_params=_params(("parallel", "parallel")),
        name="ssd_mix",
    )(proj, proj, proj, proj, conv_w, conv_w, conv_w, conv_b, conv_b, conv_b,
      dt4, ac4, dtt, act, d_exp, norm_g)


def _ssd_layer(xf, xb, w_in, conv_w, conv_b, dt_bias, a_log, d_skip, norm_g, w_out, ln_g, ln_b,
               batch, seq):
    n_main = SSD_D_INNER + (SSD_D_INNER + 2 * SSD_GROUPS * SSD_STATE)
    proj = _matmul(xb, w_in[:, :n_main].astype(BF16), BF16, "ssd_proj")
    lane_pad = ((0, 0), (0, LANES - SSD_HEADS))
    w_dt = jnp.pad(w_in[:, n_main:], lane_pad)
    dt_c, ac_c, dt_r, ac_r = _ssd_gate(xf, w_dt, jnp.pad(dt_bias.reshape(1, -1), lane_pad),
                                       jnp.pad(a_log.reshape(1, -1), lane_pad), batch, seq)

    def cols(t):
        return t[:, :SSD_HEADS].reshape(batch, seq, SSD_GROUPS, SSD_HG).transpose(0, 2, 1, 3)

    def rows(t):
        return t[:, :SSD_HEADS].reshape(batch, SSD_GROUPS, SSD_HG, seq // SSD_CHUNK, SSD_CHUNK)

    d_exp = jnp.repeat(d_skip, SSD_P).reshape(1, SSD_D_INNER)
    y = _ssd_mix(proj, conv_w, conv_b.reshape(1, -1), cols(dt_c), cols(ac_c), rows(dt_r), rows(ac_r),
                 d_exp, norm_g.reshape(1, SSD_D_INNER), batch, seq)
    return _proj_ln(y, w_out.astype(BF16), xf, ln_g, ln_b, "ssd_out_ln")


def _router_body(x_ref, w_ref, bias_ref, e_ref, w_out_ref):
    tm = x_ref.shape[0]
    gsz = N_EXPERTS // N_EXPERT_GROUPS
    logits = _dot(x_ref[...], w_ref[...], HI).T[:N_EXPERTS, :]
    scores = _sigmoid(logits)
    sel = scores + bias_ref[...]
    sub = lax.broadcasted_iota(jnp.int32, (gsz, tm), 0).astype(F32)
    neg = jnp.float32(-jnp.inf)

    def top1(v, idx, sentinel):
        m = jnp.max(v, axis=0, keepdims=True)
        i = jnp.min(jnp.where(v == m, idx, sentinel), axis=0, keepdims=True)
        return m, i

    s_g = [sel[g * gsz:(g + 1) * gsz, :] for g in range(N_EXPERT_GROUPS)]
    sc_g = [scores[g * gsz:(g + 1) * gsz, :] for g in range(N_EXPERT_GROUPS)]
    grp_rows = []
    for g in range(N_EXPERT_GROUPS):
        m1, i1 = top1(s_g[g], sub, float(gsz))
        m2 = jnp.max(jnp.where(sub == i1, neg, s_g[g]), axis=0, keepdims=True)
        grp_rows.append(m1 + m2)
    grp = jnp.concatenate(grp_rows, axis=0)
    keep = jnp.zeros((N_EXPERT_GROUPS, tm), F32)
    for _ in range(TOPK_GROUPS):
        _, ig = top1(grp, sub, float(N_EXPERT_GROUPS))
        hit = sub == ig
        keep = jnp.where(hit, 1.0, keep)
        grp = jnp.where(hit, neg, grp)
    cand = [jnp.where(keep[g:g + 1, :] > 0.5, s_g[g], neg) for g in range(N_EXPERT_GROUPS)]
    ids = [sub + float(g * gsz) for g in range(N_EXPERT_GROUPS)]
    e_rows, w_rows = [], []
    for _ in range(TOP_K):
        m = cand[0]
        for g in range(1, N_EXPERT_GROUPS):
            m = jnp.maximum(m, cand[g])
        m = jnp.max(m, axis=0, keepdims=True)
        first = jnp.where(cand[0] == m, ids[0], float(N_EXPERTS))
        for g in range(1, N_EXPERT_GROUPS):
            first = jnp.minimum(first, jnp.where(cand[g] == m, ids[g], float(N_EXPERTS)))
        first = jnp.min(first, axis=0, keepdims=True)
        wsum = jnp.zeros((gsz, tm), F32)
        for g in range(N_EXPERT_GROUPS):
            hit = ids[g] == first
            wsum = wsum + jnp.where(hit, sc_g[g], 0.0)
            cand[g] = jnp.where(hit, neg, cand[g])
        e_rows.append(first)
        w_rows.append(jnp.sum(wsum, axis=0, keepdims=True))
    top_e = jnp.concatenate(e_rows, axis=0)
    top_w = jnp.concatenate(w_rows, axis=0)
    denom = jnp.sum(top_w, axis=0, keepdims=True)
    e_ref[...] = top_e.astype(jnp.int32)
    w_out_ref[...] = top_w / denom * ROUTED_SCALE


def _router(x, w_router, bias, tm=512):
    t, d = x.shape
    tm = min(tm, t)
    return pl.pallas_call(
        _router_body,
        out_shape=(jax.ShapeDtypeStruct((TOP_K, t), jnp.int32),
                   jax.ShapeDtypeStruct((TOP_K, t), F32)),
        grid=(t // tm,),
        in_specs=[pl.BlockSpec((tm, d), lambda i: (i, 0)),
                  pl.BlockSpec((d, LANES), lambda i: (0, 0)),
                  pl.BlockSpec((N_EXPERTS, 1), lambda i: (0, 0))],
        out_specs=(pl.BlockSpec((TOP_K, tm), lambda i: (0, i)),
                   pl.BlockSpec((TOP_K, tm), lambda i: (0, i))),
        compiler_params=_params(("parallel",)),
        name="moe_router",
    )(x, jnp.pad(w_router, ((0, 0), (0, LANES - N_EXPERTS))), bias.reshape(N_EXPERTS, 1))


def _moe_body(cnt_ref, off_ref, tok_ref, rw_ref,
              x_ref, wg_ref, wu_ref, wd_ref, sg_ref, su_ref, sd_ref, g_ref, b_ref,
              of_ref, ob_ref, buf_ref, y_ref, *, tb):
    blk = pl.program_id(0)
    e = pl.program_id(1)
    rows_per = MOE_ROWS
    tile = min(256, tb)

    @pl.when(e == 0)
    def _shared():
        buf_ref[...] = jnp.zeros_like(buf_ref)
        for t0 in range(0, tb, tile):
            xb = x_ref[t0:t0 + tile, :].astype(BF16)
            h = _silu(_dot(xb, sg_ref[...])) * _dot(xb, su_ref[...])
            of_ref[t0:t0 + tile, :] = _dot(h.astype(BF16), sd_ref[...])

    n = cnt_ref[blk * N_EXPERTS + e]
    base0 = off_ref[blk * N_EXPERTS + e]

    def chunk(c, carry):
        base = base0 + c * rows_per
        rows = jnp.minimum(rows_per, n - c * rows_per)
        full = rows // SUBLANES

        def gather8(jj, carry2):
            j0 = pl.multiple_of(jj * SUBLANES, SUBLANES)
            for u in range(SUBLANES):
                t = tok_ref[base + j0 + u]
                buf_ref[pl.ds(j0 + u, 1), :] = x_ref[pl.ds(t, 1), :]
            return carry2

        def gather1(j, carry2):
            t = tok_ref[base + j]
            buf_ref[pl.ds(j, 1), :] = x_ref[pl.ds(t, 1), :]
            return carry2

        lax.fori_loop(0, full, gather8, 0)
        lax.fori_loop(full * SUBLANES, rows, gather1, 0)

        xb = buf_ref[...].astype(BF16)
        h = _silu(_dot(xb, wg_ref[0])) * _dot(xb, wu_ref[0])
        y_ref[...] = _dot(h.astype(BF16), wd_ref[0])

        def scatter8(jj, carry2):
            j0 = pl.multiple_of(jj * SUBLANES, SUBLANES)
            toks = [tok_ref[base + j0 + u] for u in range(SUBLANES)]
            vals = [of_ref[pl.ds(toks[u], 1), :] + rw_ref[base + j0 + u] * y_ref[pl.ds(j0 + u, 1), :]
                    for u in range(SUBLANES)]
            for u in range(SUBLANES):
                of_ref[pl.ds(toks[u], 1), :] = vals[u]
            return carry2

        def scatter1(j, carry2):
            t = tok_ref[base + j]
            of_ref[pl.ds(t, 1), :] = of_ref[pl.ds(t, 1), :] + rw_ref[base + j] * y_ref[pl.ds(j, 1), :]
            return carry2

        lax.fori_loop(0, full, scatter8, 0)
        lax.fori_loop(full * SUBLANES, rows, scatter1, 0)
        return carry

    lax.fori_loop(0, (n + rows_per - 1) // rows_per, chunk, 0)

    @pl.when(e == N_EXPERTS - 1)
    def _finish():
        for t0 in range(0, tb, tile):
            z = DN_ALPHA * x_ref[t0:t0 + tile, :] + of_ref[t0:t0 + tile, :]
            out = _layer_norm_rows(z, g_ref[...], b_ref[...])
            of_ref[t0:t0 + tile, :] = out
            ob_ref[t0:t0 + tile, :] = out.astype(BF16)


def _moe_experts(x, cnt, off, tok, rw, wg, wu, wd, sg, su, sd, ln_g, ln_b, tb):
    t, d = x.shape
    nblk = t // tb
    ff = wg.shape[2]
    grid_spec = pltpu.PrefetchScalarGridSpec(
        num_scalar_prefetch=4,
        grid=(nblk, N_EXPERTS),
        in_specs=[pl.BlockSpec((tb, d), lambda i, e, *_: (i, 0)),
                  pl.BlockSpec((1, d, ff), lambda i, e, *_: (e, 0, 0)),
                  pl.BlockSpec((1, d, ff), lambda i, e, *_: (e, 0, 0)),
                  pl.BlockSpec((1, ff, d), lambda i, e, *_: (e, 0, 0)),
                  pl.BlockSpec((d, ff), lambda i, e, *_: (0, 0)),
                  pl.BlockSpec((d, ff), lambda i, e, *_: (0, 0)),
                  pl.BlockSpec((ff, d), lambda i, e, *_: (0, 0)),
                  pl.BlockSpec((1, d), lambda i, e, *_: (0, 0)),
                  pl.BlockSpec((1, d), lambda i, e, *_: (0, 0))],
        out_specs=(pl.BlockSpec((tb, d), lambda i, e, *_: (i, 0)),
                   pl.BlockSpec((tb, d), lambda i, e, *_: (i, 0))),
        scratch_shapes=[pltpu.VMEM((MOE_ROWS, d), F32), pltpu.VMEM((MOE_ROWS, d), F32)],
    )
    return pl.pallas_call(
        functools.partial(_moe_body, tb=tb),
        out_shape=(jax.ShapeDtypeStruct((t, d), F32), jax.ShapeDtypeStruct((t, d), BF16)),
        grid_spec=grid_spec,
        compiler_params=_params(("arbitrary", "arbitrary")),
        name="moe_experts",
    )(cnt, off, tok, rw, x, wg, wu, wd, sg, su, sd, ln_g.reshape(1, d), ln_b.reshape(1, d))


def _moe_layer(xf, w_router, router_bias, w_gate, w_up, w_down, ws_gate, ws_up, ws_down,
               ln_g, ln_b, tb, blocks_per_call):
    t, d = xf.shape
    top_e, top_w = _router(xf, w_router, router_bias)
    nblk = t // tb
    per = tb * TOP_K
    e_flat = top_e.T.reshape(nblk, per)
    w_flat = top_w.T.reshape(nblk, per)
    order = jnp.argsort(e_flat, axis=-1, stable=True)
    tok = (order // TOP_K).astype(jnp.int32)
    rw = jnp.take_along_axis(w_flat, order, axis=-1)
    cnt = jnp.sum(e_flat[:, :, None] == jnp.arange(N_EXPERTS, dtype=jnp.int32), axis=1,
                  dtype=jnp.int32)
    off = jnp.cumsum(cnt, axis=-1, dtype=jnp.int32) - cnt
    wg, wu, wd = w_gate.astype(BF16), w_up.astype(BF16), w_down.astype(BF16)
    sg, su, sd = ws_gate.astype(BF16), ws_up.astype(BF16), ws_down.astype(BF16)
    outs_f, outs_b = [], []
    for c0 in range(0, nblk, blocks_per_call):
        c1 = c0 + blocks_per_call
        off_c = off[c0:c1] + (jnp.arange(blocks_per_call, dtype=jnp.int32) * per)[:, None]
        of, ob = _moe_experts(xf[c0 * tb:c1 * tb], cnt[c0:c1].reshape(-1), off_c.reshape(-1),
                              tok[c0:c1].reshape(-1), rw[c0:c1].reshape(-1),
                              wg, wu, wd, sg, su, sd, ln_g, ln_b, tb)
        outs_f.append(of)
        outs_b.append(ob)
    if len(outs_f) == 1:
        return outs_f[0], outs_b[0]
    return jnp.concatenate(outs_f, axis=0), jnp.concatenate(outs_b, axis=0)


def kernel(x, fox_w_in, fox_b_f, fox_w_out, gla_w_in, gla_w_gate_up, gla_b_gate, gla_norm, gla_w_out, ssd_w_in, ssd_conv_w, ssd_conv_b, ssd_dt_bias, ssd_a_log, ssd_d, ssd_norm, ssd_w_out, ln1_g, ln1_b, moe_router, moe_router_bias, moe_w_gate, moe_w_up, moe_w_down, moe_ws_gate, moe_ws_up, moe_ws_down, ln2_g, ln2_b):
    batch, seq, d = x.shape
    xf = x.reshape(batch * seq, d)
    xb = xf.astype(BF16)
    tb = seq
    blocks_per_call = min(batch, 4)
    for i in range(DEPTH):
        kind, j = i % N_MIXERS, i // N_MIXERS
        if kind == 0:
            xf, xb = _fox_layer(xf, xb, fox_w_in[j], fox_b_f[j], fox_w_out[j], ln1_g[i], ln1_b[i],
                                batch, seq)
        elif kind == 1:
            xf, xb = _gla_layer(xf, xb, gla_w_in[j], gla_w_gate_up[j], gla_b_gate[j], gla_norm[j],
                                gla_w_out[j], ln1_g[i], ln1_b[i], batch, seq)
        else:
            xf, xb = _ssd_layer(xf, xb, ssd_w_in[j], ssd_conv_w[j], ssd_conv_b[j], ssd_dt_bias[j],
                                ssd_a_log[j], ssd_d[j], ssd_norm[j], ssd_w_out[j], ln1_g[i], ln1_b[i],
                                batch, seq)
        xf, xb = _moe_layer(xf, moe_router[i], moe_router_bias[i], moe_w_gate[i], moe_w_up[i],
                            moe_w_down[i], moe_ws_gate[i], moe_ws_up[i], moe_ws_down[i],
                            ln2_g[i], ln2_b[i], tb, blocks_per_call)
    return xf.reshape(batch, seq, d)
```

```python
import functools

import jax
import jax.numpy as jnp
from jax import lax
from jax.experimental import pallas as pl
from jax.experimental.pallas import tpu as pltpu

F32 = jnp.float32
BF16 = jnp.bfloat16
HI = lax.Precision.HIGHEST

D_MODEL = 1024
DEPTH = 4
N_MIXERS = 3
DN_ALPHA = (2 * DEPTH) ** 0.25
LN_EPS = 1e-5
RMS_EPS = 1e-6

FOX_HEADS = 16
FOX_HEAD_DIM = 64
GLA_HEADS = 4
GLA_DK = 512
GLA_DV = 1024
GLA_HK = 128
GLA_HV = 256
GLA_GATE_RANK = 16
GLA_GATE_TAU = 16.0
GLA_CHUNK = 64
SSD_D_INNER = 2048
SSD_HEADS = 32
SSD_GROUPS = 8
SSD_HG = 4
SSD_P = 64
SSD_STATE = 128
SSD_CONV = 4
SSD_CHUNK = 128
N_EXPERTS = 64
TOP_K = 8
N_EXPERT_GROUPS = 8
TOPK_GROUPS = 4
EXPERT_FF = 256
ROUTED_SCALE = 2.5

LANES = 128
SUBLANES = 8
VMEM_LIMIT = 56 * 2 ** 20
MOE_ROWS = 128


def _params(sem):
    return pltpu.CompilerParams(dimension_semantics=sem, vmem_limit_bytes=VMEM_LIMIT)


def _sigmoid(x):
    return 1.0 / (1.0 + jnp.exp(-x))


def _silu(x):
    return x * _sigmoid(x)


def _log_sigmoid(x):
    return jnp.minimum(x, 0.0) - jnp.log(1.0 + jnp.exp(-jnp.abs(x)))


def _softplus(x):
    return jnp.maximum(x, 0.0) + jnp.log(1.0 + jnp.exp(-jnp.abs(x)))


def _dot(a, b, precision=None):
    return jnp.dot(a, b, preferred_element_type=F32, precision=precision)


def _dot_nt(a, b, precision=None):
    return lax.dot_general(a, b, (((1,), (1,)), ((), ())), preferred_element_type=F32,
                           precision=precision)


def _dot_tn(a, b, precision=None):
    return lax.dot_general(a, b, (((0,), (0,)), ((), ())), preferred_element_type=F32,
                           precision=precision)


def _tri(n, lower):
    r = lax.broadcasted_iota(jnp.int32, (n, n), 0)
    c = lax.broadcasted_iota(jnp.int32, (n, n), 1)
    return (r >= c) if lower else (r <= c)


def _mm_body(x_ref, w_ref, o_ref, *, precision):
    o_ref[...] = _dot(x_ref[...], w_ref[...], precision).astype(o_ref.dtype)


def _matmul(x, w, out_dtype, name, tm=512, tn=512, precision=None):
    m, k = x.shape
    n = w.shape[1]
    tm, tn = min(tm, m), min(tn, n)
    return pl.pallas_call(
        functools.partial(_mm_body, precision=precision),
        out_shape=jax.ShapeDtypeStruct((m, n), out_dtype),
        grid=(n // tn, m // tm),
        in_specs=[pl.BlockSpec((tm, k), lambda j, i: (i, 0)),
                  pl.BlockSpec((k, tn), lambda j, i: (0, j))],
        out_specs=pl.BlockSpec((tm, tn), lambda j, i: (i, j)),
        compiler_params=_params(("parallel", "parallel")),
        name=name,
    )(x, w)


def _layer_norm_rows(z, g, b):
    mu = jnp.mean(z, axis=-1, keepdims=True)
    zc = z - mu
    var = jnp.mean(zc * zc, axis=-1, keepdims=True)
    return zc * lax.rsqrt(var + LN_EPS) * g + b


def _tok_rows(ref, t0, n):
    return jnp.concatenate(
        [ref[pl.ds(t0 * SUBLANES + k, n, stride=SUBLANES), :] for k in range(SUBLANES)], axis=1)


def _tok_store(ref, t0, val):
    for k in range(SUBLANES):
        ref[pl.ds(t0 * SUBLANES + k, val.shape[0], stride=SUBLANES), :] = val[:, k * LANES:(k + 1) * LANES]


def _proj_ln_body(o_ref, w_ref, x_ref, g_ref, b_ref, xf_ref, xb_ref, x8_ref):
    y = _dot(o_ref[...], w_ref[...])
    out = _layer_norm_rows(DN_ALPHA * x_ref[...] + y, g_ref[...], b_ref[...])
    xf_ref[...] = out
    xb_ref[...] = out.astype(BF16)
    _tok_store(x8_ref, 0, out)


def _proj_ln(o, w, x, g, b, name, tm=512):
    m, k = o.shape
    d = w.shape[1]
    assert d == SUBLANES * LANES
    tm = min(tm, m)
    return pl.pallas_call(
        _proj_ln_body,
        out_shape=(jax.ShapeDtypeStruct((m, d), F32), jax.ShapeDtypeStruct((m, d), BF16),
                   jax.ShapeDtypeStruct((m * SUBLANES, LANES), F32)),
        grid=(m // tm,),
        in_specs=[pl.BlockSpec((tm, k), lambda i: (i, 0)),
                  pl.BlockSpec((k, d), lambda i: (0, 0)),
                  pl.BlockSpec((tm, d), lambda i: (i, 0)),
                  pl.BlockSpec((1, d), lambda i: (0, 0)),
                  pl.BlockSpec((1, d), lambda i: (0, 0))],
        out_specs=(pl.BlockSpec((tm, d), lambda i: (i, 0)),
                   pl.BlockSpec((tm, d), lambda i: (i, 0)),
                   pl.BlockSpec((tm * SUBLANES, LANES), lambda i: (i, 0))),
        compiler_params=_params(("parallel",)),
        name=name,
    )(o, w, x, g.reshape(1, d), b.reshape(1, d))


def _fox_gate_body(x_ref, w_ref, b_ref, c_ref, *, seq):
    f = _dot(x_ref[...], w_ref[...], HI)
    z = f.T[:FOX_HEADS, :] + b_ref[...]
    lf = _log_sigmoid(z)
    upper = _tri(LANES, lower=False).astype(F32)
    carry = jnp.zeros((FOX_HEADS, 1), F32)
    for blk in range(seq // LANES):
        seg = _dot(lf[:, blk * LANES:(blk + 1) * LANES], upper, HI) + carry
        c_ref[0, :, blk * LANES:(blk + 1) * LANES] = seg
        carry = seg[:, LANES - 1:LANES]


def _fox_gate(x, w_f, b_f, batch, seq):
    d = x.shape[1]
    return pl.pallas_call(
        functools.partial(_fox_gate_body, seq=seq),
        out_shape=jax.ShapeDtypeStruct((batch, FOX_HEADS, seq), F32),
        grid=(batch,),
        in_specs=[pl.BlockSpec((seq, d), lambda b: (b, 0)),
                  pl.BlockSpec((d, LANES), lambda b: (0, 0)),
                  pl.BlockSpec((FOX_HEADS, 1), lambda b: (0, 0))],
        out_specs=pl.BlockSpec((1, FOX_HEADS, seq), lambda b: (b, 0, 0)),
        compiler_params=_params(("parallel",)),
        name="fox_gate",
    )(x, w_f, b_f)


def _fox_attn_body(q_ref, k_ref, v_ref, c_ref, o_ref, kh_ref, qh_ref, vt_ref, acc_ref, ml_ref, ot_ref,
                   *, seq, t):
    nq = seq // t
    hd = FOX_HEAD_DIM
    lane = lax.broadcasted_iota(jnp.int32, (1, LANES), 1)
    valid = _tri(t, lower=False)
    scale = hd ** -0.5
    for h in range(2):
        own = (lane >= h * hd) & (lane < (h + 1) * hd)
        aug = (1 - h) * hd
        c_col = c_ref[0, 0, :, h:h + 1]
        hi = c_col.astype(BF16).astype(F32)
        r1 = c_col - hi
        mid = r1.astype(BF16).astype(F32)
        lo = (r1 - mid).astype(BF16).astype(F32)
        sel = [(lane == aug + n).astype(F32) for n in range(3)]
        c_lanes = hi * sel[0] + mid * sel[1] + lo * sel[2]
        own_f = own.astype(F32)
        kh_ref[h] = (k_ref[...].astype(F32) * (scale * own_f) + c_lanes).astype(BF16)
        qh_ref[h] = (q_ref[...].astype(F32) * own_f - (sel[0] + sel[1] + sel[2])).astype(BF16)
    vt_ref[...] = v_ref[...].astype(F32).T.astype(BF16)

    def scores(tile):
        j, i, h = tile
        s = _dot_nt(kh_ref[h, j * t:(j + 1) * t, :], qh_ref[h, i * t:(i + 1) * t, :])
        return jnp.where(valid, s, -jnp.inf) if i == j else s

    def update(tile, s):
        j, i, h = tile
        rows = slice(h * hd, (h + 1) * hd)
        m_new = jnp.max(s, axis=0, keepdims=True)
        if j > 0:
            m_old = ml_ref[i, h:h + 1, :]
            m_new = jnp.maximum(m_old, m_new)
        p = jnp.exp(s - m_new)
        l_new = jnp.sum(p, axis=0, keepdims=True)
        acc = _dot(vt_ref[rows, j * t:(j + 1) * t], p.astype(BF16))
        if j > 0:
            a = jnp.exp(m_old - m_new)
            l_new = a * ml_ref[i, 2 + h:3 + h, :] + l_new
            acc = a * acc_ref[i, rows, :] + acc
        if i == j:
            ot_ref[rows, i * t:(i + 1) * t] = acc / l_new
        else:
            ml_ref[i, h:h + 1, :] = m_new
            ml_ref[i, 2 + h:3 + h, :] = l_new
            acc_ref[i, rows, :] = acc

    tiles = [(j, i, h) for j in range(nq) for i in range(j, nq) for h in range(2)]
    ahead = 2
    pending = []
    for idx in range(len(tiles) + ahead):
        if idx < len(tiles):
            pending.append(scores(tiles[idx]))
        if idx >= ahead:
            update(tiles[idx - ahead], pending.pop(0))
    o_ref[...] = ot_ref[...].T.astype(o_ref.dtype)


def _fox_attn(qkv, c, batch, seq, t=256):
    t = min(t, seq)
    pairs = FOX_HEADS // 2
    c_cols = c.reshape(batch, pairs, 2, seq).transpose(0, 1, 3, 2)
    return pl.pallas_call(
        functools.partial(_fox_attn_body, seq=seq, t=t),
        out_shape=jax.ShapeDtypeStruct((batch * seq, D_MODEL), BF16),
        grid=(batch, pairs),
        in_specs=[pl.BlockSpec((seq, LANES), lambda b, p: (b, p)),
                  pl.BlockSpec((seq, LANES), lambda b, p: (b, pairs + p)),
                  pl.BlockSpec((seq, LANES), lambda b, p: (b, 2 * pairs + p)),
                  pl.BlockSpec((1, 1, seq, 2), lambda b, p: (b, p, 0, 0))],
        out_specs=pl.BlockSpec((seq, LANES), lambda b, p: (b, p)),
        scratch_shapes=[pltpu.VMEM((2, seq, LANES), BF16),
                        pltpu.VMEM((2, seq, LANES), BF16),
                        pltpu.VMEM((LANES, seq), BF16),
                        pltpu.VMEM((seq // t, LANES, t), F32),
                        pltpu.VMEM((seq // t, SUBLANES, t), F32),
                        pltpu.VMEM((LANES, seq), F32)],
        compiler_params=_params(("parallel", "parallel")),
        name="fox_attn",
    )(qkv, qkv, qkv, c_cols)


def _fox_layer(xf, xb, w_in, b_f, w_out, ln_g, ln_b, batch, seq):
    d = D_MODEL
    qkv = _matmul(xb, w_in[:, :3 * d].astype(BF16), BF16, "fox_qkv")
    w_f = jnp.pad(w_in[:, 3 * d:], ((0, 0), (0, LANES - FOX_HEADS)))
    c = _fox_gate(xf, w_f, b_f.reshape(FOX_HEADS, 1), batch, seq)
    o = _fox_attn(qkv, c, batch, seq)
    return _proj_ln(o, w_out.astype(BF16), xf, ln_g, ln_b, "fox_out_ln")


def _gla_body(q_ref, k_ref, v_ref, r_ref, gl_ref, wgu_ref, bg_ref, ng_ref, o_ref, st_ref, *, seq):
    ch = GLA_CHUNK
    lower = _tri(ch, lower=True)
    lower_f = lower.astype(F32)
    st_ref[...] = jnp.zeros_like(st_ref)

    def chunk(i, carry):
        rows = pl.ds(pl.multiple_of(i * ch, ch), ch)
        gate = _dot(gl_ref[rows, :], wgu_ref[...], HI) + bg_ref[...]
        log_a = _log_sigmoid(gate) * (1.0 / GLA_GATE_TAU)
        bcum = _dot(lower_f, log_a, HI)
        b_last = bcum[ch - 1:ch, :]
        q = q_ref[rows, :].astype(F32) * (GLA_HK ** -0.5)
        k = k_ref[rows, :].astype(F32)
        v = v_ref[rows, :]
        q_dec = (q * jnp.exp(bcum)).astype(BF16)
        k_inv = (k * jnp.exp(-bcum)).astype(BF16)
        k_end = (k * jnp.exp(b_last - bcum)).astype(BF16)
        att = jnp.where(lower, _dot_nt(q_dec, k_inv), 0.0)
        state_t = st_ref[...]
        o = _dot(att.astype(BF16), v) + _dot_nt(q_dec, state_t.astype(BF16))
        st_ref[...] = state_t * jnp.exp(b_last) + _dot_tn(v, k_end)
        ms = jnp.mean(o * o, axis=-1, keepdims=True)
        on = o * lax.rsqrt(ms + RMS_EPS) * ng_ref[...]
        o_ref[rows, :] = (on * _silu(r_ref[rows, :].astype(F32))).astype(o_ref.dtype)
        return carry

    lax.fori_loop(0, seq // ch, chunk, 0)


def _gla_mix(proj, g_low, w_gu, b_gate, norm_g, batch, seq):
    nq = GLA_DK // GLA_HK
    nv = 2 * GLA_DK // GLA_HV
    nr = (2 * GLA_DK + GLA_DV) // GLA_HV
    return pl.pallas_call(
        functools.partial(_gla_body, seq=seq),
        out_shape=jax.ShapeDtypeStruct((batch * seq, GLA_DV), BF16),
        grid=(batch, GLA_HEADS),
        in_specs=[pl.BlockSpec((seq, GLA_HK), lambda b, h: (b, h)),
                  pl.BlockSpec((seq, GLA_HK), lambda b, h: (b, nq + h)),
                  pl.BlockSpec((seq, GLA_HV), lambda b, h: (b, nv + h)),
                  pl.BlockSpec((seq, GLA_HV), lambda b, h: (b, nr + h)),
                  pl.BlockSpec((seq, LANES), lambda b, h: (b, 0)),
                  pl.BlockSpec((LANES, GLA_HK), lambda b, h: (0, h)),
                  pl.BlockSpec((1, GLA_HK), lambda b, h: (0, h)),
                  pl.BlockSpec((1, GLA_HV), lambda b, h: (0, 0))],
        out_specs=pl.BlockSpec((seq, GLA_HV), lambda b, h: (b, h)),
        scratch_shapes=[pltpu.VMEM((GLA_HV, GLA_HK), F32)],
        compiler_params=_params(("parallel", "parallel")),
        name="gla_mix",
    )(proj, proj, proj, proj, g_low, w_gu, b_gate.reshape(1, GLA_DK), norm_g.reshape(1, GLA_HV))


def _gla_layer(xf, xb, w_in, w_gate_up, b_gate, norm_g, w_out, ln_g, ln_b, batch, seq):
    n_main = 2 * GLA_DK + 2 * GLA_DV
    proj = _matmul(xb, w_in[:, :n_main].astype(BF16), BF16, "gla_proj")
    w_g = jnp.pad(w_in[:, n_main:], ((0, 0), (0, LANES - GLA_GATE_RANK)))
    g_low = _matmul(xf, w_g, F32, "gla_gate", tn=LANES, precision=HI)
    w_gu = jnp.pad(w_gate_up, ((0, LANES - GLA_GATE_RANK), (0, 0)))
    y = _gla_mix(proj, g_low, w_gu, b_gate, norm_g, batch, seq)
    return _proj_ln(y, w_out.astype(BF16), xf, ln_g, ln_b, "gla_out_ln")


def _ssd_gate_body(x_ref, w_ref, bias_ref, alog_ref, dt_ref, ac_ref, dtt_ref, act_ref, *, seq):
    ch = SSD_CHUNK
    raw = _dot(x_ref[...], w_ref[...], HI)
    dt = _softplus(raw + bias_ref[...])
    a = dt * (-jnp.exp(alog_ref[...]))
    lower_f = _tri(ch, lower=True).astype(F32)
    dt_ref[...] = dt
    dtt_ref[0] = dt.T
    for c in range(seq // ch):
        acum = _dot(lower_f, a[c * ch:(c + 1) * ch, :], HI)
        ac_ref[c * ch:(c + 1) * ch, :] = acum
        act_ref[0, :, c * ch:(c + 1) * ch] = acum.T


def _ssd_gate(x, w_dt, dt_bias, a_log, batch, seq):
    d = x.shape[1]
    col = jax.ShapeDtypeStruct((batch * seq, LANES), F32)
    row = jax.ShapeDtypeStruct((batch, LANES, seq), F32)
    return pl.pallas_call(
        functools.partial(_ssd_gate_body, seq=seq),
        out_shape=(col, col, row, row),
        grid=(batch,),
        in_specs=[pl.BlockSpec((seq, d), lambda b: (b, 0)),
                  pl.BlockSpec((d, LANES), lambda b: (0, 0)),
                  pl.BlockSpec((1, LANES), lambda b: (0, 0)),
                  pl.BlockSpec((1, LANES), lambda b: (0, 0))],
        out_specs=(pl.BlockSpec((seq, LANES), lambda b: (b, 0)),
                   pl.BlockSpec((seq, LANES), lambda b: (b, 0)),
                   pl.BlockSpec((1, LANES, seq), lambda b: (b, 0, 0)),
                   pl.BlockSpec((1, LANES, seq), lambda b: (b, 0, 0))),
        compiler_params=_params(("parallel",)),
        name="ssd_gate",
    )(x, w_dt, dt_bias, a_log)


def _expand_heads(cols, width):
    rows = cols.shape[0]
    lane = lax.broadcasted_iota(jnp.int32, (rows, SSD_HG * width), 1)
    out = jnp.broadcast_to(cols[:, SSD_HG - 1:SSD_HG], (rows, SSD_HG * width))
    for h in range(SSD_HG - 2, -1, -1):
        out = jnp.where(lane < (h + 1) * width, jnp.broadcast_to(cols[:, h:h + 1], out.shape), out)
    return out


def _ssd_body(z_ref, xr_ref, br_ref, cr_ref, cwx_ref, cwb_ref, cwc_ref, cbx_ref, cbb_ref, cbc_ref,
              dt4_ref, ac4_ref, dtt_ref, act_ref, dskip_ref, ng_ref, o_ref,
              padx_ref, padb_ref, xs_ref, bm_ref, cm_ref, st_ref, *, seq):
    ch = SSD_CHUNK
    hp = SSD_HG * SSD_P
    pad = SUBLANES

    def conv(raw_ref, pad_ref, w_ref, b_ref, dst_ref):
        pad_ref[0:pad, :] = jnp.zeros((pad, pad_ref.shape[1]), F32)
        pad_ref[pad:pad + seq, :] = raw_ref[...].astype(F32)
        tile = min(256, seq)
        for t0 in range(0, seq, tile):
            acc = jnp.broadcast_to(b_ref[...], (tile, pad_ref.shape[1]))
            for j in range(SSD_CONV):
                off = t0 + pad - (SSD_CONV - 1) + j
                acc = acc + w_ref[j:j + 1, :] * pad_ref[off:off + tile, :]
            dst_ref[t0:t0 + tile, :] = _silu(acc)

    conv(xr_ref, padx_ref, cwx_ref, cbx_ref, xs_ref)
    conv(br_ref, padb_ref, cwb_ref, cbb_ref, bm_ref)
    conv(cr_ref, padb_ref, cwc_ref, cbc_ref, cm_ref)

    lower = _tri(ch, lower=True)
    lane = lax.broadcasted_iota(jnp.int32, (1, hp), 1)
    st_ref[...] = jnp.zeros_like(st_ref)

    def chunk(c, carry):
        rows = pl.ds(pl.multiple_of(c * ch, ch), ch)
        xs = xs_ref[rows, :]
        bm = bm_ref[rows, :].astype(BF16)
        cm = cm_ref[rows, :].astype(BF16)
        dt4 = dt4_ref[0, 0, rows, :]
        ac4 = ac4_ref[0, 0, rows, :]
        to_end = jnp.exp(ac4[ch - 1:ch, :] - ac4)
        w_state = _expand_heads(dt4 * to_end, SSD_P)
        e_acum = _expand_heads(jnp.exp(ac4), SSD_P)
        cb = _dot_nt(cm, bm)
        xs_b = xs.astype(BF16)
        y = xs * dskip_ref[...]
        for h in range(SSD_HG):
            a_col = jnp.broadcast_to(ac4[:, h:h + 1], (ch, ch))
            a_row = act_ref[0, 0, h, pl.ds(c, 1), :]
            d_row = dtt_ref[0, 0, h, pl.ds(c, 1), :]
            decay = jnp.exp(jnp.where(lower, a_col - a_row, -jnp.inf))
            m_h = (cb * decay * d_row).astype(BF16)
            head = (lane >= h * SSD_P) & (lane < (h + 1) * SSD_P)
            y = y + _dot(m_h, jnp.where(head, xs_b, jnp.zeros_like(xs_b)))
        state = st_ref[...]
        y = y + _dot(cm, state.astype(BF16)) * e_acum
        st_ref[...] = state * e_acum[ch - 1:ch, :] + _dot_tn(bm, (xs * w_state).astype(BF16))
        y = y * _silu(z_ref[rows, :].astype(F32))
        ms = jnp.mean(y * y, axis=-1, keepdims=True)
        o_ref[rows, :] = (y * lax.rsqrt(ms + RMS_EPS) * ng_ref[...]).astype(o_ref.dtype)
        return carry

    lax.fori_loop(0, seq // ch, chunk, 0)


def _ssd_mix(proj, conv_w, conv_b, dt4, ac4, dtt, act, d_exp, norm_g, batch, seq):
    g = SSD_GROUPS
    hp = SSD_HG * SSD_P
    n = SSD_STATE
    nz = SSD_D_INNER // hp
    nb = 2 * SSD_D_INNER // n
    nc = nb + g
    cb0 = SSD_D_INNER // n
    nch = seq // SSD_CHUNK
    return pl.pallas_call(
        functools.partial(_ssd_body, seq=seq),
        out_shape=jax.ShapeDtypeStruct((batch * seq, SSD_D_INNER), BF16),
        grid=(batch, g),
        in_specs=[pl.BlockSpec((seq, hp), lambda b, i: (b, i)),
                  pl.BlockSpec((seq, hp), lambda b, i: (b, nz + i)),
                  pl.BlockSpec((seq, n), lambda b, i: (b, nb + i)),
                  pl.BlockSpec((seq, n), lambda b, i: (b, nc + i)),
                  pl.BlockSpec((SSD_CONV, hp), lambda b, i: (0, i)),
                  pl.BlockSpec((SSD_CONV, n), lambda b, i: (0, cb0 + i)),
                  pl.BlockSpec((SSD_CONV, n), lambda b, i: (0, cb0 + g + i)),
                  pl.BlockSpec((1, hp), lambda b, i: (0, i)),
                  pl.BlockSpec((1, n), lambda b, i: (0, cb0 + i)),
                  pl.BlockSpec((1, n), lambda b, i: (0, cb0 + g + i)),
                  pl.BlockSpec((1, 1, seq, SSD_HG), lambda b, i: (b, i, 0, 0)),
                  pl.BlockSpec((1, 1, seq, SSD_HG), lambda b, i: (b, i, 0, 0)),
                  pl.BlockSpec((1, 1, SSD_HG, nch, SSD_CHUNK), lambda b, i: (b, i, 0, 0, 0)),
                  pl.BlockSpec((1, 1, SSD_HG, nch, SSD_CHUNK), lambda b, i: (b, i, 0, 0, 0)),
                  pl.BlockSpec((1, hp), lambda b, i: (0, i)),
                  pl.BlockSpec((1, hp), lambda b, i: (0, i))],
        out_specs=pl.BlockSpec((seq, hp), lambda b, i: (b, i)),
        scratch_shapes=[pltpu.VMEM((seq + SUBLANES, hp), F32),
                        pltpu.VMEM((seq + SUBLANES, n), F32),
                        pltpu.VMEM((seq, hp), F32),
                        pltpu.VMEM((seq, n), F32),
                        pltpu.VMEM((seq, n), F32),
                        pltpu.VMEM((n, hp), F32)],
        compiler_params=_params(("parallel", "parallel")),
        name="ssd_mix",
    )(proj, proj, proj, proj, conv_w, conv_w, conv_w, conv_b, conv_b, conv_b,
      dt4, ac4, dtt, act, d_exp, norm_g)


def _ssd_layer(xf, xb, w_in, conv_w, conv_b, dt_bias, a_log, d_skip, norm_g, w_out, ln_g, ln_b,
               batch, seq):
    n_main = SSD_D_INNER + (SSD_D_INNER + 2 * SSD_GROUPS * SSD_STATE)
    proj = _matmul(xb, w_in[:, :n_main].astype(BF16), BF16, "ssd_proj")
    lane_pad = ((0, 0), (0, LANES - SSD_HEADS))
    w_dt = jnp.pad(w_in[:, n_main:], lane_pad)
    dt_c, ac_c, dt_r, ac_r = _ssd_gate(xf, w_dt, jnp.pad(dt_bias.reshape(1, -1), lane_pad),
                                       jnp.pad(a_log.reshape(1, -1), lane_pad), batch, seq)

    def cols(t):
        return t[:, :SSD_HEADS].reshape(batch, seq, SSD_GROUPS, SSD_HG).transpose(0, 2, 1, 3)

    def rows(t):
        return t[:, :SSD_HEADS].reshape(batch, SSD_GROUPS, SSD_HG, seq // SSD_CHUNK, SSD_CHUNK)

    d_exp = jnp.repeat(d_skip, SSD_P).reshape(1, SSD_D_INNER)
    y = _ssd_mix(proj, conv_w, conv_b.reshape(1, -1), cols(dt_c), cols(ac_c), rows(dt_r), rows(ac_r),
                 d_exp, norm_g.reshape(1, SSD_D_INNER), batch, seq)
    return _proj_ln(y, w_out.astype(BF16), xf, ln_g, ln_b, "ssd_out_ln")


def _router_body(x_ref, w_ref, bias_ref, e_ref, w_out_ref):
    tm = x_ref.shape[0]
    gsz = N_EXPERTS // N_EXPERT_GROUPS
    logits = _dot(x_ref[...], w_ref[...], HI).T[:N_EXPERTS, :]
    scores = _sigmoid(logits)
    sel = scores + bias_ref[...]
    sub = lax.broadcasted_iota(jnp.int32, (gsz, tm), 0).astype(F32)
    neg = jnp.float32(-jnp.inf)

    def top1(v, idx, sentinel):
        m = jnp.max(v, axis=0, keepdims=True)
        i = jnp.min(jnp.where(v == m, idx, sentinel), axis=0, keepdims=True)
        return m, i

    s_g = [sel[g * gsz:(g + 1) * gsz, :] for g in range(N_EXPERT_GROUPS)]
    sc_g = [scores[g * gsz:(g + 1) * gsz, :] for g in range(N_EXPERT_GROUPS)]
    grp_rows = []
    for g in range(N_EXPERT_GROUPS):
        m1, i1 = top1(s_g[g], sub, float(gsz))
        m2 = jnp.max(jnp.where(sub == i1, neg, s_g[g]), axis=0, keepdims=True)
        grp_rows.append(m1 + m2)
    grp = jnp.concatenate(grp_rows, axis=0)
    keep = jnp.zeros((N_EXPERT_GROUPS, tm), F32)
    for _ in range(TOPK_GROUPS):
        _, ig = top1(grp, sub, float(N_EXPERT_GROUPS))
        hit = sub == ig
        keep = jnp.where(hit, 1.0, keep)
        grp = jnp.where(hit, neg, grp)
    cand = [jnp.where(keep[g:g + 1, :] > 0.5, s_g[g], neg) for g in range(N_EXPERT_GROUPS)]
    ids = [sub + float(g * gsz) for g in range(N_EXPERT_GROUPS)]
    e_rows, w_rows = [], []
    for _ in range(TOP_K):
        m = cand[0]
        for g in range(1, N_EXPERT_GROUPS):
            m = jnp.maximum(m, cand[g])
        m = jnp.max(m, axis=0, keepdims=True)
        first = jnp.where(cand[0] == m, ids[0], float(N_EXPERTS))
        for g in range(1, N_EXPERT_GROUPS):
            first = jnp.minimum(first, jnp.where(cand[g] == m, ids[g], float(N_EXPERTS)))
        first = jnp.min(first, axis=0, keepdims=True)
        wsum = jnp.zeros((gsz, tm), F32)
        for g in range(N_EXPERT_GROUPS):
            hit = ids[g] == first
            wsum = wsum + jnp.where(hit, sc_g[g], 0.0)
            cand[g] = jnp.where(hit, neg, cand[g])
        e_rows.append(first)
        w_rows.append(jnp.sum(wsum, axis=0, keepdims=True))
    top_e = jnp.concatenate(e_rows, axis=0)
    top_w = jnp.concatenate(w_rows, axis=0)
    denom = jnp.sum(top_w, axis=0, keepdims=True)
    e_ref[...] = top_e.astype(jnp.int32)
    w_out_ref[...] = top_w / denom * ROUTED_SCALE


def _router(x, w_router, bias, tm=512):
    t, d = x.shape
    tm = min(tm, t)
    return pl.pallas_call(
        _router_body,
        out_shape=(jax.ShapeDtypeStruct((TOP_K, t), jnp.int32),
                   jax.ShapeDtypeStruct((TOP_K, t), F32)),
        grid=(t // tm,),
        in_specs=[pl.BlockSpec((tm, d), lambda i: (i, 0)),
                  pl.BlockSpec((d, LANES), lambda i: (0, 0)),
                  pl.BlockSpec((N_EXPERTS, 1), lambda i: (0, 0))],
        out_specs=(pl.BlockSpec((TOP_K, tm), lambda i: (0, i)),
                   pl.BlockSpec((TOP_K, tm), lambda i: (0, i))),
        compiler_params=_params(("parallel",)),
        name="moe_router",
    )(x, jnp.pad(w_router, ((0, 0), (0, LANES - N_EXPERTS))), bias.reshape(N_EXPERTS, 1))


def _moe_body(cnt_ref, off_ref, tok_ref, rw_ref,
              x8_ref, wg_ref, wu_ref, wd_ref, sg_ref, su_ref, sd_ref,
              acc_ref, buf_ref, y_ref, *, tb):
    blk = pl.program_id(0)
    e = pl.program_id(1)
    rows_per = MOE_ROWS
    tile = min(256, tb)

    def token(ref, t):
        return ref.at[pl.ds(pl.multiple_of(t * SUBLANES, SUBLANES), SUBLANES), :]

    @pl.when(e == 0)
    def _shared():
        buf_ref[...] = jnp.zeros_like(buf_ref)
        for t0 in range(0, tb, tile):
            xb = _tok_rows(x8_ref, t0, tile).astype(BF16)
            h = _silu(_dot(xb, sg_ref[...])) * _dot(xb, su_ref[...])
            _tok_store(acc_ref, t0, _dot(h.astype(BF16), sd_ref[...]))

    n = cnt_ref[blk * N_EXPERTS + e]
    base0 = off_ref[blk * N_EXPERTS + e]

    def chunk(c, carry):
        base = base0 + c * rows_per
        rows = jnp.minimum(rows_per, n - c * rows_per)
        full = rows // SUBLANES

        def gather8(jj, carry2):
            j0 = jj * SUBLANES
            for u in range(SUBLANES):
                token(buf_ref, j0 + u)[...] = token(x8_ref, tok_ref[base + j0 + u])[...]
            return carry2

        def gather1(j, carry2):
            token(buf_ref, j)[...] = token(x8_ref, tok_ref[base + j])[...]
            return carry2

        lax.fori_loop(0, full, gather8, 0)
        lax.fori_loop(full * SUBLANES, rows, gather1, 0)

        xb = _tok_rows(buf_ref, 0, rows_per).astype(BF16)
        h = _silu(_dot(xb, wg_ref[0])) * _dot(xb, wu_ref[0])
        _tok_store(y_ref, 0, _dot(h.astype(BF16), wd_ref[0]))

        def scatter8(jj, carry2):
            j0 = jj * SUBLANES
            toks = [tok_ref[base + j0 + u] for u in range(SUBLANES)]
            vals = [token(acc_ref, toks[u])[...] + rw_ref[base + j0 + u] * token(y_ref, j0 + u)[...]
                    for u in range(SUBLANES)]
            for u in range(SUBLANES):
                token(acc_ref, toks[u])[...] = vals[u]
            return carry2

        def scatter1(j, carry2):
            t = tok_ref[base + j]
            token(acc_ref, t)[...] = token(acc_ref, t)[...] + rw_ref[base + j] * token(y_ref, j)[...]
            return carry2

        lax.fori_loop(0, full, scatter8, 0)
        lax.fori_loop(full * SUBLANES, rows, scatter1, 0)
        return carry

    lax.fori_loop(0, (n + rows_per - 1) // rows_per, chunk, 0)


def _moe_experts(x8, cnt, off, tok, rw, wg, wu, wd, sg, su, sd, tb):
    d = wg.shape[1]
    ff = wg.shape[2]
    nblk = x8.shape[0] // (tb * SUBLANES)
    grid_spec = pltpu.PrefetchScalarGridSpec(
        num_scalar_prefetch=4,
        grid=(nblk, N_EXPERTS),
        in_specs=[pl.BlockSpec((tb * SUBLANES, LANES), lambda i, e, *_: (i, 0)),
                  pl.BlockSpec((1, d, ff), lambda i, e, *_: (e, 0, 0)),
                  pl.BlockSpec((1, d, ff), lambda i, e, *_: (e, 0, 0)),
                  pl.BlockSpec((1, ff, d), lambda i, e, *_: (e, 0, 0)),
                  pl.BlockSpec((d, ff), lambda i, e, *_: (0, 0)),
                  pl.BlockSpec((d, ff), lambda i, e, *_: (0, 0)),
                  pl.BlockSpec((ff, d), lambda i, e, *_: (0, 0))],
        out_specs=pl.BlockSpec((tb * SUBLANES, LANES), lambda i, e, *_: (i, 0)),
        scratch_shapes=[pltpu.VMEM((MOE_ROWS * SUBLANES, LANES), F32),
                        pltpu.VMEM((MOE_ROWS * SUBLANES, LANES), F32)],
    )
    return pl.pallas_call(
        functools.partial(_moe_body, tb=tb),
        out_shape=jax.ShapeDtypeStruct(x8.shape, F32),
        grid_spec=grid_spec,
        compiler_params=_params(("arbitrary", "arbitrary")),
        name="moe_experts",
    )(cnt, off, tok, rw, x8, wg, wu, wd, sg, su, sd)


def _moe_finish_body(x_ref, acc_ref, g_ref, b_ref, xf_ref, xb_ref):
    z = DN_ALPHA * x_ref[...] + _tok_rows(acc_ref, 0, x_ref.shape[0])
    out = _layer_norm_rows(z, g_ref[...], b_ref[...])
    xf_ref[...] = out
    xb_ref[...] = out.astype(BF16)


def _moe_finish(x, acc8, g, b, tm=512):
    m, d = x.shape
    tm = min(tm, m)
    return pl.pallas_call(
        _moe_finish_body,
        out_shape=(jax.ShapeDtypeStruct((m, d), F32), jax.ShapeDtypeStruct((m, d), BF16)),
        grid=(m // tm,),
        in_specs=[pl.BlockSpec((tm, d), lambda i: (i, 0)),
                  pl.BlockSpec((tm * SUBLANES, LANES), lambda i: (i, 0)),
                  pl.BlockSpec((1, d), lambda i: (0, 0)),
                  pl.BlockSpec((1, d), lambda i: (0, 0))],
        out_specs=(pl.BlockSpec((tm, d), lambda i: (i, 0)),
                   pl.BlockSpec((tm, d), lambda i: (i, 0))),
        compiler_params=_params(("parallel",)),
        name="moe_finish",
    )(x, acc8, g.reshape(1, d), b.reshape(1, d))


def _moe_layer(xf, x8, w_router, router_bias, w_gate, w_up, w_down, ws_gate, ws_up, ws_down,
               ln_g, ln_b, tb, blocks_per_call):
    t, d = xf.shape
    top_e, top_w = _router(xf, w_router, router_bias)
    nblk = t // tb
    per = tb * TOP_K
    e_flat = top_e.T.reshape(nblk, per)
    w_flat = top_w.T.reshape(nblk, per)
    order = jnp.argsort(e_flat, axis=-1, stable=True)
    tok = (order // TOP_K).astype(jnp.int32)
    rw = jnp.take_along_axis(w_flat, order, axis=-1)
    cnt = jnp.sum(e_flat[:, :, None] == jnp.arange(N_EXPERTS, dtype=jnp.int32), axis=1,
                  dtype=jnp.int32)
    off = jnp.cumsum(cnt, axis=-1, dtype=jnp.int32) - cnt
    wg, wu, wd = w_gate.astype(BF16), w_up.astype(BF16), w_down.astype(BF16)
    sg, su, sd = ws_gate.astype(BF16), ws_up.astype(BF16), ws_down.astype(BF16)
    accs = []
    for c0 in range(0, nblk, blocks_per_call):
        c1 = c0 + blocks_per_call
        off_c = off[c0:c1] + (jnp.arange(blocks_per_call, dtype=jnp.int32) * per)[:, None]
        accs.append(_moe_experts(x8[c0 * tb * SUBLANES:c1 * tb * SUBLANES], cnt[c0:c1].reshape(-1),
                                 off_c.reshape(-1), tok[c0:c1].reshape(-1), rw[c0:c1].reshape(-1),
                                 wg, wu, wd, sg, su, sd, tb))
    acc8 = accs[0] if len(accs) == 1 else jnp.concatenate(accs, axis=0)
    return _moe_finish(xf, acc8, ln_g, ln_b)


def kernel(x, fox_w_in, fox_b_f, fox_w_out, gla_w_in, gla_w_gate_up, gla_b_gate, gla_norm, gla_w_out, ssd_w_in, ssd_conv_w, ssd_conv_b, ssd_dt_bias, ssd_a_log, ssd_d, ssd_norm, ssd_w_out, ln1_g, ln1_b, moe_router, moe_router_bias, moe_w_gate, moe_w_up, moe_w_down, moe_ws_gate, moe_ws_up, moe_ws_down, ln2_g, ln2_b):
    batch, seq, d = x.shape
    xf = x.reshape(batch * seq, d)
    xb = xf.astype(BF16)
    tb = seq
    blocks_per_call = min(batch, 4)
    for i in range(DEPTH):
        kind, j = i % N_MIXERS, i // N_MIXERS
        if kind == 0:
            xf, xb, x8 = _fox_layer(xf, xb, fox_w_in[j], fox_b_f[j], fox_w_out[j], ln1_g[i], ln1_b[i],
                                    batch, seq)
        elif kind == 1:
            xf, xb, x8 = _gla_layer(xf, xb, gla_w_in[j], gla_w_gate_up[j], gla_b_gate[j], gla_norm[j],
                                    gla_w_out[j], ln1_g[i], ln1_b[i], batch, seq)
        else:
            xf, xb, x8 = _ssd_layer(xf, xb, ssd_w_in[j], ssd_conv_w[j], ssd_conv_b[j], ssd_dt_bias[j],
                                    ssd_a_log[j], ssd_d[j], ssd_norm[j], ssd_w_out[j], ln1_g[i],
                                    ln1_b[i], batch, seq)
        xf, xb = _moe_layer(xf, x8, moe_router[i], moe_router_bias[i], moe_w_gate[i], moe_w_up[i],
                            moe_w_down[i], moe_ws_gate[i], moe_ws_up[i], moe_ws_down[i],
                            ln2_g[i], ln2_b[i], tb, blocks_per_call)
    return xf.reshape(batch, seq, d)
```

```python
import functools

import jax
import jax.numpy as jnp
from jax import lax
from jax.experimental import pallas as pl
from jax.experimental.pallas import tpu as pltpu

F32 = jnp.float32
BF16 = jnp.bfloat16
HI = lax.Precision.HIGHEST

D_MODEL = 1024
DEPTH = 4
N_MIXERS = 3
DN_ALPHA = (2 * DEPTH) ** 0.25
LN_EPS = 1e-5
RMS_EPS = 1e-6

FOX_HEADS = 16
FOX_HEAD_DIM = 64
GLA_HEADS = 4
GLA_DK = 512
GLA_DV = 1024
GLA_HK = 128
GLA_HV = 256
GLA_GATE_RANK = 16
GLA_GATE_TAU = 16.0
GLA_CHUNK = 64
SSD_D_INNER = 2048
SSD_HEADS = 32
SSD_GROUPS = 8
SSD_HG = 4
SSD_P = 64
SSD_STATE = 128
SSD_CONV = 4
SSD_CHUNK = 128
N_EXPERTS = 64
TOP_K = 8
N_EXPERT_GROUPS = 8
TOPK_GROUPS = 4
EXPERT_FF = 256
ROUTED_SCALE = 2.5

LANES = 128
SUBLANES = 8
VMEM_LIMIT = 56 * 2 ** 20
MOE_ROWS = 128
MOE_LIST_ALIGN = 1024


def _params(sem):
    return pltpu.CompilerParams(dimension_semantics=sem, vmem_limit_bytes=VMEM_LIMIT)


def _sigmoid(x):
    return 1.0 / (1.0 + jnp.exp(-x))


def _silu(x):
    return x * _sigmoid(x)


def _log_sigmoid(x):
    return jnp.minimum(x, 0.0) - jnp.log(1.0 + jnp.exp(-jnp.abs(x)))


def _softplus(x):
    return jnp.maximum(x, 0.0) + jnp.log(1.0 + jnp.exp(-jnp.abs(x)))


def _dot(a, b, precision=None):
    return jnp.dot(a, b, preferred_element_type=F32, precision=precision)


def _dot_nt(a, b, precision=None):
    return lax.dot_general(a, b, (((1,), (1,)), ((), ())), preferred_element_type=F32,
                           precision=precision)


def _dot_tn(a, b, precision=None):
    return lax.dot_general(a, b, (((0,), (0,)), ((), ())), preferred_element_type=F32,
                           precision=precision)


def _tri(n, lower):
    r = lax.broadcasted_iota(jnp.int32, (n, n), 0)
    c = lax.broadcasted_iota(jnp.int32, (n, n), 1)
    return (r >= c) if lower else (r <= c)


def _mm_body(x_ref, w_ref, o_ref, *, precision):
    o_ref[...] = _dot(x_ref[...], w_ref[...], precision).astype(o_ref.dtype)


def _matmul(x, w, out_dtype, name, tm=1024, tn=1024, precision=None):
    m, k = x.shape
    n = w.shape[1]
    tm, tn = min(tm, m), min(tn, n)
    return pl.pallas_call(
        functools.partial(_mm_body, precision=precision),
        out_shape=jax.ShapeDtypeStruct((m, n), out_dtype),
        grid=(n // tn, m // tm),
        in_specs=[pl.BlockSpec((tm, k), lambda j, i: (i, 0)),
                  pl.BlockSpec((k, tn), lambda j, i: (0, j))],
        out_specs=pl.BlockSpec((tm, tn), lambda j, i: (i, j)),
        compiler_params=_params(("parallel", "parallel")),
        name=name,
    )(x, w)


def _layer_norm_rows(z, g, b):
    mu = jnp.mean(z, axis=-1, keepdims=True)
    zc = z - mu
    var = jnp.mean(zc * zc, axis=-1, keepdims=True)
    return zc * lax.rsqrt(var + LN_EPS) * g + b


def _tok_rows(ref, t0, n):
    return jnp.concatenate(
        [ref[pl.ds(t0 * SUBLANES + k, n, stride=SUBLANES), :] for k in range(SUBLANES)], axis=1)


def _tok_store(ref, t0, val):
    for k in range(SUBLANES):
        ref[pl.ds(t0 * SUBLANES + k, val.shape[0], stride=SUBLANES), :] = val[:, k * LANES:(k + 1) * LANES]


def _proj_ln_body(o_ref, w_ref, x_ref, g_ref, b_ref, xf_ref, xb_ref, x8_ref):
    y = _dot(o_ref[...], w_ref[...])
    out = _layer_norm_rows(DN_ALPHA * x_ref[...] + y, g_ref[...], b_ref[...])
    xf_ref[...] = out
    xb_ref[...] = out.astype(BF16)
    _tok_store(x8_ref, 0, out)


def _proj_ln(o, w, x, g, b, name, tm=512):
    m, k = o.shape
    d = w.shape[1]
    assert d == SUBLANES * LANES
    tm = min(tm, m)
    return pl.pallas_call(
        _proj_ln_body,
        out_shape=(jax.ShapeDtypeStruct((m, d), F32), jax.ShapeDtypeStruct((m, d), BF16),
                   jax.ShapeDtypeStruct((m * SUBLANES, LANES), F32)),
        grid=(m // tm,),
        in_specs=[pl.BlockSpec((tm, k), lambda i: (i, 0)),
                  pl.BlockSpec((k, d), lambda i: (0, 0)),
                  pl.BlockSpec((tm, d), lambda i: (i, 0)),
                  pl.BlockSpec((1, d), lambda i: (0, 0)),
                  pl.BlockSpec((1, d), lambda i: (0, 0))],
        out_specs=(pl.BlockSpec((tm, d), lambda i: (i, 0)),
                   pl.BlockSpec((tm, d), lambda i: (i, 0)),
                   pl.BlockSpec((tm * SUBLANES, LANES), lambda i: (i, 0))),
        compiler_params=_params(("parallel",)),
        name=name,
    )(o, w, x, g.reshape(1, d), b.reshape(1, d))


def _fox_gate_body(x_ref, w_ref, b_ref, c_ref, *, seq):
    f = _dot(x_ref[...], w_ref[...], HI)
    z = f.T[:FOX_HEADS, :] + b_ref[...]
    lf = _log_sigmoid(z)
    upper = _tri(LANES, lower=False).astype(F32)
    carry = jnp.zeros((FOX_HEADS, 1), F32)
    for blk in range(seq // LANES):
        seg = _dot(lf[:, blk * LANES:(blk + 1) * LANES], upper, HI) + carry
        c_ref[0, :, blk * LANES:(blk + 1) * LANES] = seg
        carry = seg[:, LANES - 1:LANES]


def _fox_gate(x, w_f, b_f, batch, seq):
    d = x.shape[1]
    return pl.pallas_call(
        functools.partial(_fox_gate_body, seq=seq),
        out_shape=jax.ShapeDtypeStruct((batch, FOX_HEADS, seq), F32),
        grid=(batch,),
        in_specs=[pl.BlockSpec((seq, d), lambda b: (b, 0)),
                  pl.BlockSpec((d, LANES), lambda b: (0, 0)),
                  pl.BlockSpec((FOX_HEADS, 1), lambda b: (0, 0))],
        out_specs=pl.BlockSpec((1, FOX_HEADS, seq), lambda b: (b, 0, 0)),
        compiler_params=_params(("parallel",)),
        name="fox_gate",
    )(x, w_f, b_f)


def _fox_attn_body(q_ref, k_ref, v_ref, c_ref, o_ref, kh_ref, qh_ref, vt_ref, acc_ref, ml_ref, ot_ref,
                   *, seq, t):
    nq = seq // t
    hd = FOX_HEAD_DIM
    lane = lax.broadcasted_iota(jnp.int32, (1, LANES), 1)
    valid = _tri(t, lower=False)
    scale = hd ** -0.5
    for h in range(2):
        own = (lane >= h * hd) & (lane < (h + 1) * hd)
        aug = (1 - h) * hd
        c_col = c_ref[0, 0, :, h:h + 1]
        hi = c_col.astype(BF16).astype(F32)
        r1 = c_col - hi
        mid = r1.astype(BF16).astype(F32)
        lo = (r1 - mid).astype(BF16).astype(F32)
        sel = [(lane == aug + n).astype(F32) for n in range(3)]
        c_lanes = hi * sel[0] + mid * sel[1] + lo * sel[2]
        own_f = own.astype(F32)
        kh_ref[h] = (k_ref[...].astype(F32) * (scale * own_f) + c_lanes).astype(BF16)
        qh_ref[h] = (q_ref[...].astype(F32) * own_f - (sel[0] + sel[1] + sel[2])).astype(BF16)
    vt_ref[...] = v_ref[...].astype(F32).T.astype(BF16)

    def scores(tile):
        j, i, h = tile
        s = _dot_nt(kh_ref[h, j * t:(j + 1) * t, :], qh_ref[h, i * t:(i + 1) * t, :])
        return jnp.where(valid, s, -jnp.inf) if i == j else s

    def update(tile, s):
        j, i, h = tile
        rows = slice(h * hd, (h + 1) * hd)
        m_new = jnp.max(s, axis=0, keepdims=True)
        if j > 0:
            m_old = ml_ref[i, h:h + 1, :]
            m_new = jnp.maximum(m_old, m_new)
        p = jnp.exp(s - m_new)
        l_new = jnp.sum(p, axis=0, keepdims=True)
        acc = _dot(vt_ref[rows, j * t:(j + 1) * t], p.astype(BF16))
        if j > 0:
            a = jnp.exp(m_old - m_new)
            l_new = a * ml_ref[i, 2 + h:3 + h, :] + l_new
            acc = a * acc_ref[i, rows, :] + acc
        if i == j:
            ot_ref[rows, i * t:(i + 1) * t] = acc / l_new
        else:
            ml_ref[i, h:h + 1, :] = m_new
            ml_ref[i, 2 + h:3 + h, :] = l_new
            acc_ref[i, rows, :] = acc

    tiles = [(j, i, h) for j in range(nq) for i in range(j, nq) for h in range(2)]
    ahead = 2
    pending = []
    for idx in range(len(tiles) + ahead):
        if idx < len(tiles):
            pending.append(scores(tiles[idx]))
        if idx >= ahead:
            update(tiles[idx - ahead], pending.pop(0))
    o_ref[...] = ot_ref[...].T.astype(o_ref.dtype)


def _fox_attn(qkv, c, batch, seq, t=256):
    t = min(t, seq)
    pairs = FOX_HEADS // 2
    c_cols = c.reshape(batch, pairs, 2, seq).transpose(0, 1, 3, 2)
    return pl.pallas_call(
        functools.partial(_fox_attn_body, seq=seq, t=t),
        out_shape=jax.ShapeDtypeStruct((batch * seq, D_MODEL), BF16),
        grid=(batch, pairs),
        in_specs=[pl.BlockSpec((seq, LANES), lambda b, p: (b, p)),
                  pl.BlockSpec((seq, LANES), lambda b, p: (b, pairs + p)),
                  pl.BlockSpec((seq, LANES), lambda b, p: (b, 2 * pairs + p)),
                  pl.BlockSpec((1, 1, seq, 2), lambda b, p: (b, p, 0, 0))],
        out_specs=pl.BlockSpec((seq, LANES), lambda b, p: (b, p)),
        scratch_shapes=[pltpu.VMEM((2, seq, LANES), BF16),
                        pltpu.VMEM((2, seq, LANES), BF16),
                        pltpu.VMEM((LANES, seq), BF16),
                        pltpu.VMEM((seq // t, LANES, t), F32),
                        pltpu.VMEM((seq // t, SUBLANES, t), F32),
                        pltpu.VMEM((LANES, seq), F32)],
        compiler_params=_params(("parallel", "parallel")),
        name="fox_attn",
    )(qkv, qkv, qkv, c_cols)


def _fox_layer(xf, xb, w_in, b_f, w_out, ln_g, ln_b, batch, seq):
    d = D_MODEL
    qkv = _matmul(xb, w_in[:, :3 * d].astype(BF16), BF16, "fox_qkv")
    w_f = jnp.pad(w_in[:, 3 * d:], ((0, 0), (0, LANES - FOX_HEADS)))
    c = _fox_gate(xf, w_f, b_f.reshape(FOX_HEADS, 1), batch, seq)
    o = _fox_attn(qkv, c, batch, seq)
    return _proj_ln(o, w_out.astype(BF16), xf, ln_g, ln_b, "fox_out_ln")


def _gla_body(q_ref, k_ref, v_ref, r_ref, gl_ref, wgu_ref, bg_ref, ng_ref, o_ref, bc_ref, st_ref, *, seq):
    ch = GLA_CHUNK
    tile = min(256, seq)
    r_i = lax.broadcasted_iota(jnp.int32, (tile, tile), 0)
    c_i = lax.broadcasted_iota(jnp.int32, (tile, tile), 1)
    same_chunk_lower = ((r_i >= c_i) & (r_i // ch == c_i // ch)).astype(F32)
    for t0 in range(0, seq, tile):
        gate = _dot(gl_ref[t0:t0 + tile, :], wgu_ref[...], HI) + bg_ref[...]
        log_a = _log_sigmoid(gate) * (1.0 / GLA_GATE_TAU)
        bc_ref[t0:t0 + tile, :] = _dot(same_chunk_lower, log_a, HI)
    lower = _tri(ch, lower=True)
    st_ref[...] = jnp.zeros_like(st_ref)
    heads = range(GLA_HEADS)

    def chunk(i, carry):
        rows = pl.ds(pl.multiple_of(i * ch, ch), ch)
        bcum = bc_ref[rows, :]
        b_last = bcum[ch - 1:ch, :]
        q = q_ref[rows, :].astype(F32) * (GLA_HK ** -0.5)
        k = k_ref[rows, :].astype(F32)
        q_dec = (q * jnp.exp(bcum)).astype(BF16)
        k_inv = (k * jnp.exp(-bcum)).astype(BF16)
        k_end = (k * jnp.exp(b_last - bcum)).astype(BF16)
        decay = jnp.exp(b_last)
        kl = [slice(h * GLA_HK, (h + 1) * GLA_HK) for h in heads]
        vl = [slice(h * GLA_HV, (h + 1) * GLA_HV) for h in heads]
        v = [v_ref[rows, vl[h]] for h in heads]
        att = [jnp.where(lower, _dot_nt(q_dec[:, kl[h]], k_inv[:, kl[h]]), 0.0).astype(BF16)
               for h in heads]
        state_t = [st_ref[h] for h in heads]
        o_inter = [_dot_nt(q_dec[:, kl[h]], state_t[h].astype(BF16)) for h in heads]
        o_intra = [_dot(att[h], v[h]) for h in heads]
        kv_t = [_dot_tn(v[h], k_end[:, kl[h]]) for h in heads]
        for h in heads:
            st_ref[h] = state_t[h] * decay[:, kl[h]] + kv_t[h]
            o = o_intra[h] + o_inter[h]
            ms = jnp.mean(o * o, axis=-1, keepdims=True)
            on = o * lax.rsqrt(ms + RMS_EPS) * ng_ref[...]
            o_ref[rows, vl[h]] = (on * _silu(r_ref[rows, vl[h]].astype(F32))).astype(o_ref.dtype)
        return carry

    lax.fori_loop(0, seq // ch, chunk, 0)


def _gla_mix(proj, g_low, w_gu, b_gate, norm_g, batch, seq):
    return pl.pallas_call(
        functools.partial(_gla_body, seq=seq),
        out_shape=jax.ShapeDtypeStruct((batch * seq, GLA_DV), BF16),
        grid=(batch,),
        in_specs=[pl.BlockSpec((seq, GLA_DK), lambda b: (b, 0)),
                  pl.BlockSpec((seq, GLA_DK), lambda b: (b, 1)),
                  pl.BlockSpec((seq, GLA_DV), lambda b: (b, 1)),
                  pl.BlockSpec((seq, GLA_DV), lambda b: (b, 2)),
                  pl.BlockSpec((seq, LANES), lambda b: (b, 0)),
                  pl.BlockSpec((LANES, GLA_DK), lambda b: (0, 0)),
                  pl.BlockSpec((1, GLA_DK), lambda b: (0, 0)),
                  pl.BlockSpec((1, GLA_HV), lambda b: (0, 0))],
        out_specs=pl.BlockSpec((seq, GLA_DV), lambda b: (b, 0)),
        scratch_shapes=[pltpu.VMEM((seq, GLA_DK), F32),
                        pltpu.VMEM((GLA_HEADS, GLA_HV, GLA_HK), F32)],
        compiler_params=_params(("parallel",)),
        name="gla_mix",
    )(proj, proj, proj, proj, g_low, w_gu, b_gate.reshape(1, GLA_DK), norm_g.reshape(1, GLA_HV))


def _gla_layer(xf, xb, w_in, w_gate_up, b_gate, norm_g, w_out, ln_g, ln_b, batch, seq):
    n_main = 2 * GLA_DK + 2 * GLA_DV
    proj = _matmul(xb, w_in[:, :n_main].astype(BF16), BF16, "gla_proj")
    w_g = jnp.pad(w_in[:, n_main:], ((0, 0), (0, LANES - GLA_GATE_RANK)))
    g_low = _matmul(xf, w_g, F32, "gla_gate", tn=LANES, precision=HI)
    w_gu = jnp.pad(w_gate_up, ((0, LANES - GLA_GATE_RANK), (0, 0)))
    y = _gla_mix(proj, g_low, w_gu, b_gate, norm_g, batch, seq)
    return _proj_ln(y, w_out.astype(BF16), xf, ln_g, ln_b, "gla_out_ln")


def _ssd_gate_body(x_ref, w_ref, bias_ref, alog_ref, dt_ref, ac_ref, dtt_ref, act_ref, *, seq):
    ch = SSD_CHUNK
    raw = _dot(x_ref[...], w_ref[...], HI)
    dt = _softplus(raw + bias_ref[...])
    a = dt * (-jnp.exp(alog_ref[...]))
    lower_f = _tri(ch, lower=True).astype(F32)
    dt_ref[...] = dt
    dtt_ref[0] = dt.T
    for c in range(seq // ch):
        acum = _dot(lower_f, a[c * ch:(c + 1) * ch, :], HI)
        ac_ref[c * ch:(c + 1) * ch, :] = acum
        act_ref[0, :, c * ch:(c + 1) * ch] = acum.T


def _ssd_gate(x, w_dt, dt_bias, a_log, batch, seq):
    d = x.shape[1]
    col = jax.ShapeDtypeStruct((batch * seq, LANES), F32)
    row = jax.ShapeDtypeStruct((batch, LANES, seq), F32)
    return pl.pallas_call(
        functools.partial(_ssd_gate_body, seq=seq),
        out_shape=(col, col, row, row),
        grid=(batch,),
        in_specs=[pl.BlockSpec((seq, d), lambda b: (b, 0)),
                  pl.BlockSpec((d, LANES), lambda b: (0, 0)),
                  pl.BlockSpec((1, LANES), lambda b: (0, 0)),
                  pl.BlockSpec((1, LANES), lambda b: (0, 0))],
        out_specs=(pl.BlockSpec((seq, LANES), lambda b: (b, 0)),
                   pl.BlockSpec((seq, LANES), lambda b: (b, 0)),
                   pl.BlockSpec((1, LANES, seq), lambda b: (b, 0, 0)),
                   pl.BlockSpec((1, LANES, seq), lambda b: (b, 0, 0))),
        compiler_params=_params(("parallel",)),
        name="ssd_gate",
    )(x, w_dt, dt_bias, a_log)


def _expand_heads(cols, width):
    rows = cols.shape[0]
    lane = lax.broadcasted_iota(jnp.int32, (rows, SSD_HG * width), 1)
    out = jnp.broadcast_to(cols[:, SSD_HG - 1:SSD_HG], (rows, SSD_HG * width))
    for h in range(SSD_HG - 2, -1, -1):
        out = jnp.where(lane < (h + 1) * width, jnp.broadcast_to(cols[:, h:h + 1], out.shape), out)
    return out


def _ssd_body(z_ref, xr_ref, br_ref, cr_ref, cwx_ref, cwb_ref, cwc_ref, cbx_ref, cbb_ref, cbc_ref,
              dt4_ref, ac4_ref, dtt_ref, act_ref, dskip_ref, ng_ref, o_ref,
              padx_ref, padb_ref, xs_ref, bm_ref, cm_ref, st_ref, *, seq):
    ch = SSD_CHUNK
    hp = SSD_HG * SSD_P
    pad = SUBLANES

    def conv(raw_ref, pad_ref, w_ref, b_ref, dst_ref):
        pad_ref[0:pad, :] = jnp.zeros((pad, pad_ref.shape[1]), F32)
        pad_ref[pad:pad + seq, :] = raw_ref[...].astype(F32)
        tile = min(256, seq)
        for t0 in range(0, seq, tile):
            acc = jnp.broadcast_to(b_ref[...], (tile, pad_ref.shape[1]))
            for j in range(SSD_CONV):
                off = t0 + pad - (SSD_CONV - 1) + j
                acc = acc + w_ref[j:j + 1, :] * pad_ref[off:off + tile, :]
            dst_ref[t0:t0 + tile, :] = _silu(acc)

    conv(xr_ref, padx_ref, cwx_ref, cbx_ref, xs_ref)
    conv(br_ref, padb_ref, cwb_ref, cbb_ref, bm_ref)
    conv(cr_ref, padb_ref, cwc_ref, cbc_ref, cm_ref)

    lower = _tri(ch, lower=True)
    lane = lax.broadcasted_iota(jnp.int32, (1, hp), 1)
    st_ref[...] = jnp.zeros_like(st_ref)

    def chunk(c, carry):
        rows = pl.ds(pl.multiple_of(c * ch, ch), ch)
        xs = xs_ref[rows, :]
        bm = bm_ref[rows, :].astype(BF16)
        cm = cm_ref[rows, :].astype(BF16)
        dt4 = dt4_ref[0, 0, rows, :]
        ac4 = ac4_ref[0, 0, rows, :]
        to_end = jnp.exp(ac4[ch - 1:ch, :] - ac4)
        w_state = _expand_heads(dt4 * to_end, SSD_P)
        e_acum = _expand_heads(jnp.exp(ac4), SSD_P)
        cb = _dot_nt(cm, bm)
        xs_b = xs.astype(BF16)
        y = xs * dskip_ref[...]
        for h in range(SSD_HG):
            a_col = jnp.broadcast_to(ac4[:, h:h + 1], (ch, ch))
            a_row = act_ref[0, 0, h, pl.ds(c, 1), :]
            d_row = dtt_ref[0, 0, h, pl.ds(c, 1), :]
            decay = jnp.exp(jnp.where(lower, a_col - a_row, -jnp.inf))
            m_h = (cb * decay * d_row).astype(BF16)
            head = (lane >= h * SSD_P) & (lane < (h + 1) * SSD_P)
            y = y + _dot(m_h, jnp.where(head, xs_b, jnp.zeros_like(xs_b)))
        state = st_ref[...]
        y = y + _dot(cm, state.astype(BF16)) * e_acum
        st_ref[...] = state * e_acum[ch - 1:ch, :] + _dot_tn(bm, (xs * w_state).astype(BF16))
        y = y * _silu(z_ref[rows, :].astype(F32))
        ms = jnp.mean(y * y, axis=-1, keepdims=True)
        o_ref[rows, :] = (y * lax.rsqrt(ms + RMS_EPS) * ng_ref[...]).astype(o_ref.dtype)
        return carry

    lax.fori_loop(0, seq // ch, chunk, 0)


def _ssd_mix(proj, conv_w, conv_b, dt4, ac4, dtt, act, d_exp, norm_g, batch, seq):
    g = SSD_GROUPS
    hp = SSD_HG * SSD_P
    n = SSD_STATE
    nz = SSD_D_INNER // hp
    nb = 2 * SSD_D_INNER // n
    nc = nb + g
    cb0 = SSD_D_INNER // n
    nch = seq // SSD_CHUNK
    return pl.pallas_call(
        functools.partial(_ssd_body, seq=seq),
        out_shape=jax.ShapeDtypeStruct((batch * seq, SSD_D_INNER), BF16),
        grid=(batch, g),
        in_specs=[pl.BlockSpec((seq, hp), lambda b, i: (b, i)),
                  pl.BlockSpec((seq, hp), lambda b, i: (b, nz + i)),
                  pl.BlockSpec((seq, n), lambda b, i: (b, nb + i)),
                  pl.BlockSpec((seq, n), lambda b, i: (b, nc + i)),
                  pl.BlockSpec((SSD_CONV, hp), lambda b, i: (0, i)),
                  pl.BlockSpec((SSD_CONV, n), lambda b, i: (0, cb0 + i)),
                  pl.BlockSpec((SSD_CONV, n), lambda b, i: (0, cb0 + g + i)),
                  pl.BlockSpec((1, hp), lambda b, i: (0, i)),
                  pl.BlockSpec((1, n), lambda b, i: (0, cb0 + i)),
                  pl.BlockSpec((1, n), lambda b, i: (0, cb0 + g + i)),
                  pl.BlockSpec((1, 1, seq, SSD_HG), lambda b, i: (b, i, 0, 0)),
                  pl.BlockSpec((1, 1, seq, SSD_HG), lambda b, i: (b, i, 0, 0)),
                  pl.BlockSpec((1, 1, SSD_HG, nch, SSD_CHUNK), lambda b, i: (b, i, 0, 0, 0)),
                  pl.BlockSpec((1, 1, SSD_HG, nch, SSD_CHUNK), lambda b, i: (b, i, 0, 0, 0)),
                  pl.BlockSpec((1, hp), lambda b, i: (0, i)),
                  pl.BlockSpec((1, hp), lambda b, i: (0, i))],
        out_specs=pl.BlockSpec((seq, hp), lambda b, i: (b, i)),
        scratch_shapes=[pltpu.VMEM((seq + SUBLANES, hp), F32),
                        pltpu.VMEM((seq + SUBLANES, n), F32),
                        pltpu.VMEM((seq, hp), F32),
                        pltpu.VMEM((seq, n), F32),
                        pltpu.VMEM((seq, n), F32),
                        pltpu.VMEM((n, hp), F32)],
        compiler_params=_params(("parallel", "parallel")),
        name="ssd_mix",
    )(proj, proj, proj, proj, conv_w, conv_w, conv_w, conv_b, conv_b, conv_b,
      dt4, ac4, dtt, act, d_exp, norm_g)


def _ssd_layer(xf, xb, w_in, conv_w, conv_b, dt_bias, a_log, d_skip, norm_g, w_out, ln_g, ln_b,
               batch, seq):
    n_main = SSD_D_INNER + (SSD_D_INNER + 2 * SSD_GROUPS * SSD_STATE)
    proj = _matmul(xb, w_in[:, :n_main].astype(BF16), BF16, "ssd_proj")
    lane_pad = ((0, 0), (0, LANES - SSD_HEADS))
    w_dt = jnp.pad(w_in[:, n_main:], lane_pad)
    dt_c, ac_c, dt_r, ac_r = _ssd_gate(xf, w_dt, jnp.pad(dt_bias.reshape(1, -1), lane_pad),
                                       jnp.pad(a_log.reshape(1, -1), lane_pad), batch, seq)

    def cols(t):
        return t[:, :SSD_HEADS].reshape(batch, seq, SSD_GROUPS, SSD_HG).transpose(0, 2, 1, 3)

    def rows(t):
        return t[:, :SSD_HEADS].reshape(batch, SSD_GROUPS, SSD_HG, seq // SSD_CHUNK, SSD_CHUNK)

    d_exp = jnp.repeat(d_skip, SSD_P).reshape(1, SSD_D_INNER)
    y = _ssd_mix(proj, conv_w, conv_b.reshape(1, -1), cols(dt_c), cols(ac_c), rows(dt_r), rows(ac_r),
                 d_exp, norm_g.reshape(1, SSD_D_INNER), batch, seq)
    return _proj_ln(y, w_out.astype(BF16), xf, ln_g, ln_b, "ssd_out_ln")


def _router_body(x_ref, w_ref, bias_ref, e_ref, w_out_ref):
    tm = x_ref.shape[0]
    gsz = N_EXPERTS // N_EXPERT_GROUPS
    logits = _dot(x_ref[...], w_ref[...], HI).T[:N_EXPERTS, :]
    scores = _sigmoid(logits)
    sel = scores + bias_ref[...]
    sub = lax.broadcasted_iota(jnp.int32, (gsz, tm), 0).astype(F32)
    neg = jnp.float32(-jnp.inf)

    def top1(v, idx, sentinel):
        m = jnp.max(v, axis=0, keepdims=True)
        i = jnp.min(jnp.where(v == m, idx, sentinel), axis=0, keepdims=True)
        return m, i

    s_g = [sel[g * gsz:(g + 1) * gsz, :] for g in range(N_EXPERT_GROUPS)]
    sc_g = [scores[g * gsz:(g + 1) * gsz, :] for g in range(N_EXPERT_GROUPS)]
    grp_rows = []
    for g in range(N_EXPERT_GROUPS):
        m1, i1 = top1(s_g[g], sub, float(gsz))
        m2 = jnp.max(jnp.where(sub == i1, neg, s_g[g]), axis=0, keepdims=True)
        grp_rows.append(m1 + m2)
    grp = jnp.concatenate(grp_rows, axis=0)
    keep = jnp.zeros((N_EXPERT_GROUPS, tm), F32)
    for _ in range(TOPK_GROUPS):
        _, ig = top1(grp, sub, float(N_EXPERT_GROUPS))
        hit = sub == ig
        keep = jnp.where(hit, 1.0, keep)
        grp = jnp.where(hit, neg, grp)
    cand = [jnp.where(keep[g:g + 1, :] > 0.5, s_g[g], neg) for g in range(N_EXPERT_GROUPS)]
    ids = [sub + float(g * gsz) for g in range(N_EXPERT_GROUPS)]
    e_rows, w_rows = [], []
    for _ in range(TOP_K):
        m = cand[0]
        for g in range(1, N_EXPERT_GROUPS):
            m = jnp.maximum(m, cand[g])
        m = jnp.max(m, axis=0, keepdims=True)
        first = jnp.where(cand[0] == m, ids[0], float(N_EXPERTS))
        for g in range(1, N_EXPERT_GROUPS):
            first = jnp.minimum(first, jnp.where(cand[g] == m, ids[g], float(N_EXPERTS)))
        first = jnp.min(first, axis=0, keepdims=True)
        wsum = jnp.zeros((gsz, tm), F32)
        for g in range(N_EXPERT_GROUPS):
            hit = ids[g] == first
            wsum = wsum + jnp.where(hit, sc_g[g], 0.0)
            cand[g] = jnp.where(hit, neg, cand[g])
        e_rows.append(first)
        w_rows.append(jnp.sum(wsum, axis=0, keepdims=True))
    top_e = jnp.concatenate(e_rows, axis=0)
    top_w = jnp.concatenate(w_rows, axis=0)
    denom = jnp.sum(top_w, axis=0, keepdims=True)
    e_ref[...] = top_e.astype(jnp.int32)
    w_out_ref[...] = top_w / denom * ROUTED_SCALE


def _router(x, w_router, bias, tm=512):
    t, d = x.shape
    tm = min(tm, t)
    return pl.pallas_call(
        _router_body,
        out_shape=(jax.ShapeDtypeStruct((TOP_K, t), jnp.int32),
                   jax.ShapeDtypeStruct((TOP_K, t), F32)),
        grid=(t // tm,),
        in_specs=[pl.BlockSpec((tm, d), lambda i: (i, 0)),
                  pl.BlockSpec((d, LANES), lambda i: (0, 0)),
                  pl.BlockSpec((N_EXPERTS, 1), lambda i: (0, 0))],
        out_specs=(pl.BlockSpec((TOP_K, tm), lambda i: (0, i)),
                   pl.BlockSpec((TOP_K, tm), lambda i: (0, i))),
        compiler_params=_params(("parallel",)),
        name="moe_router",
    )(x, jnp.pad(w_router, ((0, 0), (0, LANES - N_EXPERTS))), bias.reshape(N_EXPERTS, 1))


def _moe_list_len(tb):
    rows = tb * TOP_K + N_EXPERTS * MOE_ROWS + 2 * MOE_ROWS
    return -(-rows // MOE_LIST_ALIGN) * MOE_LIST_ALIGN


def _moe_body(nch_ref, cstart_ref, gt_hbm, st_hbm, rw_hbm,
              x8_ref, wg_ref, wu_ref, wd_ref, sg_ref, su_ref, sd_ref,
              out_ref, gt_ref, st_ref, rw_ref, acc_ref, buf_ref, y_ref, sem, *, tb):
    blk = pl.program_id(0)
    e = pl.program_id(1)
    rows_per = MOE_ROWS
    tile = min(256, tb)
    n_list = gt_ref.shape[0]

    def tile_at(ref, off):
        return ref.at[pl.ds(pl.multiple_of(off, SUBLANES), SUBLANES), :]

    def gather(lbase):
        for r in range(rows_per):
            tile_at(buf_ref, r * SUBLANES)[...] = tile_at(x8_ref, gt_ref[lbase + r])[...]

    def scatter(lbase):
        for r0 in range(0, rows_per, SUBLANES):
            offs = [st_ref[lbase + r0 + u] for u in range(SUBLANES)]
            vals = [tile_at(acc_ref, offs[u])[...]
                    + rw_ref[lbase + r0 + u] * tile_at(y_ref, (r0 + u) * SUBLANES)[...]
                    for u in range(SUBLANES)]
            for u in range(SUBLANES):
                tile_at(acc_ref, offs[u])[...] = vals[u]

    @pl.when(e == 0)
    def _start_block():
        copies = [pltpu.make_async_copy(src.at[pl.ds(pl.multiple_of(blk * n_list, MOE_LIST_ALIGN), n_list)],
                                        dst, sem.at[n])
                  for n, (src, dst) in enumerate(((gt_hbm, gt_ref), (st_hbm, st_ref), (rw_hbm, rw_ref)))]
        for cp in copies:
            cp.start()
        for t0 in range(0, tb, tile):
            xb = _tok_rows(x8_ref, t0, tile).astype(BF16)
            h = _silu(_dot(xb, sg_ref[...])) * _dot(xb, su_ref[...])
            _tok_store(acc_ref, t0, _dot(h.astype(BF16), sd_ref[...]))
        tile_at(acc_ref, tb * SUBLANES)[...] = jnp.zeros((SUBLANES, LANES), F32)
        y_ref[...] = jnp.zeros_like(y_ref)
        for cp in copies:
            cp.wait()
        gather(rows_per)

    first = cstart_ref[blk * N_EXPERTS + e]
    count = nch_ref[blk * N_EXPERTS + e]

    def step(g, carry):
        xb = _tok_rows(buf_ref, 0, rows_per).astype(BF16)
        scatter(g * rows_per)
        gather((g + 2) * rows_per)
        h = _silu(_dot(xb, wg_ref[0])) * _dot(xb, wu_ref[0])
        _tok_store(y_ref, 0, _dot(h.astype(BF16), wd_ref[0]))
        return carry

    lax.fori_loop(first, first + count, step, 0)

    @pl.when(e == N_EXPERTS - 1)
    def _finish_block():
        scatter((first + count) * rows_per)
        out_ref[...] = acc_ref[0:tb * SUBLANES, :]


def _moe_experts(x8, nch, cstart, gt, st, rw, wg, wu, wd, sg, su, sd, tb):
    d = wg.shape[1]
    ff = wg.shape[2]
    nblk = x8.shape[0] // (tb * SUBLANES)
    n_list = _moe_list_len(tb)
    grid_spec = pltpu.PrefetchScalarGridSpec(
        num_scalar_prefetch=2,
        grid=(nblk, N_EXPERTS),
        in_specs=[pl.BlockSpec(memory_space=pl.ANY),
                  pl.BlockSpec(memory_space=pl.ANY),
                  pl.BlockSpec(memory_space=pl.ANY),
                  pl.BlockSpec((tb * SUBLANES, LANES), lambda i, e, *_: (i, 0)),
                  pl.BlockSpec((1, d, ff), lambda i, e, *_: (e, 0, 0)),
                  pl.BlockSpec((1, d, ff), lambda i, e, *_: (e, 0, 0)),
                  pl.BlockSpec((1, ff, d), lambda i, e, *_: (e, 0, 0)),
                  pl.BlockSpec((d, ff), lambda i, e, *_: (0, 0)),
                  pl.BlockSpec((d, ff), lambda i, e, *_: (0, 0)),
                  pl.BlockSpec((ff, d), lambda i, e, *_: (0, 0))],
        out_specs=pl.BlockSpec((tb * SUBLANES, LANES), lambda i, e, *_: (i, 0)),
        scratch_shapes=[pltpu.SMEM((n_list,), jnp.int32),
                        pltpu.SMEM((n_list,), jnp.int32),
                        pltpu.SMEM((n_list,), F32),
                        pltpu.VMEM(((tb + 1) * SUBLANES, LANES), F32),
                        pltpu.VMEM((MOE_ROWS * SUBLANES, LANES), F32),
                        pltpu.VMEM((MOE_ROWS * SUBLANES, LANES), F32),
                        pltpu.SemaphoreType.DMA((3,))],
    )
    return pl.pallas_call(
        functools.partial(_moe_body, tb=tb),
        out_shape=jax.ShapeDtypeStruct(x8.shape, F32),
        grid_spec=grid_spec,
        compiler_params=_params(("arbitrary", "arbitrary")),
        name="moe_experts",
    )(nch, cstart, gt, st, rw, x8, wg, wu, wd, sg, su, sd)


def _moe_finish_body(x_ref, acc_ref, g_ref, b_ref, xf_ref, xb_ref):
    z = DN_ALPHA * x_ref[...] + _tok_rows(acc_ref, 0, x_ref.shape[0])
    out = _layer_norm_rows(z, g_ref[...], b_ref[...])
    xf_ref[...] = out
    xb_ref[...] = out.astype(BF16)


def _moe_finish(x, acc8, g, b, tm=512):
    m, d = x.shape
    tm = min(tm, m)
    return pl.pallas_call(
        _moe_finish_body,
        out_shape=(jax.ShapeDtypeStruct((m, d), F32), jax.ShapeDtypeStruct((m, d), BF16)),
        grid=(m // tm,),
        in_specs=[pl.BlockSpec((tm, d), lambda i: (i, 0)),
                  pl.BlockSpec((tm * SUBLANES, LANES), lambda i: (i, 0)),
                  pl.BlockSpec((1, d), lambda i: (0, 0)),
                  pl.BlockSpec((1, d), lambda i: (0, 0))],
        out_specs=(pl.BlockSpec((tm, d), lambda i: (i, 0)),
                   pl.BlockSpec((tm, d), lambda i: (i, 0))),
        compiler_params=_params(("parallel",)),
        name="moe_finish",
    )(x, acc8, g.reshape(1, d), b.reshape(1, d))


def _moe_layer(xf, x8, w_router, router_bias, w_gate, w_up, w_down, ws_gate, ws_up, ws_down,
               ln_g, ln_b, tb):
    t, d = xf.shape
    top_e, top_w = _router(xf, w_router, router_bias)
    nblk = t // tb
    per = tb * TOP_K
    e_flat = top_e.T.reshape(nblk, per)
    w_flat = top_w.T.reshape(nblk, per)
    order = jnp.argsort(e_flat, axis=-1, stable=True)
    tok = (order // TOP_K).astype(jnp.int32)
    rw = jnp.take_along_axis(w_flat, order, axis=-1)
    cnt = jnp.sum(e_flat[:, :, None] == jnp.arange(N_EXPERTS, dtype=jnp.int32), axis=1,
                  dtype=jnp.int32)
    off = jnp.cumsum(cnt, axis=-1, dtype=jnp.int32) - cnt
    nch = (cnt + MOE_ROWS - 1) // MOE_ROWS
    cend = jnp.cumsum(nch, axis=-1, dtype=jnp.int32)
    cstart = cend - nch
    n_list = _moe_list_len(tb)
    pos = jnp.arange(n_list, dtype=jnp.int32)
    chunk = pos // MOE_ROWS - 1
    exp_of = jnp.sum(cend[:, None, :] <= chunk[None, :, None], axis=-1, dtype=jnp.int32)
    exp_c = jnp.minimum(exp_of, N_EXPERTS - 1)
    row = pos[None, :] - (1 + jnp.take_along_axis(cstart, exp_c, axis=-1)) * MOE_ROWS
    valid = (chunk[None, :] >= 0) & (exp_of < N_EXPERTS) & (row < jnp.take_along_axis(cnt, exp_c, axis=-1))
    src = jnp.clip(jnp.take_along_axis(off, exp_c, axis=-1) + row, 0, per - 1)
    tile_off = jnp.take_along_axis(tok, src, axis=-1) * SUBLANES
    gt = jnp.where(valid, tile_off, 0)
    st = jnp.where(valid, tile_off, tb * SUBLANES)
    rw_pad = jnp.where(valid, jnp.take_along_axis(rw, src, axis=-1), 0.0)
    wg, wu, wd = w_gate.astype(BF16), w_up.astype(BF16), w_down.astype(BF16)
    sg, su, sd = ws_gate.astype(BF16), ws_up.astype(BF16), ws_down.astype(BF16)
    acc8 = _moe_experts(x8, nch.reshape(-1), cstart.reshape(-1), gt.reshape(-1), st.reshape(-1),
                        rw_pad.reshape(-1), wg, wu, wd, sg, su, sd, tb)
    return _moe_finish(xf, acc8, ln_g, ln_b)


def kernel(x, fox_w_in, fox_b_f, fox_w_out, gla_w_in, gla_w_gate_up, gla_b_gate, gla_norm, gla_w_out, ssd_w_in, ssd_conv_w, ssd_conv_b, ssd_dt_bias, ssd_a_log, ssd_d, ssd_norm, ssd_w_out, ln1_g, ln1_b, moe_router, moe_router_bias, moe_w_gate, moe_w_up, moe_w_down, moe_ws_gate, moe_ws_up, moe_ws_down, ln2_g, ln2_b):
    batch, seq, d = x.shape
    xf = x.reshape(batch * seq, d)
    xb = xf.astype(BF16)
    tb = seq
    for i in range(DEPTH):
        kind, j = i % N_MIXERS, i // N_MIXERS
        if kind == 0:
            xf, xb, x8 = _fox_layer(xf, xb, fox_w_in[j], fox_b_f[j], fox_w_out[j], ln1_g[i], ln1_b[i],
                                    batch, seq)
        elif kind == 1:
            xf, xb, x8 = _gla_layer(xf, xb, gla_w_in[j], gla_w_gate_up[j], gla_b_gate[j], gla_norm[j],
                                    gla_w_out[j], ln1_g[i], ln1_b[i], batch, seq)
        else:
            xf, xb, x8 = _ssd_layer(xf, xb, ssd_w_in[j], ssd_conv_w[j], ssd_conv_b[j], ssd_dt_bias[j],
                                    ssd_a_log[j], ssd_d[j], ssd_norm[j], ssd_w_out[j], ln1_g[i],
                                    ln1_b[i], batch, seq)
        xf, xb = _moe_layer(xf, x8, moe_router[i], moe_router_bias[i], moe_w_gate[i], moe_w_up[i],
                            moe_w_down[i], moe_ws_gate[i], moe_ws_up[i], moe_ws_down[i],
                            ln2_g[i], ln2_b[i], tb)
    return xf.reshape(batch, seq, d)
```

```python
import functools

import jax
import jax.numpy as jnp
from jax import lax
from jax.experimental import pallas as pl
from jax.experimental.pallas import tpu as pltpu

F32 = jnp.float32
BF16 = jnp.bfloat16
HI = lax.Precision.HIGHEST

D_MODEL = 1024
DEPTH = 4
N_MIXERS = 3
DN_ALPHA = (2 * DEPTH) ** 0.25
LN_EPS = 1e-5
RMS_EPS = 1e-6

FOX_HEADS = 16
FOX_HEAD_DIM = 64
GLA_HEADS = 4
GLA_DK = 512
GLA_DV = 1024
GLA_HK = 128
GLA_HV = 256
GLA_GATE_RANK = 16
GLA_GATE_TAU = 16.0
GLA_CHUNK = 64
SSD_D_INNER = 2048
SSD_HEADS = 32
SSD_GROUPS = 8
SSD_HG = 4
SSD_P = 64
SSD_STATE = 128
SSD_CONV = 4
SSD_CHUNK = 128
N_EXPERTS = 64
TOP_K = 8
N_EXPERT_GROUPS = 8
TOPK_GROUPS = 4
EXPERT_FF = 256
ROUTED_SCALE = 2.5

LANES = 128
SUBLANES = 8
VMEM_LIMIT = 56 * 2 ** 20
MOE_ROWS = 128
MOE_LIST_ALIGN = 1024
MOE_SCATTER_GROUP = 8


def _params(sem):
    return pltpu.CompilerParams(dimension_semantics=sem, vmem_limit_bytes=VMEM_LIMIT)


def _sigmoid(x):
    return 1.0 / (1.0 + jnp.exp(-x))


def _silu(x):
    return x * _sigmoid(x)


def _log_sigmoid(x):
    return jnp.minimum(x, 0.0) - jnp.log(1.0 + jnp.exp(-jnp.abs(x)))


def _softplus(x):
    return jnp.maximum(x, 0.0) + jnp.log(1.0 + jnp.exp(-jnp.abs(x)))


def _dot(a, b, precision=None):
    return jnp.dot(a, b, preferred_element_type=F32, precision=precision)


def _dot_nt(a, b, precision=None):
    return lax.dot_general(a, b, (((1,), (1,)), ((), ())), preferred_element_type=F32,
                           precision=precision)


def _dot_tn(a, b, precision=None):
    return lax.dot_general(a, b, (((0,), (0,)), ((), ())), preferred_element_type=F32,
                           precision=precision)


def _tri(n, lower):
    r = lax.broadcasted_iota(jnp.int32, (n, n), 0)
    c = lax.broadcasted_iota(jnp.int32, (n, n), 1)
    return (r >= c) if lower else (r <= c)


def _mm_body(x_ref, w_ref, o_ref, *, precision):
    o_ref[...] = _dot(x_ref[...], w_ref[...], precision).astype(o_ref.dtype)


def _matmul(x, w, out_dtype, name, tm=1024, tn=1024, precision=None):
    m, k = x.shape
    n = w.shape[1]
    tm, tn = min(tm, m), min(tn, n)
    return pl.pallas_call(
        functools.partial(_mm_body, precision=precision),
        out_shape=jax.ShapeDtypeStruct((m, n), out_dtype),
        grid=(n // tn, m // tm),
        in_specs=[pl.BlockSpec((tm, k), lambda j, i: (i, 0)),
                  pl.BlockSpec((k, tn), lambda j, i: (0, j))],
        out_specs=pl.BlockSpec((tm, tn), lambda j, i: (i, j)),
        compiler_params=_params(("parallel", "parallel")),
        name=name,
    )(x, w)


def _layer_norm_rows(z, g, b):
    mu = jnp.mean(z, axis=-1, keepdims=True)
    zc = z - mu
    var = jnp.mean(zc * zc, axis=-1, keepdims=True)
    return zc * lax.rsqrt(var + LN_EPS) * g + b


def _tok_rows(ref, t0, n):
    return jnp.concatenate(
        [ref[pl.ds(t0 * SUBLANES + k, n, stride=SUBLANES), :] for k in range(SUBLANES)], axis=1)


def _tok_store(ref, t0, val):
    for k in range(SUBLANES):
        ref[pl.ds(t0 * SUBLANES + k, val.shape[0], stride=SUBLANES), :] = val[:, k * LANES:(k + 1) * LANES]


def _proj_ln_body(o_ref, w_ref, x_ref, g_ref, b_ref, xf_ref, xb_ref, x8_ref):
    y = _dot(o_ref[...], w_ref[...])
    out = _layer_norm_rows(DN_ALPHA * x_ref[...] + y, g_ref[...], b_ref[...])
    xf_ref[...] = out
    xb_ref[...] = out.astype(BF16)
    _tok_store(x8_ref, 0, out)


def _proj_ln(o, w, x, g, b, name, tm=512):
    m, k = o.shape
    d = w.shape[1]
    assert d == SUBLANES * LANES
    tm = min(tm, m)
    return pl.pallas_call(
        _proj_ln_body,
        out_shape=(jax.ShapeDtypeStruct((m, d), F32), jax.ShapeDtypeStruct((m, d), BF16),
                   jax.ShapeDtypeStruct((m * SUBLANES, LANES), F32)),
        grid=(m // tm,),
        in_specs=[pl.BlockSpec((tm, k), lambda i: (i, 0)),
                  pl.BlockSpec((k, d), lambda i: (0, 0)),
                  pl.BlockSpec((tm, d), lambda i: (i, 0)),
                  pl.BlockSpec((1, d), lambda i: (0, 0)),
                  pl.BlockSpec((1, d), lambda i: (0, 0))],
        out_specs=(pl.BlockSpec((tm, d), lambda i: (i, 0)),
                   pl.BlockSpec((tm, d), lambda i: (i, 0)),
                   pl.BlockSpec((tm * SUBLANES, LANES), lambda i: (i, 0))),
        compiler_params=_params(("parallel",)),
        name=name,
    )(o, w, x, g.reshape(1, d), b.reshape(1, d))


def _fox_gate_body(x_ref, w_ref, b_ref, c_ref, *, seq):
    f = _dot(x_ref[...], w_ref[...], HI)
    z = f.T[:FOX_HEADS, :] + b_ref[...]
    lf = _log_sigmoid(z)
    upper = _tri(LANES, lower=False).astype(F32)
    carry = jnp.zeros((FOX_HEADS, 1), F32)
    for blk in range(seq // LANES):
        seg = _dot(lf[:, blk * LANES:(blk + 1) * LANES], upper, HI) + carry
        c_ref[0, :, blk * LANES:(blk + 1) * LANES] = seg
        carry = seg[:, LANES - 1:LANES]


def _fox_gate(x, w_f, b_f, batch, seq):
    d = x.shape[1]
    return pl.pallas_call(
        functools.partial(_fox_gate_body, seq=seq),
        out_shape=jax.ShapeDtypeStruct((batch, FOX_HEADS, seq), F32),
        grid=(batch,),
        in_specs=[pl.BlockSpec((seq, d), lambda b: (b, 0)),
                  pl.BlockSpec((d, LANES), lambda b: (0, 0)),
                  pl.BlockSpec((FOX_HEADS, 1), lambda b: (0, 0))],
        out_specs=pl.BlockSpec((1, FOX_HEADS, seq), lambda b: (b, 0, 0)),
        compiler_params=_params(("parallel",)),
        name="fox_gate",
    )(x, w_f, b_f)


def _fox_attn_body(q_ref, k_ref, v_ref, c_ref, o_ref, kh_ref, qh_ref, vt_ref, acc_ref, ml_ref, ot_ref,
                   *, seq, t):
    nq = seq // t
    hd = FOX_HEAD_DIM
    lane = lax.broadcasted_iota(jnp.int32, (1, LANES), 1)
    valid = _tri(t, lower=False)
    scale = hd ** -0.5
    for h in range(2):
        own = (lane >= h * hd) & (lane < (h + 1) * hd)
        aug = (1 - h) * hd
        c_col = c_ref[0, 0, :, h:h + 1]
        hi = c_col.astype(BF16).astype(F32)
        r1 = c_col - hi
        mid = r1.astype(BF16).astype(F32)
        lo = (r1 - mid).astype(BF16).astype(F32)
        sel = [(lane == aug + n).astype(F32) for n in range(3)]
        c_lanes = hi * sel[0] + mid * sel[1] + lo * sel[2]
        own_f = own.astype(F32)
        kh_ref[h] = (k_ref[...].astype(F32) * (scale * own_f) + c_lanes).astype(BF16)
        qh_ref[h] = (q_ref[...].astype(F32) * own_f - (sel[0] + sel[1] + sel[2])).astype(BF16)
    vt_ref[...] = v_ref[...].astype(F32).T.astype(BF16)

    def scores(tile):
        j, i, h = tile
        s = _dot_nt(kh_ref[h, j * t:(j + 1) * t, :], qh_ref[h, i * t:(i + 1) * t, :])
        return jnp.where(valid, s, -jnp.inf) if i == j else s

    def update(tile, s):
        j, i, h = tile
        rows = slice(h * hd, (h + 1) * hd)
        m_new = jnp.max(s, axis=0, keepdims=True)
        if j > 0:
            m_old = ml_ref[i, h:h + 1, :]
            m_new = jnp.maximum(m_old, m_new)
        p = jnp.exp(s - m_new)
        l_new = jnp.sum(p, axis=0, keepdims=True)
        acc = _dot(vt_ref[rows, j * t:(j + 1) * t], p.astype(BF16))
        if j > 0:
            a = jnp.exp(m_old - m_new)
            l_new = a * ml_ref[i, 2 + h:3 + h, :] + l_new
            acc = a * acc_ref[i, rows, :] + acc
        if i == j:
            ot_ref[rows, i * t:(i + 1) * t] = acc / l_new
        else:
            ml_ref[i, h:h + 1, :] = m_new
            ml_ref[i, 2 + h:3 + h, :] = l_new
            acc_ref[i, rows, :] = acc

    tiles = [(j, i, h) for j in range(nq) for i in range(j, nq) for h in range(2)]
    ahead = 2
    pending = []
    for idx in range(len(tiles) + ahead):
        if idx < len(tiles):
            pending.append(scores(tiles[idx]))
        if idx >= ahead:
            update(tiles[idx - ahead], pending.pop(0))
    o_ref[...] = ot_ref[...].T.astype(o_ref.dtype)


def _fox_attn(qkv, c, batch, seq, t=256):
    t = min(t, seq)
    pairs = FOX_HEADS // 2
    c_cols = c.reshape(batch, pairs, 2, seq).transpose(0, 1, 3, 2)
    return pl.pallas_call(
        functools.partial(_fox_attn_body, seq=seq, t=t),
        out_shape=jax.ShapeDtypeStruct((batch * seq, D_MODEL), BF16),
        grid=(batch, pairs),
        in_specs=[pl.BlockSpec((seq, LANES), lambda b, p: (b, p)),
                  pl.BlockSpec((seq, LANES), lambda b, p: (b, pairs + p)),
                  pl.BlockSpec((seq, LANES), lambda b, p: (b, 2 * pairs + p)),
                  pl.BlockSpec((1, 1, seq, 2), lambda b, p: (b, p, 0, 0))],
        out_specs=pl.BlockSpec((seq, LANES), lambda b, p: (b, p)),
        scratch_shapes=[pltpu.VMEM((2, seq, LANES), BF16),
                        pltpu.VMEM((2, seq, LANES), BF16),
                        pltpu.VMEM((LANES, seq), BF16),
                        pltpu.VMEM((seq // t, LANES, t), F32),
                        pltpu.VMEM((seq // t, SUBLANES, t), F32),
                        pltpu.VMEM((LANES, seq), F32)],
        compiler_params=_params(("parallel", "parallel")),
        name="fox_attn",
    )(qkv, qkv, qkv, c_cols)


def _fox_layer(xf, xb, w_in, b_f, w_out, ln_g, ln_b, batch, seq):
    d = D_MODEL
    qkv = _matmul(xb, w_in[:, :3 * d].astype(BF16), BF16, "fox_qkv")
    w_f = jnp.pad(w_in[:, 3 * d:], ((0, 0), (0, LANES - FOX_HEADS)))
    c = _fox_gate(xf, w_f, b_f.reshape(FOX_HEADS, 1), batch, seq)
    o = _fox_attn(qkv, c, batch, seq)
    return _proj_ln(o, w_out.astype(BF16), xf, ln_g, ln_b, "fox_out_ln")


def _gla_body(q_ref, k_ref, v_ref, r_ref, gl_ref, wgu_ref, bg_ref, ng_ref, o_ref, bc_ref, st_ref, *, seq):
    ch = GLA_CHUNK
    tile = min(256, seq)
    r_i = lax.broadcasted_iota(jnp.int32, (tile, tile), 0)
    c_i = lax.broadcasted_iota(jnp.int32, (tile, tile), 1)
    same_chunk_lower = ((r_i >= c_i) & (r_i // ch == c_i // ch)).astype(F32)
    for t0 in range(0, seq, tile):
        gate = _dot(gl_ref[t0:t0 + tile, :], wgu_ref[...], HI) + bg_ref[...]
        log_a = _log_sigmoid(gate) * (1.0 / GLA_GATE_TAU)
        bc_ref[t0:t0 + tile, :] = _dot(same_chunk_lower, log_a, HI)
    lower = _tri(ch, lower=True)
    st_ref[...] = jnp.zeros_like(st_ref)
    heads = range(GLA_HEADS)

    def chunk(i, carry):
        rows = pl.ds(pl.multiple_of(i * ch, ch), ch)
        bcum = bc_ref[rows, :]
        b_last = bcum[ch - 1:ch, :]
        q = q_ref[rows, :].astype(F32) * (GLA_HK ** -0.5)
        k = k_ref[rows, :].astype(F32)
        q_dec = (q * jnp.exp(bcum)).astype(BF16)
        k_inv = (k * jnp.exp(-bcum)).astype(BF16)
        k_end = (k * jnp.exp(b_last - bcum)).astype(BF16)
        decay = jnp.exp(b_last)
        kl = [slice(h * GLA_HK, (h + 1) * GLA_HK) for h in heads]
        vl = [slice(h * GLA_HV, (h + 1) * GLA_HV) for h in heads]
        v = [v_ref[rows, vl[h]] for h in heads]
        att = [jnp.where(lower, _dot_nt(q_dec[:, kl[h]], k_inv[:, kl[h]]), 0.0).astype(BF16)
               for h in heads]
        state_t = [st_ref[h] for h in heads]
        o_inter = [_dot_nt(q_dec[:, kl[h]], state_t[h].astype(BF16)) for h in heads]
        o_intra = [_dot(att[h], v[h]) for h in heads]
        kv_t = [_dot_tn(v[h], k_end[:, kl[h]]) for h in heads]
        for h in heads:
            st_ref[h] = state_t[h] * decay[:, kl[h]] + kv_t[h]
            o = o_intra[h] + o_inter[h]
            ms = jnp.mean(o * o, axis=-1, keepdims=True)
            on = o * lax.rsqrt(ms + RMS_EPS) * ng_ref[...]
            o_ref[rows, vl[h]] = (on * _silu(r_ref[rows, vl[h]].astype(F32))).astype(o_ref.dtype)
        return carry

    lax.fori_loop(0, seq // ch, chunk, 0)


def _gla_mix(proj, g_low, w_gu, b_gate, norm_g, batch, seq):
    return pl.pallas_call(
        functools.partial(_gla_body, seq=seq),
        out_shape=jax.ShapeDtypeStruct((batch * seq, GLA_DV), BF16),
        grid=(batch,),
        in_specs=[pl.BlockSpec((seq, GLA_DK), lambda b: (b, 0)),
                  pl.BlockSpec((seq, GLA_DK), lambda b: (b, 1)),
                  pl.BlockSpec((seq, GLA_DV), lambda b: (b, 1)),
                  pl.BlockSpec((seq, GLA_DV), lambda b: (b, 2)),
                  pl.BlockSpec((seq, LANES), lambda b: (b, 0)),
                  pl.BlockSpec((LANES, GLA_DK), lambda b: (0, 0)),
                  pl.BlockSpec((1, GLA_DK), lambda b: (0, 0)),
                  pl.BlockSpec((1, GLA_HV), lambda b: (0, 0))],
        out_specs=pl.BlockSpec((seq, GLA_DV), lambda b: (b, 0)),
        scratch_shapes=[pltpu.VMEM((seq, GLA_DK), F32),
                        pltpu.VMEM((GLA_HEADS, GLA_HV, GLA_HK), F32)],
        compiler_params=_params(("parallel",)),
        name="gla_mix",
    )(proj, proj, proj, proj, g_low, w_gu, b_gate.reshape(1, GLA_DK), norm_g.reshape(1, GLA_HV))


def _gla_layer(xf, xb, w_in, w_gate_up, b_gate, norm_g, w_out, ln_g, ln_b, batch, seq):
    n_main = 2 * GLA_DK + 2 * GLA_DV
    proj = _matmul(xb, w_in[:, :n_main].astype(BF16), BF16, "gla_proj")
    w_g = jnp.pad(w_in[:, n_main:], ((0, 0), (0, LANES - GLA_GATE_RANK)))
    g_low = _matmul(xf, w_g, F32, "gla_gate", tn=LANES, precision=HI)
    w_gu = jnp.pad(w_gate_up, ((0, LANES - GLA_GATE_RANK), (0, 0)))
    y = _gla_mix(proj, g_low, w_gu, b_gate, norm_g, batch, seq)
    return _proj_ln(y, w_out.astype(BF16), xf, ln_g, ln_b, "gla_out_ln")


def _ssd_gate_body(x_ref, w_ref, bias_ref, alog_ref, dt_ref, ac_ref, dtt_ref, act_ref, *, seq):
    ch = SSD_CHUNK
    raw = _dot(x_ref[...], w_ref[...], HI)
    dt = _softplus(raw + bias_ref[...])
    a = dt * (-jnp.exp(alog_ref[...]))
    lower_f = _tri(ch, lower=True).astype(F32)
    dt_ref[...] = dt
    dtt_ref[0] = dt.T
    for c in range(seq // ch):
        acum = _dot(lower_f, a[c * ch:(c + 1) * ch, :], HI)
        ac_ref[c * ch:(c + 1) * ch, :] = acum
        act_ref[0, :, c * ch:(c + 1) * ch] = acum.T


def _ssd_gate(x, w_dt, dt_bias, a_log, batch, seq):
    d = x.shape[1]
    col = jax.ShapeDtypeStruct((batch * seq, LANES), F32)
    row = jax.ShapeDtypeStruct((batch, LANES, seq), F32)
    return pl.pallas_call(
        functools.partial(_ssd_gate_body, seq=seq),
        out_shape=(col, col, row, row),
        grid=(batch,),
        in_specs=[pl.BlockSpec((seq, d), lambda b: (b, 0)),
                  pl.BlockSpec((d, LANES), lambda b: (0, 0)),
                  pl.BlockSpec((1, LANES), lambda b: (0, 0)),
                  pl.BlockSpec((1, LANES), lambda b: (0, 0))],
        out_specs=(pl.BlockSpec((seq, LANES), lambda b: (b, 0)),
                   pl.BlockSpec((seq, LANES), lambda b: (b, 0)),
                   pl.BlockSpec((1, LANES, seq), lambda b: (b, 0, 0)),
                   pl.BlockSpec((1, LANES, seq), lambda b: (b, 0, 0))),
        compiler_params=_params(("parallel",)),
        name="ssd_gate",
    )(x, w_dt, dt_bias, a_log)


def _expand_heads(cols, width):
    rows = cols.shape[0]
    lane = lax.broadcasted_iota(jnp.int32, (rows, SSD_HG * width), 1)
    out = jnp.broadcast_to(cols[:, SSD_HG - 1:SSD_HG], (rows, SSD_HG * width))
    for h in range(SSD_HG - 2, -1, -1):
        out = jnp.where(lane < (h + 1) * width, jnp.broadcast_to(cols[:, h:h + 1], out.shape), out)
    return out


def _ssd_body(z_ref, xr_ref, br_ref, cr_ref, cwx_ref, cwb_ref, cwc_ref, cbx_ref, cbb_ref, cbc_ref,
              dt4_ref, ac4_ref, dtt_ref, act_ref, dskip_ref, ng_ref, o_ref,
              padx_ref, padb_ref, xs_ref, bm_ref, cm_ref, st_ref, *, seq):
    ch = SSD_CHUNK
    hp = SSD_HG * SSD_P
    pad = SUBLANES

    def conv(raw_ref, pad_ref, w_ref, b_ref, dst_ref):
        pad_ref[0:pad, :] = jnp.zeros((pad, pad_ref.shape[1]), F32)
        pad_ref[pad:pad + seq, :] = raw_ref[...].astype(F32)
        tile = min(256, seq)
        for t0 in range(0, seq, tile):
            acc = jnp.broadcast_to(b_ref[...], (tile, pad_ref.shape[1]))
            for j in range(SSD_CONV):
                off = t0 + pad - (SSD_CONV - 1) + j
                acc = acc + w_ref[j:j + 1, :] * pad_ref[off:off + tile, :]
            dst_ref[t0:t0 + tile, :] = _silu(acc)

    conv(xr_ref, padx_ref, cwx_ref, cbx_ref, xs_ref)
    conv(br_ref, padb_ref, cwb_ref, cbb_ref, bm_ref)
    conv(cr_ref, padb_ref, cwc_ref, cbc_ref, cm_ref)

    lower = _tri(ch, lower=True)
    lane = lax.broadcasted_iota(jnp.int32, (1, hp), 1)
    st_ref[...] = jnp.zeros_like(st_ref)

    def chunk(c, carry):
        rows = pl.ds(pl.multiple_of(c * ch, ch), ch)
        xs = xs_ref[rows, :]
        bm = bm_ref[rows, :].astype(BF16)
        cm = cm_ref[rows, :].astype(BF16)
        dt4 = dt4_ref[0, 0, rows, :]
        ac4 = ac4_ref[0, 0, rows, :]
        to_end = jnp.exp(ac4[ch - 1:ch, :] - ac4)
        w_state = _expand_heads(dt4 * to_end, SSD_P)
        e_acum = _expand_heads(jnp.exp(ac4), SSD_P)
        cb = _dot_nt(cm, bm)
        xs_b = xs.astype(BF16)
        y = xs * dskip_ref[...]
        for h in range(SSD_HG):
            a_col = jnp.broadcast_to(ac4[:, h:h + 1], (ch, ch))
            a_row = act_ref[0, 0, h, pl.ds(c, 1), :]
            d_row = dtt_ref[0, 0, h, pl.ds(c, 1), :]
            decay = jnp.exp(jnp.where(lower, a_col - a_row, -jnp.inf))
            m_h = (cb * decay * d_row).astype(BF16)
            head = (lane >= h * SSD_P) & (lane < (h + 1) * SSD_P)
            y = y + _dot(m_h, jnp.where(head, xs_b, jnp.zeros_like(xs_b)))
        state = st_ref[...]
        y = y + _dot(cm, state.astype(BF16)) * e_acum
        st_ref[...] = state * e_acum[ch - 1:ch, :] + _dot_tn(bm, (xs * w_state).astype(BF16))
        y = y * _silu(z_ref[rows, :].astype(F32))
        ms = jnp.mean(y * y, axis=-1, keepdims=True)
        o_ref[rows, :] = (y * lax.rsqrt(ms + RMS_EPS) * ng_ref[...]).astype(o_ref.dtype)
        return carry

    lax.fori_loop(0, seq // ch, chunk, 0)


def _ssd_mix(proj, conv_w, conv_b, dt4, ac4, dtt, act, d_exp, norm_g, batch, seq):
    g = SSD_GROUPS
    hp = SSD_HG * SSD_P
    n = SSD_STATE
    nz = SSD_D_INNER // hp
    nb = 2 * SSD_D_INNER // n
    nc = nb + g
    cb0 = SSD_D_INNER // n
    nch = seq // SSD_CHUNK
    return pl.pallas_call(
        functools.partial(_ssd_body, seq=seq),
        out_shape=jax.ShapeDtypeStruct((batch * seq, SSD_D_INNER), BF16),
        grid=(batch, g),
        in_specs=[pl.BlockSpec((seq, hp), lambda b, i: (b, i)),
                  pl.BlockSpec((seq, hp), lambda b, i: (b, nz + i)),
                  pl.BlockSpec((seq, n), lambda b, i: (b, nb + i)),
                  pl.BlockSpec((seq, n), lambda b, i: (b, nc + i)),
                  pl.BlockSpec((SSD_CONV, hp), lambda b, i: (0, i)),
                  pl.BlockSpec((SSD_CONV, n), lambda b, i: (0, cb0 + i)),
                  pl.BlockSpec((SSD_CONV, n), lambda b, i: (0, cb0 + g + i)),
                  pl.BlockSpec((1, hp), lambda b, i: (0, i)),
                  pl.BlockSpec((1, n), lambda b, i: (0, cb0 + i)),
                  pl.BlockSpec((1, n), lambda b, i: (0, cb0 + g + i)),
                  pl.BlockSpec((1, 1, seq, SSD_HG), lambda b, i: (b, i, 0, 0)),
                  pl.BlockSpec((1, 1, seq, SSD_HG), lambda b, i: (b, i, 0, 0)),
                  pl.BlockSpec((1, 1, SSD_HG, nch, SSD_CHUNK), lambda b, i: (b, i, 0, 0, 0)),
                  pl.BlockSpec((1, 1, SSD_HG, nch, SSD_CHUNK), lambda b, i: (b, i, 0, 0, 0)),
                  pl.BlockSpec((1, hp), lambda b, i: (0, i)),
                  pl.BlockSpec((1, hp), lambda b, i: (0, i))],
        out_specs=pl.BlockSpec((seq, hp), lambda b, i: (b, i)),
        scratch_shapes=[pltpu.VMEM((seq + SUBLANES, hp), F32),
                        pltpu.VMEM((seq + SUBLANES, n), F32),
                        pltpu.VMEM((seq, hp), F32),
                        pltpu.VMEM((seq, n), F32),
                        pltpu.VMEM((seq, n), F32),
                        pltpu.VMEM((n, hp), F32)],
        compiler_params=_params(("parallel", "parallel")),
        name="ssd_mix",
    )(proj, proj, proj, proj, conv_w, conv_w, conv_w, conv_b, conv_b, conv_b,
      dt4, ac4, dtt, act, d_exp, norm_g)


def _ssd_layer(xf, xb, w_in, conv_w, conv_b, dt_bias, a_log, d_skip, norm_g, w_out, ln_g, ln_b,
               batch, seq):
    n_main = SSD_D_INNER + (SSD_D_INNER + 2 * SSD_GROUPS * SSD_STATE)
    proj = _matmul(xb, w_in[:, :n_main].astype(BF16), BF16, "ssd_proj")
    lane_pad = ((0, 0), (0, LANES - SSD_HEADS))
    w_dt = jnp.pad(w_in[:, n_main:], lane_pad)
    dt_c, ac_c, dt_r, ac_r = _ssd_gate(xf, w_dt, jnp.pad(dt_bias.reshape(1, -1), lane_pad),
                                       jnp.pad(a_log.reshape(1, -1), lane_pad), batch, seq)

    def cols(t):
        return t[:, :SSD_HEADS].reshape(batch, seq, SSD_GROUPS, SSD_HG).transpose(0, 2, 1, 3)

    def rows(t):
        return t[:, :SSD_HEADS].reshape(batch, SSD_GROUPS, SSD_HG, seq // SSD_CHUNK, SSD_CHUNK)

    d_exp = jnp.repeat(d_skip, SSD_P).reshape(1, SSD_D_INNER)
    y = _ssd_mix(proj, conv_w, conv_b.reshape(1, -1), cols(dt_c), cols(ac_c), rows(dt_r), rows(ac_r),
                 d_exp, norm_g.reshape(1, SSD_D_INNER), batch, seq)
    return _proj_ln(y, w_out.astype(BF16), xf, ln_g, ln_b, "ssd_out_ln")


def _router_body(x_ref, w_ref, bias_ref, e_ref, w_out_ref):
    tm = x_ref.shape[0]
    gsz = N_EXPERTS // N_EXPERT_GROUPS
    logits = _dot(x_ref[...], w_ref[...], HI).T[:N_EXPERTS, :]
    scores = _sigmoid(logits)
    sel = scores + bias_ref[...]
    sub = lax.broadcasted_iota(jnp.int32, (gsz, tm), 0).astype(F32)
    neg = jnp.float32(-jnp.inf)

    def top1(v, idx, sentinel):
        m = jnp.max(v, axis=0, keepdims=True)
        i = jnp.min(jnp.where(v == m, idx, sentinel), axis=0, keepdims=True)
        return m, i

    s_g = [sel[g * gsz:(g + 1) * gsz, :] for g in range(N_EXPERT_GROUPS)]
    sc_g = [scores[g * gsz:(g + 1) * gsz, :] for g in range(N_EXPERT_GROUPS)]
    grp_rows = []
    for g in range(N_EXPERT_GROUPS):
        m1, i1 = top1(s_g[g], sub, float(gsz))
        m2 = jnp.max(jnp.where(sub == i1, neg, s_g[g]), axis=0, keepdims=True)
        grp_rows.append(m1 + m2)
    grp = jnp.concatenate(grp_rows, axis=0)
    keep = jnp.zeros((N_EXPERT_GROUPS, tm), F32)
    for _ in range(TOPK_GROUPS):
        _, ig = top1(grp, sub, float(N_EXPERT_GROUPS))
        hit = sub == ig
        keep = jnp.where(hit, 1.0, keep)
        grp = jnp.where(hit, neg, grp)
    cand = [jnp.where(keep[g:g + 1, :] > 0.5, s_g[g], neg) for g in range(N_EXPERT_GROUPS)]
    ids = [sub + float(g * gsz) for g in range(N_EXPERT_GROUPS)]
    e_rows, w_rows = [], []
    for _ in range(TOP_K):
        m = cand[0]
        for g in range(1, N_EXPERT_GROUPS):
            m = jnp.maximum(m, cand[g])
        m = jnp.max(m, axis=0, keepdims=True)
        first = jnp.where(cand[0] == m, ids[0], float(N_EXPERTS))
        for g in range(1, N_EXPERT_GROUPS):
            first = jnp.minimum(first, jnp.where(cand[g] == m, ids[g], float(N_EXPERTS)))
        first = jnp.min(first, axis=0, keepdims=True)
        wsum = jnp.zeros((gsz, tm), F32)
        for g in range(N_EXPERT_GROUPS):
            hit = ids[g] == first
            wsum = wsum + jnp.where(hit, sc_g[g], 0.0)
            cand[g] = jnp.where(hit, neg, cand[g])
        e_rows.append(first)
        w_rows.append(jnp.sum(wsum, axis=0, keepdims=True))
    top_e = jnp.concatenate(e_rows, axis=0)
    top_w = jnp.concatenate(w_rows, axis=0)
    denom = jnp.sum(top_w, axis=0, keepdims=True)
    e_ref[...] = top_e.astype(jnp.int32)
    w_out_ref[...] = top_w / denom * ROUTED_SCALE


def _router(x, w_router, bias, tm=512):
    t, d = x.shape
    tm = min(tm, t)
    return pl.pallas_call(
        _router_body,
        out_shape=(jax.ShapeDtypeStruct((TOP_K, t), jnp.int32),
                   jax.ShapeDtypeStruct((TOP_K, t), F32)),
        grid=(t // tm,),
        in_specs=[pl.BlockSpec((tm, d), lambda i: (i, 0)),
                  pl.BlockSpec((d, LANES), lambda i: (0, 0)),
                  pl.BlockSpec((N_EXPERTS, 1), lambda i: (0, 0))],
        out_specs=(pl.BlockSpec((TOP_K, tm), lambda i: (0, i)),
                   pl.BlockSpec((TOP_K, tm), lambda i: (0, i))),
        compiler_params=_params(("parallel",)),
        name="moe_router",
    )(x, jnp.pad(w_router, ((0, 0), (0, LANES - N_EXPERTS))), bias.reshape(N_EXPERTS, 1))


def _moe_list_len(tb):
    rows = tb * TOP_K + N_EXPERTS * MOE_ROWS + 2 * MOE_ROWS
    return -(-rows // MOE_LIST_ALIGN) * MOE_LIST_ALIGN


def _moe_body(nch_ref, cstart_ref, gt_hbm, st_hbm, rw_hbm,
              x8_ref, wg_ref, wu_ref, wd_ref, sg_ref, su_ref, sd_ref,
              out_ref, gt_ref, st_ref, rw_ref, acc_ref, buf_ref, y_ref, sem, *, tb):
    blk = pl.program_id(0)
    e = pl.program_id(1)
    rows_per = MOE_ROWS
    tile = min(256, tb)
    n_list = gt_ref.shape[0]

    def tile_at(ref, off):
        return ref.at[pl.ds(pl.multiple_of(off, SUBLANES), SUBLANES), :]

    def gather(lbase):
        src = gt_ref.at[pl.ds(lbase, rows_per)]
        for r in range(rows_per):
            tile_at(buf_ref, r * SUBLANES)[...] = tile_at(x8_ref, src[r])[...]

    def scatter(lbase):
        dst = st_ref.at[pl.ds(lbase, rows_per)]
        wts = rw_ref.at[pl.ds(lbase, rows_per)]
        for r0 in range(0, rows_per, MOE_SCATTER_GROUP):
            offs = [dst[r0 + u] for u in range(MOE_SCATTER_GROUP)]
            vals = [tile_at(acc_ref, offs[u])[...]
                    + wts[r0 + u] * tile_at(y_ref, (r0 + u) * SUBLANES)[...]
                    for u in range(MOE_SCATTER_GROUP)]
            for u in range(MOE_SCATTER_GROUP):
                tile_at(acc_ref, offs[u])[...] = vals[u]

    @pl.when(e == 0)
    def _start_block():
        copies = [pltpu.make_async_copy(src.at[pl.ds(pl.multiple_of(blk * n_list, MOE_LIST_ALIGN), n_list)],
                                        dst, sem.at[n])
                  for n, (src, dst) in enumerate(((gt_hbm, gt_ref), (st_hbm, st_ref), (rw_hbm, rw_ref)))]
        for cp in copies:
            cp.start()
        for t0 in range(0, tb, tile):
            xb = _tok_rows(x8_ref, t0, tile).astype(BF16)
            h = _silu(_dot(xb, sg_ref[...])) * _dot(xb, su_ref[...])
            _tok_store(acc_ref, t0, _dot(h.astype(BF16), sd_ref[...]))
        tile_at(acc_ref, tb * SUBLANES)[...] = jnp.zeros((SUBLANES, LANES), F32)
        y_ref[...] = jnp.zeros_like(y_ref)
        for cp in copies:
            cp.wait()
        gather(rows_per)

    first = cstart_ref[blk * N_EXPERTS + e]
    count = nch_ref[blk * N_EXPERTS + e]

    def step(g, carry):
        xb = _tok_rows(buf_ref, 0, rows_per).astype(BF16)
        scatter(g * rows_per)
        gather((g + 2) * rows_per)
        h = _silu(_dot(xb, wg_ref[0])) * _dot(xb, wu_ref[0])
        _tok_store(y_ref, 0, _dot(h.astype(BF16), wd_ref[0]))
        return carry

    lax.fori_loop(first, first + count, step, 0)

    @pl.when(e == N_EXPERTS - 1)
    def _finish_block():
        scatter((first + count) * rows_per)
        out_ref[...] = acc_ref[0:tb * SUBLANES, :]


def _moe_experts(x8, nch, cstart, gt, st, rw, wg, wu, wd, sg, su, sd, tb):
    d = wg.shape[1]
    ff = wg.shape[2]
    nblk = x8.shape[0] // (tb * SUBLANES)
    n_list = _moe_list_len(tb)
    grid_spec = pltpu.PrefetchScalarGridSpec(
        num_scalar_prefetch=2,
        grid=(nblk, N_EXPERTS),
        in_specs=[pl.BlockSpec(memory_space=pl.ANY),
                  pl.BlockSpec(memory_space=pl.ANY),
                  pl.BlockSpec(memory_space=pl.ANY),
                  pl.BlockSpec((tb * SUBLANES, LANES), lambda i, e, *_: (i, 0)),
                  pl.BlockSpec((1, d, ff), lambda i, e, *_: (e, 0, 0)),
                  pl.BlockSpec((1, d, ff), lambda i, e, *_: (e, 0, 0)),
                  pl.BlockSpec((1, ff, d), lambda i, e, *_: (e, 0, 0)),
                  pl.BlockSpec((d, ff), lambda i, e, *_: (0, 0)),
                  pl.BlockSpec((d, ff), lambda i, e, *_: (0, 0)),
                  pl.BlockSpec((ff, d), lambda i, e, *_: (0, 0))],
        out_specs=pl.BlockSpec((tb * SUBLANES, LANES), lambda i, e, *_: (i, 0)),
        scratch_shapes=[pltpu.SMEM((n_list,), jnp.int32),
                        pltpu.SMEM((n_list,), jnp.int32),
                        pltpu.SMEM((n_list,), F32),
                        pltpu.VMEM(((tb + 1) * SUBLANES, LANES), F32),
                        pltpu.VMEM((MOE_ROWS * SUBLANES, LANES), F32),
                        pltpu.VMEM((MOE_ROWS * SUBLANES, LANES), F32),
                        pltpu.SemaphoreType.DMA((3,))],
    )
    return pl.pallas_call(
        functools.partial(_moe_body, tb=tb),
        out_shape=jax.ShapeDtypeStruct(x8.shape, F32),
        grid_spec=grid_spec,
        compiler_params=_params(("arbitrary", "arbitrary")),
        name="moe_experts",
    )(nch, cstart, gt, st, rw, x8, wg, wu, wd, sg, su, sd)


def _moe_finish_body(x_ref, acc_ref, g_ref, b_ref, xf_ref, xb_ref):
    z = DN_ALPHA * x_ref[...] + _tok_rows(acc_ref, 0, x_ref.shape[0])
    out = _layer_norm_rows(z, g_ref[...], b_ref[...])
    xf_ref[...] = out
    xb_ref[...] = out.astype(BF16)


def _moe_finish(x, acc8, g, b, tm=512):
    m, d = x.shape
    tm = min(tm, m)
    return pl.pallas_call(
        _moe_finish_body,
        out_shape=(jax.ShapeDtypeStruct((m, d), F32), jax.ShapeDtypeStruct((m, d), BF16)),
        grid=(m // tm,),
        in_specs=[pl.BlockSpec((tm, d), lambda i: (i, 0)),
                  pl.BlockSpec((tm * SUBLANES, LANES), lambda i: (i, 0)),
                  pl.BlockSpec((1, d), lambda i: (0, 0)),
                  pl.BlockSpec((1, d), lambda i: (0, 0))],
        out_specs=(pl.BlockSpec((tm, d), lambda i: (i, 0)),
                   pl.BlockSpec((tm, d), lambda i: (i, 0))),
        compiler_params=_params(("parallel",)),
        name="moe_finish",
    )(x, acc8, g.reshape(1, d), b.reshape(1, d))


def _moe_layer(xf, x8, w_router, router_bias, w_gate, w_up, w_down, ws_gate, ws_up, ws_down,
               ln_g, ln_b, tb):
    t, d = xf.shape
    top_e, top_w = _router(xf, w_router, router_bias)
    nblk = t // tb
    per = tb * TOP_K
    e_flat = top_e.T.reshape(nblk, per)
    w_flat = top_w.T.reshape(nblk, per)
    order = jnp.argsort(e_flat, axis=-1, stable=True).astype(jnp.int32)
    cnt = jnp.sum(e_flat[:, :, None] == jnp.arange(N_EXPERTS, dtype=jnp.int32), axis=1,
                  dtype=jnp.int32)
    nch = (cnt + MOE_ROWS - 1) // MOE_ROWS
    cend = jnp.cumsum(nch, axis=-1, dtype=jnp.int32)
    cstart = cend - nch
    n_list = _moe_list_len(tb)
    chunk = jnp.arange(n_list // MOE_ROWS, dtype=jnp.int32) - 1
    before = cend[:, None, :] <= chunk[None, :, None]
    exp_of = jnp.sum(before, axis=-1, dtype=jnp.int32)
    first_chunk = jnp.sum(jnp.where(before, nch[:, None, :], 0), axis=-1)
    first_src = jnp.sum(jnp.where(before, cnt[:, None, :], 0), axis=-1)
    own = jnp.arange(N_EXPERTS, dtype=jnp.int32) == exp_of[..., None]
    n_rows = jnp.sum(jnp.where(own, cnt[:, None, :], 0), axis=-1)
    row0 = (chunk[None, :] - first_chunk) * MOE_ROWS
    lane = jnp.arange(MOE_ROWS, dtype=jnp.int32)
    valid =((chunk[None, :, None] >= 0) & (row0[..., None] + lane < n_rows[..., None])).reshape(nblk, n_list)
    src = lax.optimization_barrier(
        jnp.clip((first_src + row0)[..., None] + lane, 0, per - 1).reshape(nblk, n_list))
    picked = lax.optimization_barrier(jnp.take_along_axis(order, src, axis=-1))
    tile_off = (picked // TOP_K) * SUBLANES
    gt = jnp.where(valid, tile_off, 0)
    st = jnp.where(valid, tile_off, tb * SUBLANES)
    rw_pad = jnp.where(valid, jnp.take_along_axis(w_flat, picked, axis=-1), 0.0)
    wg, wu, wd = w_gate.astype(BF16), w_up.astype(BF16), w_down.astype(BF16)
    sg, su, sd = ws_gate.astype(BF16), ws_up.astype(BF16), ws_down.astype(BF16)
    acc8 = _moe_experts(x8, nch.reshape(-1), cstart.reshape(-1), gt.reshape(-1), st.reshape(-1),
                        rw_pad.reshape(-1), wg, wu, wd, sg, su, sd, tb)
    return _moe_finish(xf, acc8, ln_g, ln_b)


def kernel(x, fox_w_in, fox_b_f, fox_w_out, gla_w_in, gla_w_gate_up, gla_b_gate, gla_norm, gla_w_out, ssd_w_in, ssd_conv_w, ssd_conv_b, ssd_dt_bias, ssd_a_log, ssd_d, ssd_norm, ssd_w_out, ln1_g, ln1_b, moe_router, moe_router_bias, moe_w_gate, moe_w_up, moe_w_down, moe_ws_gate, moe_ws_up, moe_ws_down, ln2_g, ln2_b):
    batch, seq, d = x.shape
    xf = x.reshape(batch * seq, d)
    xb = xf.astype(BF16)
    tb = seq
    for i in range(DEPTH):
        kind, j = i % N_MIXERS, i // N_MIXERS
        if kind == 0:
            xf, xb, x8 = _fox_layer(xf, xb, fox_w_in[j], fox_b_f[j], fox_w_out[j], ln1_g[i], ln1_b[i],
                                    batch, seq)
        elif kind == 1:
            xf, xb, x8 = _gla_layer(xf, xb, gla_w_in[j], gla_w_gate_up[j], gla_b_gate[j], gla_norm[j],
                                    gla_w_out[j], ln1_g[i], ln1_b[i], batch, seq)
        else:
            xf, xb, x8 = _ssd_layer(xf, xb, ssd_w_in[j], ssd_conv_w[j], ssd_conv_b[j], ssd_dt_bias[j],
                                    ssd_a_log[j], ssd_d[j], ssd_norm[j], ssd_w_out[j], ln1_g[i],
                                    ln1_b[i], batch, seq)
        xf, xb = _moe_layer(xf, x8, moe_router[i], moe_router_bias[i], moe_w_gate[i], moe_w_up[i],
                            moe_w_down[i], moe_ws_gate[i], moe_ws_up[i], moe_ws_down[i],
                            ln2_g[i], ln2_b[i], tb)
    return xf.reshape(batch, seq, d)
```

```python
import functools

import jax
import jax.numpy as jnp
from jax import lax
from jax.experimental import pallas as pl
from jax.experimental.pallas import tpu as pltpu

F32 = jnp.float32
BF16 = jnp.bfloat16
HI = lax.Precision.HIGHEST

D_MODEL = 1024
DEPTH = 4
N_MIXERS = 3
DN_ALPHA = (2 * DEPTH) ** 0.25
LN_EPS = 1e-5
RMS_EPS = 1e-6

FOX_HEADS = 16
FOX_HEAD_DIM = 64
GLA_HEADS = 4
GLA_DK = 512
GLA_DV = 1024
GLA_HK = 128
GLA_HV = 256
GLA_GATE_RANK = 16
GLA_GATE_TAU = 16.0
GLA_CHUNK = 64
SSD_D_INNER = 2048
SSD_HEADS = 32
SSD_GROUPS = 8
SSD_HG = 4
SSD_P = 64
SSD_STATE = 128
SSD_CONV = 4
SSD_CHUNK = 128
N_EXPERTS = 64
TOP_K = 8
N_EXPERT_GROUPS = 8
TOPK_GROUPS = 4
EXPERT_FF = 256
ROUTED_SCALE = 2.5

LANES = 128
SUBLANES = 8
VMEM_LIMIT = 56 * 2 ** 20
MOE_ROWS = 128
MOE_LIST_ALIGN = 1024
MOE_SCATTER_GROUP = 8
MOE_EXPERTS_PER_STEP = 4
SSD_GROUPS_PER_STEP = 2


def _params(sem):
    return pltpu.CompilerParams(dimension_semantics=sem, vmem_limit_bytes=VMEM_LIMIT)


def _sigmoid(x):
    return 1.0 / (1.0 + jnp.exp(-x))


def _silu(x):
    return x * _sigmoid(x)


def _log_sigmoid(x):
    return jnp.minimum(x, 0.0) - jnp.log(1.0 + jnp.exp(-jnp.abs(x)))


def _softplus(x):
    return jnp.maximum(x, 0.0) + jnp.log(1.0 + jnp.exp(-jnp.abs(x)))


def _dot(a, b, precision=None):
    return jnp.dot(a, b, preferred_element_type=F32, precision=precision)


def _dot_nt(a, b, precision=None):
    return lax.dot_general(a, b, (((1,), (1,)), ((), ())), preferred_element_type=F32,
                           precision=precision)


def _dot_tn(a, b, precision=None):
    return lax.dot_general(a, b, (((0,), (0,)), ((), ())), preferred_element_type=F32,
                           precision=precision)


def _tri(n, lower):
    r = lax.broadcasted_iota(jnp.int32, (n, n), 0)
    c = lax.broadcasted_iota(jnp.int32, (n, n), 1)
    return (r >= c) if lower else (r <= c)


def _mm_body(x_ref, w_ref, o_ref, *, precision):
    o_ref[...] = _dot(x_ref[...], w_ref[...], precision).astype(o_ref.dtype)


def _matmul(x, w, out_dtype, name, tm=1024, tn=1024, precision=None):
    m, k = x.shape
    n = w.shape[1]
    tm, tn = min(tm, m), min(tn, n)
    return pl.pallas_call(
        functools.partial(_mm_body, precision=precision),
        out_shape=jax.ShapeDtypeStruct((m, n), out_dtype),
        grid=(n // tn, m // tm),
        in_specs=[pl.BlockSpec((tm, k), lambda j, i: (i, 0)),
                  pl.BlockSpec((k, tn), lambda j, i: (0, j))],
        out_specs=pl.BlockSpec((tm, tn), lambda j, i: (i, j)),
        compiler_params=_params(("parallel", "parallel")),
        name=name,
    )(x, w)


def _layer_norm_rows(z, g, b):
    mu = jnp.mean(z, axis=-1, keepdims=True)
    zc = z - mu
    var = jnp.mean(zc * zc, axis=-1, keepdims=True)
    return zc * lax.rsqrt(var + LN_EPS) * g + b


def _tok_rows(ref, t0, n):
    return jnp.concatenate(
        [ref[pl.ds(t0 * SUBLANES + k, n, stride=SUBLANES), :] for k in range(SUBLANES)], axis=1)


def _tok_store(ref, t0, val):
    for k in range(SUBLANES):
        ref[pl.ds(t0 * SUBLANES + k, val.shape[0], stride=SUBLANES), :] = val[:, k * LANES:(k + 1) * LANES]


def _proj_ln_body(o_ref, w_ref, x_ref, g_ref, b_ref, xf_ref, xb_ref, x8_ref):
    y = _dot(o_ref[...], w_ref[...])
    out = _layer_norm_rows(DN_ALPHA * x_ref[...] + y, g_ref[...], b_ref[...])
    xf_ref[...] = out
    xb_ref[...] = out.astype(BF16)
    _tok_store(x8_ref, 0, out)


def _proj_ln(o, w, x, g, b, name, tm=512):
    m, k = o.shape
    d = w.shape[1]
    assert d == SUBLANES * LANES
    tm = min(tm, m)
    return pl.pallas_call(
        _proj_ln_body,
        out_shape=(jax.ShapeDtypeStruct((m, d), F32), jax.ShapeDtypeStruct((m, d), BF16),
                   jax.ShapeDtypeStruct((m * SUBLANES, LANES), F32)),
        grid=(m // tm,),
        in_specs=[pl.BlockSpec((tm, k), lambda i: (i, 0)),
                  pl.BlockSpec((k, d), lambda i: (0, 0)),
                  pl.BlockSpec((tm, d), lambda i: (i, 0)),
                  pl.BlockSpec((1, d), lambda i: (0, 0)),
                  pl.BlockSpec((1, d), lambda i: (0, 0))],
        out_specs=(pl.BlockSpec((tm, d), lambda i: (i, 0)),
                   pl.BlockSpec((tm, d), lambda i: (i, 0)),
                   pl.BlockSpec((tm * SUBLANES, LANES), lambda i: (i, 0))),
        compiler_params=_params(("parallel",)),
        name=name,
    )(o, w, x, g.reshape(1, d), b.reshape(1, d))


def _fox_gate_body(x_ref, w_ref, b_ref, c_ref, *, seq):
    f = _dot(x_ref[...], w_ref[...], HI)
    z = f.T[:FOX_HEADS, :] + b_ref[...]
    lf = _log_sigmoid(z)
    upper = _tri(LANES, lower=False).astype(F32)
    carry = jnp.zeros((FOX_HEADS, 1), F32)
    for blk in range(seq // LANES):
        seg = _dot(lf[:, blk * LANES:(blk + 1) * LANES], upper, HI) + carry
        c_ref[0, :, blk * LANES:(blk + 1) * LANES] = seg
        carry = seg[:, LANES - 1:LANES]


def _fox_gate(x, w_f, b_f, batch, seq):
    d = x.shape[1]
    return pl.pallas_call(
        functools.partial(_fox_gate_body, seq=seq),
        out_shape=jax.ShapeDtypeStruct((batch, FOX_HEADS, seq), F32),
        grid=(batch,),
        in_specs=[pl.BlockSpec((seq, d), lambda b: (b, 0)),
                  pl.BlockSpec((d, LANES), lambda b: (0, 0)),
                  pl.BlockSpec((FOX_HEADS, 1), lambda b: (0, 0))],
        out_specs=pl.BlockSpec((1, FOX_HEADS, seq), lambda b: (b, 0, 0)),
        compiler_params=_params(("parallel",)),
        name="fox_gate",
    )(x, w_f, b_f)


def _fox_attn_body(q_ref, k_ref, v_ref, c_ref, o_ref, kh_ref, qh_ref, vt_ref, acc_ref, ml_ref, ot_ref,
                   *, seq, t):
    nq = seq // t
    hd = FOX_HEAD_DIM
    lane = lax.broadcasted_iota(jnp.int32, (1, LANES), 1)
    valid = _tri(t, lower=False)
    scale = hd ** -0.5
    for h in range(2):
        own = (lane >= h * hd) & (lane < (h + 1) * hd)
        aug = (1 - h) * hd
        c_col = c_ref[0, 0, :, h:h + 1]
        hi = c_col.astype(BF16).astype(F32)
        r1 = c_col - hi
        mid = r1.astype(BF16).astype(F32)
        lo = (r1 - mid).astype(BF16).astype(F32)
        sel = [(lane == aug + n).astype(F32) for n in range(3)]
        c_lanes = hi * sel[0] + mid * sel[1] + lo * sel[2]
        own_f = own.astype(F32)
        kh_ref[h] = (k_ref[...].astype(F32) * (scale * own_f) + c_lanes).astype(BF16)
        qh_ref[h] = (q_ref[...].astype(F32) * own_f - (sel[0] + sel[1] + sel[2])).astype(BF16)
    vt_ref[...] = v_ref[...].astype(F32).T.astype(BF16)

    def scores(tile):
        j, i, h = tile
        s = _dot_nt(kh_ref[h, j * t:(j + 1) * t, :], qh_ref[h, i * t:(i + 1) * t, :])
        return jnp.where(valid, s, -jnp.inf) if i == j else s

    def update(tile, s):
        j, i, h = tile
        rows = slice(h * hd, (h + 1) * hd)
        m_new = jnp.max(s, axis=0, keepdims=True)
        if j > 0:
            m_old = ml_ref[i, h:h + 1, :]
            m_new = jnp.maximum(m_old, m_new)
        p = jnp.exp(s - m_new)
        l_new = jnp.sum(p, axis=0, keepdims=True)
        acc = _dot(vt_ref[rows, j * t:(j + 1) * t], p.astype(BF16))
        if j > 0:
            a = jnp.exp(m_old - m_new)
            l_new = a * ml_ref[i, 2 + h:3 + h, :] + l_new
            acc = a * acc_ref[i, rows, :] + acc
        if i == j:
            ot_ref[rows, i * t:(i + 1) * t] = acc / l_new
        else:
            ml_ref[i, h:h + 1, :] = m_new
            ml_ref[i, 2 + h:3 + h, :] = l_new
            acc_ref[i, rows, :] = acc

    tiles = [(j, i, h) for j in range(nq) for i in range(j, nq) for h in range(2)]
    ahead = 2
    pending = []
    for idx in range(len(tiles) + ahead):
        if idx < len(tiles):
            pending.append(scores(tiles[idx]))
        if idx >= ahead:
            update(tiles[idx - ahead], pending.pop(0))
    o_ref[...] = ot_ref[...].T.astype(o_ref.dtype)


def _fox_attn(qkv, c, batch, seq, t=256):
    t = min(t, seq)
    pairs = FOX_HEADS // 2
    c_cols = c.reshape(batch, pairs, 2, seq).transpose(0, 1, 3, 2)
    return pl.pallas_call(
        functools.partial(_fox_attn_body, seq=seq, t=t),
        out_shape=jax.ShapeDtypeStruct((batch * seq, D_MODEL), BF16),
        grid=(batch, pairs),
        in_specs=[pl.BlockSpec((seq, LANES), lambda b, p: (b, p)),
                  pl.BlockSpec((seq, LANES), lambda b, p: (b, pairs + p)),
                  pl.BlockSpec((seq, LANES), lambda b, p: (b, 2 * pairs + p)),
                  pl.BlockSpec((1, 1, seq, 2), lambda b, p: (b, p, 0, 0))],
        out_specs=pl.BlockSpec((seq, LANES), lambda b, p: (b, p)),
        scratch_shapes=[pltpu.VMEM((2, seq, LANES), BF16),
                        pltpu.VMEM((2, seq, LANES), BF16),
                        pltpu.VMEM((LANES, seq), BF16),
                        pltpu.VMEM((seq // t, LANES, t), F32),
                        pltpu.VMEM((seq // t, SUBLANES, t), F32),
                        pltpu.VMEM((LANES, seq), F32)],
        compiler_params=_params(("parallel", "parallel")),
        name="fox_attn",
    )(qkv, qkv, qkv, c_cols)


def _fox_layer(xf, xb, w_in, b_f, w_out, ln_g, ln_b, batch, seq):
    d = D_MODEL
    qkv = _matmul(xb, w_in[:, :3 * d].astype(BF16), BF16, "fox_qkv")
    w_f = jnp.pad(w_in[:, 3 * d:], ((0, 0), (0, LANES - FOX_HEADS)))
    c = _fox_gate(xf, w_f, b_f.reshape(FOX_HEADS, 1), batch, seq)
    o = _fox_attn(qkv, c, batch, seq)
    return _proj_ln(o, w_out.astype(BF16), xf, ln_g, ln_b, "fox_out_ln")


def _gla_body(q_ref, k_ref, v_ref, r_ref, gl_ref, wgu_ref, bg_ref, ng_ref, o_ref, bc_ref, st_ref, *, seq):
    ch = GLA_CHUNK
    tile = min(256, seq)
    r_i = lax.broadcasted_iota(jnp.int32, (tile, tile), 0)
    c_i = lax.broadcasted_iota(jnp.int32, (tile, tile), 1)
    same_chunk_lower = ((r_i >= c_i) & (r_i // ch == c_i // ch)).astype(F32)
    for t0 in range(0, seq, tile):
        gate = _dot(gl_ref[t0:t0 + tile, :], wgu_ref[...], HI) + bg_ref[...]
        log_a = _log_sigmoid(gate) * (1.0 / GLA_GATE_TAU)
        bc_ref[t0:t0 + tile, :] = _dot(same_chunk_lower, log_a, HI)
    lower = _tri(ch, lower=True)
    st_ref[...] = jnp.zeros_like(st_ref)
    heads = range(GLA_HEADS)

    def chunk(i, carry):
        rows = pl.ds(pl.multiple_of(i * ch, ch), ch)
        bcum = bc_ref[rows, :]
        b_last = bcum[ch - 1:ch, :]
        q = q_ref[rows, :].astype(F32) * (GLA_HK ** -0.5)
        k = k_ref[rows, :].astype(F32)
        q_dec = (q * jnp.exp(bcum)).astype(BF16)
        k_inv = (k * jnp.exp(-bcum)).astype(BF16)
        k_end = (k * jnp.exp(b_last - bcum)).astype(BF16)
        decay = jnp.exp(b_last)
        kl = [slice(h * GLA_HK, (h + 1) * GLA_HK) for h in heads]
        vl = [slice(h * GLA_HV, (h + 1) * GLA_HV) for h in heads]
        v = [v_ref[rows, vl[h]] for h in heads]
        att = [jnp.where(lower, _dot_nt(q_dec[:, kl[h]], k_inv[:, kl[h]]), 0.0).astype(BF16)
               for h in heads]
        state_t = [st_ref[h] for h in heads]
        o_inter = [_dot_nt(q_dec[:, kl[h]], state_t[h].astype(BF16)) for h in heads]
        o_intra = [_dot(att[h], v[h]) for h in heads]
        kv_t = [_dot_tn(v[h], k_end[:, kl[h]]) for h in heads]
        for h in heads:
            st_ref[h] = state_t[h] * decay[:, kl[h]] + kv_t[h]
            o = o_intra[h] + o_inter[h]
            ms = jnp.mean(o * o, axis=-1, keepdims=True)
            on = o * lax.rsqrt(ms + RMS_EPS) * ng_ref[...]
            o_ref[rows, vl[h]] = (on * _silu(r_ref[rows, vl[h]].astype(F32))).astype(o_ref.dtype)
        return carry

    lax.fori_loop(0, seq // ch, chunk, 0)


def _gla_mix(proj, g_low, w_gu, b_gate, norm_g, batch, seq):
    return pl.pallas_call(
        functools.partial(_gla_body, seq=seq),
        out_shape=jax.ShapeDtypeStruct((batch * seq, GLA_DV), BF16),
        grid=(batch,),
        in_specs=[pl.BlockSpec((seq, GLA_DK), lambda b: (b, 0)),
                  pl.BlockSpec((seq, GLA_DK), lambda b: (b, 1)),
                  pl.BlockSpec((seq, GLA_DV), lambda b: (b, 1)),
                  pl.BlockSpec((seq, GLA_DV), lambda b: (b, 2)),
                  pl.BlockSpec((seq, LANES), lambda b: (b, 0)),
                  pl.BlockSpec((LANES, GLA_DK), lambda b: (0, 0)),
                  pl.BlockSpec((1, GLA_DK), lambda b: (0, 0)),
                  pl.BlockSpec((1, GLA_HV), lambda b: (0, 0))],
        out_specs=pl.BlockSpec((seq, GLA_DV), lambda b: (b, 0)),
        scratch_shapes=[pltpu.VMEM((seq, GLA_DK), F32),
                        pltpu.VMEM((GLA_HEADS, GLA_HV, GLA_HK), F32)],
        compiler_params=_params(("parallel",)),
        name="gla_mix",
    )(proj, proj, proj, proj, g_low, w_gu, b_gate.reshape(1, GLA_DK), norm_g.reshape(1, GLA_HV))


def _gla_layer(xf, xb, w_in, w_gate_up, b_gate, norm_g, w_out, ln_g, ln_b, batch, seq):
    n_main = 2 * GLA_DK + 2 * GLA_DV
    proj = _matmul(xb, w_in[:, :n_main].astype(BF16), BF16, "gla_proj")
    w_g = jnp.pad(w_in[:, n_main:], ((0, 0), (0, LANES - GLA_GATE_RANK)))
    g_low = _matmul(xf, w_g, F32, "gla_gate", tn=LANES, precision=HI)
    w_gu = jnp.pad(w_gate_up, ((0, LANES - GLA_GATE_RANK), (0, 0)))
    y = _gla_mix(proj, g_low, w_gu, b_gate, norm_g, batch, seq)
    return _proj_ln(y, w_out.astype(BF16), xf, ln_g, ln_b, "gla_out_ln")


def _ssd_gate_body(x_ref, w_ref, bias_ref, alog_ref, dt_ref, ac_ref, dtt_ref, act_ref, *, seq):
    ch = SSD_CHUNK
    raw = _dot(x_ref[...], w_ref[...], HI)
    dt = _softplus(raw + bias_ref[...])
    a = dt * (-jnp.exp(alog_ref[...]))
    lower_f = _tri(ch, lower=True).astype(F32)
    dt_ref[...] = dt
    dtt_ref[0] = dt.T
    for c in range(seq // ch):
        acum = _dot(lower_f, a[c * ch:(c + 1) * ch, :], HI)
        ac_ref[c * ch:(c + 1) * ch, :] = acum
        act_ref[0, :, c * ch:(c + 1) * ch] = acum.T


def _ssd_gate(x, w_dt, dt_bias, a_log, batch, seq):
    d = x.shape[1]
    col = jax.ShapeDtypeStruct((batch * seq, LANES), F32)
    row = jax.ShapeDtypeStruct((batch, LANES, seq), F32)
    return pl.pallas_call(
        functools.partial(_ssd_gate_body, seq=seq),
        out_shape=(col, col, row, row),
        grid=(batch,),
        in_specs=[pl.BlockSpec((seq, d), lambda b: (b, 0)),
                  pl.BlockSpec((d, LANES), lambda b: (0, 0)),
                  pl.BlockSpec((1, LANES), lambda b: (0, 0)),
                  pl.BlockSpec((1, LANES), lambda b: (0, 0))],
        out_specs=(pl.BlockSpec((seq, LANES), lambda b: (b, 0)),
                   pl.BlockSpec((seq, LANES), lambda b: (b, 0)),
                   pl.BlockSpec((1, LANES, seq), lambda b: (b, 0, 0)),
                   pl.BlockSpec((1, LANES, seq), lambda b: (b, 0, 0))),
        compiler_params=_params(("parallel",)),
        name="ssd_gate",
    )(x, w_dt, dt_bias, a_log)


def _expand_heads(cols, width):
    rows = cols.shape[0]
    lane = lax.broadcasted_iota(jnp.int32, (rows, SSD_HG * width), 1)
    out = jnp.broadcast_to(cols[:, SSD_HG - 1:SSD_HG], (rows, SSD_HG * width))
    for h in range(SSD_HG - 2, -1, -1):
        out = jnp.where(lane < (h + 1) * width, jnp.broadcast_to(cols[:, h:h + 1], out.shape), out)
    return out


def _ssd_body(z_ref, xr_ref, br_ref, cr_ref, cwx_ref, cwb_ref, cwc_ref, cbx_ref, cbb_ref, cbc_ref,
              dt4_ref, ac4_ref, dtt_ref, act_ref, dskip_ref, ng_ref, o_ref,
              padx_ref, padb_ref, xs_ref, bm_ref, cm_ref, st_ref, *, seq):
    ch = SSD_CHUNK
    hp = SSD_HG * SSD_P
    n = SSD_STATE
    pad = SUBLANES

    def conv(raw_ref, pad_ref, w_ref, b_ref, dst_ref):
        pad_ref[0:pad, :] = jnp.zeros((pad, pad_ref.shape[1]), F32)
        pad_ref[pad:pad + seq, :] = raw_ref[...].astype(F32)
        tile = min(256, seq)
        width = min(256, pad_ref.shape[1])
        for c0 in range(0, pad_ref.shape[1], width):
            cols = slice(c0, c0 + width)
            for t0 in range(0, seq, tile):
                acc = jnp.broadcast_to(b_ref[:, cols], (tile, width))
                for j in range(SSD_CONV):
                    off = t0 + pad - (SSD_CONV - 1) + j
                    acc = acc + w_ref[j:j + 1, cols] * pad_ref[off:off + tile, cols]
                dst_ref[t0:t0 + tile, cols] = _silu(acc)

    conv(xr_ref, padx_ref, cwx_ref, cbx_ref, xs_ref)
    conv(br_ref, padb_ref, cwb_ref, cbb_ref, bm_ref)
    conv(cr_ref, padb_ref, cwc_ref, cbc_ref, cm_ref)

    lower = _tri(ch, lower=True)
    lane = lax.broadcasted_iota(jnp.int32, (1, hp), 1)
    st_ref[...] = jnp.zeros_like(st_ref)

    def group_chunk(gi, c, rows):
        cols = slice(gi * hp, (gi + 1) * hp)
        ncols = slice(gi * n, (gi + 1) * n)
        xs = xs_ref[rows, cols]
        bm = bm_ref[rows, ncols].astype(BF16)
        cm = cm_ref[rows, ncols].astype(BF16)
        dt4 = dt4_ref[0, gi, rows, :]
        ac4 = ac4_ref[0, gi, rows, :]
        cb = _dot_nt(cm, bm)
        yield
        state = st_ref[gi]
        y_off = _dot(cm, state.astype(BF16))
        yield
        to_end = jnp.exp(ac4[ch - 1:ch, :] - ac4)
        w_state = _expand_heads(dt4 * to_end, SSD_P)
        e_acum = _expand_heads(jnp.exp(ac4), SSD_P)
        new_state = _dot_tn(bm, (xs * w_state).astype(BF16))
        yield
        st_ref[gi] = state * e_acum[ch - 1:ch, :] + new_state
        xs_b = xs.astype(BF16)
        y = xs * dskip_ref[:, cols] + y_off * e_acum
        for h in range(SSD_HG):
            a_col = jnp.broadcast_to(ac4[:, h:h + 1], (ch, ch))
            a_row = act_ref[0, gi, h, pl.ds(c, 1), :]
            d_row = dtt_ref[0, gi, h, pl.ds(c, 1), :]
            decay = jnp.exp(jnp.where(lower, a_col - a_row, -jnp.inf))
            m_h = (cb * decay * d_row).astype(BF16)
            head = (lane >= h * SSD_P) & (lane < (h + 1) * SSD_P)
            y = y + _dot(m_h, jnp.where(head, xs_b, jnp.zeros_like(xs_b)))
            yield
        y = y * _silu(z_ref[rows, cols].astype(F32))
        ms = jnp.mean(y * y, axis=-1, keepdims=True)
        o_ref[rows, cols] = (y * lax.rsqrt(ms + RMS_EPS) * ng_ref[:, cols]).astype(o_ref.dtype)

    def chunk(c, carry):
        rows = pl.ds(pl.multiple_of(c * ch, ch), ch)
        running = [group_chunk(gi, c, rows) for gi in range(SSD_GROUPS_PER_STEP)]
        while running:
            for gen in list(running):
                if next(gen, "done") == "done":
                    running.remove(gen)
        return carry

    lax.fori_loop(0, seq // ch, chunk, 0)


def _ssd_mix(proj, conv_w, conv_b, dt4, ac4, dtt, act, d_exp, norm_g, batch, seq):
    per = SSD_GROUPS_PER_STEP
    steps = SSD_GROUPS // per
    hp = per * SSD_HG * SSD_P
    n = per * SSD_STATE
    nz = SSD_D_INNER // hp
    nb = 2 * SSD_D_INNER // n
    nc = nb + steps
    cb0 = SSD_D_INNER // n
    nch = seq // SSD_CHUNK
    return pl.pallas_call(
        functools.partial(_ssd_body, seq=seq),
        out_shape=jax.ShapeDtypeStruct((batch * seq, SSD_D_INNER), BF16),
        grid=(batch, steps),
        in_specs=[pl.BlockSpec((seq, hp), lambda b, i: (b, i)),
                  pl.BlockSpec((seq, hp), lambda b, i: (b, nz + i)),
                  pl.BlockSpec((seq, n), lambda b, i: (b, nb + i)),
                  pl.BlockSpec((seq, n), lambda b, i: (b, nc + i)),
                  pl.BlockSpec((SSD_CONV, hp), lambda b, i: (0, i)),
                  pl.BlockSpec((SSD_CONV, n), lambda b, i: (0, cb0 + i)),
                  pl.BlockSpec((SSD_CONV, n), lambda b, i: (0, cb0 + steps + i)),
                  pl.BlockSpec((1, hp), lambda b, i: (0, i)),
                  pl.BlockSpec((1, n), lambda b, i: (0, cb0 + i)),
                  pl.BlockSpec((1, n), lambda b, i: (0, cb0 + steps + i)),
                  pl.BlockSpec((1, per, seq, SSD_HG), lambda b, i: (b, i, 0, 0)),
                  pl.BlockSpec((1, per, seq, SSD_HG), lambda b, i: (b, i, 0, 0)),
                  pl.BlockSpec((1, per, SSD_HG, nch, SSD_CHUNK), lambda b, i: (b, i, 0, 0, 0)),
                  pl.BlockSpec((1, per, SSD_HG, nch, SSD_CHUNK), lambda b, i: (b, i, 0, 0, 0)),
                  pl.BlockSpec((1, hp), lambda b, i: (0, i)),
                  pl.BlockSpec((1, hp), lambda b, i: (0, i))],
        out_specs=pl.BlockSpec((seq, hp), lambda b, i: (b, i)),
        scratch_shapes=[pltpu.VMEM((seq + SUBLANES, hp), F32),
                        pltpu.VMEM((seq + SUBLANES, n), F32),
                        pltpu.VMEM((seq, hp), F32),
                        pltpu.VMEM((seq, n), F32),
                        pltpu.VMEM((seq, n), F32),
                        pltpu.VMEM((per, SSD_STATE, SSD_HG * SSD_P), F32)],
        compiler_params=_params(("parallel", "parallel")),
        name="ssd_mix",
    )(proj, proj, proj, proj, conv_w, conv_w, conv_w, conv_b, conv_b, conv_b,
      dt4, ac4, dtt, act, d_exp, norm_g)


def _ssd_layer(xf, xb, w_in, conv_w, conv_b, dt_bias, a_log, d_skip, norm_g, w_out, ln_g, ln_b,
               batch, seq):
    n_main = SSD_D_INNER + (SSD_D_INNER + 2 * SSD_GROUPS * SSD_STATE)
    proj = _matmul(xb, w_in[:, :n_main].astype(BF16), BF16, "ssd_proj")
    lane_pad = ((0, 0), (0, LANES - SSD_HEADS))
    w_dt = jnp.pad(w_in[:, n_main:], lane_pad)
    dt_c, ac_c, dt_r, ac_r = _ssd_gate(xf, w_dt, jnp.pad(dt_bias.reshape(1, -1), lane_pad),
                                       jnp.pad(a_log.reshape(1, -1), lane_pad), batch, seq)

    def cols(t):
        return t[:, :SSD_HEADS].reshape(batch, seq, SSD_GROUPS, SSD_HG).transpose(0, 2, 1, 3)

    def rows(t):
        return t[:, :SSD_HEADS].reshape(batch, SSD_GROUPS, SSD_HG, seq // SSD_CHUNK, SSD_CHUNK)

    d_exp = jnp.repeat(d_skip, SSD_P).reshape(1, SSD_D_INNER)
    y = _ssd_mix(proj, conv_w, conv_b.reshape(1, -1), cols(dt_c), cols(ac_c), rows(dt_r), rows(ac_r),
                 d_exp, norm_g.reshape(1, SSD_D_INNER), batch, seq)
    return _proj_ln(y, w_out.astype(BF16), xf, ln_g, ln_b, "ssd_out_ln")


def _router_body(x_ref, w_ref, bias_ref, e_ref, w_out_ref):
    tm = x_ref.shape[0]
    gsz = N_EXPERTS // N_EXPERT_GROUPS
    logits = _dot(x_ref[...], w_ref[...], HI).T[:N_EXPERTS, :]
    scores = _sigmoid(logits)
    sel = scores + bias_ref[...]
    sub = lax.broadcasted_iota(jnp.int32, (gsz, tm), 0).astype(F32)
    neg = jnp.float32(-jnp.inf)

    def top1(v, idx, sentinel):
        m = jnp.max(v, axis=0, keepdims=True)
        i = jnp.min(jnp.where(v == m, idx, sentinel), axis=0, keepdims=True)
        return m, i

    s_g = [sel[g * gsz:(g + 1) * gsz, :] for g in range(N_EXPERT_GROUPS)]
    sc_g = [scores[g * gsz:(g + 1) * gsz, :] for g in range(N_EXPERT_GROUPS)]
    grp_rows = []
    for g in range(N_EXPERT_GROUPS):
        m1, i1 = top1(s_g[g], sub, float(gsz))
        m2 = jnp.max(jnp.where(sub == i1, neg, s_g[g]), axis=0, keepdims=True)
        grp_rows.append(m1 + m2)
    grp = jnp.concatenate(grp_rows, axis=0)
    keep = jnp.zeros((N_EXPERT_GROUPS, tm), F32)
    for _ in range(TOPK_GROUPS):
        _, ig = top1(grp, sub, float(N_EXPERT_GROUPS))
        hit = sub == ig
        keep = jnp.where(hit, 1.0, keep)
        grp = jnp.where(hit, neg, grp)
    cand = [jnp.where(keep[g:g + 1, :] > 0.5, s_g[g], neg) for g in range(N_EXPERT_GROUPS)]
    ids = [sub + float(g * gsz) for g in range(N_EXPERT_GROUPS)]
    e_rows, w_rows = [], []
    for _ in range(TOP_K):
        m = cand[0]
        for g in range(1, N_EXPERT_GROUPS):
            m = jnp.maximum(m, cand[g])
        m = jnp.max(m, axis=0, keepdims=True)
        first = jnp.where(cand[0] == m, ids[0], float(N_EXPERTS))
        for g in range(1, N_EXPERT_GROUPS):
            first = jnp.minimum(first, jnp.where(cand[g] == m, ids[g], float(N_EXPERTS)))
        first = jnp.min(first, axis=0, keepdims=True)
        wsum = jnp.zeros((gsz, tm), F32)
        for g in range(N_EXPERT_GROUPS):
            hit = ids[g] == first
            wsum = wsum + jnp.where(hit, sc_g[g], 0.0)
            cand[g] = jnp.where(hit, neg, cand[g])
        e_rows.append(first)
        w_rows.append(jnp.sum(wsum, axis=0, keepdims=True))
    top_e = jnp.concatenate(e_rows, axis=0)
    top_w = jnp.concatenate(w_rows, axis=0)
    denom = jnp.sum(top_w, axis=0, keepdims=True)
    e_ref[...] = top_e.astype(jnp.int32)
    w_out_ref[...] = top_w / denom * ROUTED_SCALE


def _router(x, w_router, bias, tm=512):
    t, d = x.shape
    tm = min(tm, t)
    return pl.pallas_call(
        _router_body,
        out_shape=(jax.ShapeDtypeStruct((TOP_K, t), jnp.int32),
                   jax.ShapeDtypeStruct((TOP_K, t), F32)),
        grid=(t // tm,),
        in_specs=[pl.BlockSpec((tm, d), lambda i: (i, 0)),
                  pl.BlockSpec((d, LANES), lambda i: (0, 0)),
                  pl.BlockSpec((N_EXPERTS, 1), lambda i: (0, 0))],
        out_specs=(pl.BlockSpec((TOP_K, tm), lambda i: (0, i)),
                   pl.BlockSpec((TOP_K, tm), lambda i: (0, i))),
        compiler_params=_params(("parallel",)),
        name="moe_router",
    )(x, jnp.pad(w_router, ((0, 0), (0, LANES - N_EXPERTS))), bias.reshape(N_EXPERTS, 1))


def _moe_list_len(tb):
    rows = tb * TOP_K + N_EXPERTS * MOE_ROWS + 2 * MOE_ROWS
    return -(-rows // MOE_LIST_ALIGN) * MOE_LIST_ALIGN


def _moe_body(nch_ref, cstart_ref, gt_hbm, st_hbm, rw_hbm,
              x8_ref, wg_ref, wu_ref, wd_ref, sg_ref, su_ref, sd_ref,
              out_ref, gt_ref, st_ref, rw_ref, buf_ref, y_ref, sem, *, tb):
    blk = pl.program_id(0)
    grp = pl.program_id(1)
    rows_per = MOE_ROWS
    tile = min(256, tb)
    n_list = gt_ref.shape[0]
    acc_ref = out_ref.at[0]

    def tile_at(ref, off):
        return ref.at[pl.ds(pl.multiple_of(off, SUBLANES), SUBLANES), :]

    def gather(lbase):
        src = gt_ref.at[pl.ds(lbase, rows_per)]
        for r in range(rows_per):
            tile_at(buf_ref, r * SUBLANES)[...] = tile_at(x8_ref, src[r])[...]

    def scatter(lbase):
        dst = st_ref.at[pl.ds(lbase, rows_per)]
        wts = rw_ref.at[pl.ds(lbase, rows_per)]
        for r0 in range(0, rows_per, MOE_SCATTER_GROUP):
            offs = [dst[r0 + u] for u in range(MOE_SCATTER_GROUP)]
            vals = [tile_at(acc_ref, offs[u])[...]
                    + wts[r0 + u] * tile_at(y_ref, (r0 + u) * SUBLANES)[...]
                    for u in range(MOE_SCATTER_GROUP)]
            for u in range(MOE_SCATTER_GROUP):
                tile_at(acc_ref, offs[u])[...] = vals[u]

    @pl.when(grp == 0)
    def _start_block():
        copies = [pltpu.make_async_copy(src.at[pl.ds(pl.multiple_of(blk * n_list, MOE_LIST_ALIGN), n_list)],
                                        dst, sem.at[n])
                  for n, (src, dst) in enumerate(((gt_hbm, gt_ref), (st_hbm, st_ref), (rw_hbm, rw_ref)))]
        for cp in copies:
            cp.start()
        for t0 in range(0, tb, tile):
            xb = _tok_rows(x8_ref, t0, tile).astype(BF16)
            h = _silu(_dot(xb, sg_ref[...])) * _dot(xb, su_ref[...])
            _tok_store(acc_ref, t0, _dot(h.astype(BF16), sd_ref[...]))
        tile_at(acc_ref, tb * SUBLANES)[...] = jnp.zeros((SUBLANES, LANES), F32)
        y_ref[...] = jnp.zeros_like(y_ref)
        for cp in copies:
            cp.wait()
        gather(rows_per)

    for ee in range(MOE_EXPERTS_PER_STEP):
        e = blk * N_EXPERTS + grp * MOE_EXPERTS_PER_STEP + ee
        first = cstart_ref[e]

        def step(g, carry, ee=ee):
            xb = _tok_rows(buf_ref, 0, rows_per).astype(BF16)
            scatter(g * rows_per)
            gather((g + 2) * rows_per)
            h = _silu(_dot(xb, wg_ref[ee])) * _dot(xb, wu_ref[ee])
            _tok_store(y_ref, 0, _dot(h.astype(BF16), wd_ref[ee]))
            return carry

        lax.fori_loop(first, first + nch_ref[e], step, 0)

    @pl.when(grp == pl.num_programs(1) - 1)
    def _finish_block():
        last = blk * N_EXPERTS + N_EXPERTS - 1
        scatter((cstart_ref[last] + nch_ref[last]) * rows_per)


def _moe_experts(x8, nch, cstart, gt, st, rw, wg, wu, wd, sg, su, sd, tb):
    d = wg.shape[1]
    ff = wg.shape[2]
    nblk = x8.shape[0] // (tb * SUBLANES)
    n_list = _moe_list_len(tb)
    per_step = MOE_EXPERTS_PER_STEP
    grid_spec = pltpu.PrefetchScalarGridSpec(
        num_scalar_prefetch=2,
        grid=(nblk, N_EXPERTS // per_step),
        in_specs=[pl.BlockSpec(memory_space=pl.ANY),
                  pl.BlockSpec(memory_space=pl.ANY),
                  pl.BlockSpec(memory_space=pl.ANY),
                  pl.BlockSpec((tb * SUBLANES, LANES), lambda i, e, *_: (i, 0)),
                  pl.BlockSpec((per_step, d, ff), lambda i, e, *_: (e, 0, 0)),
                  pl.BlockSpec((per_step, d, ff), lambda i, e, *_: (e, 0, 0)),
                  pl.BlockSpec((per_step, ff, d), lambda i, e, *_: (e, 0, 0)),
                  pl.BlockSpec((d, ff), lambda i, e, *_: (0, 0)),
                  pl.BlockSpec((d, ff), lambda i, e, *_: (0, 0)),
                  pl.BlockSpec((ff, d), lambda i, e, *_: (0, 0))],
        out_specs=pl.BlockSpec((1, (tb + 1) * SUBLANES, LANES), lambda i, e, *_: (i, 0, 0)),
        scratch_shapes=[pltpu.SMEM((n_list,), jnp.int32),
                        pltpu.SMEM((n_list,), jnp.int32),
                        pltpu.SMEM((n_list,), F32),
                        pltpu.VMEM((MOE_ROWS * SUBLANES, LANES), F32),
                        pltpu.VMEM((MOE_ROWS * SUBLANES, LANES), F32),
                        pltpu.SemaphoreType.DMA((3,))],
    )
    return pl.pallas_call(
        functools.partial(_moe_body, tb=tb),
        out_shape=jax.ShapeDtypeStruct((nblk, (tb + 1) * SUBLANES, LANES), F32),
        grid_spec=grid_spec,
        compiler_params=_params(("arbitrary", "arbitrary")),
        name="moe_experts",
    )(nch, cstart, gt, st, rw, x8, wg, wu, wd, sg, su, sd)


def _moe_finish_body(x_ref, acc_ref, g_ref, b_ref, xf_ref, xb_ref):
    z = DN_ALPHA * x_ref[...] + _tok_rows(acc_ref.at[0], 0, x_ref.shape[0])
    out = _layer_norm_rows(z, g_ref[...], b_ref[...])
    xf_ref[...] = out
    xb_ref[...] = out.astype(BF16)


def _moe_finish(x, acc8, g, b, tb, tm=512):
    m, d = x.shape
    tm = min(tm, tb)
    per_blk = tb // tm
    return pl.pallas_call(
        _moe_finish_body,
        out_shape=(jax.ShapeDtypeStruct((m, d), F32), jax.ShapeDtypeStruct((m, d), BF16)),
        grid=(m // tm,),
        in_specs=[pl.BlockSpec((tm, d), lambda i: (i, 0)),
                  pl.BlockSpec((1, tm * SUBLANES, LANES), lambda i: (i // per_blk, i % per_blk, 0)),
                  pl.BlockSpec((1, d), lambda i: (0, 0)),
                  pl.BlockSpec((1, d), lambda i: (0, 0))],
        out_specs=(pl.BlockSpec((tm, d), lambda i: (i, 0)),
                   pl.BlockSpec((tm, d), lambda i: (i, 0))),
        compiler_params=_params(("parallel",)),
        name="moe_finish",
    )(x, acc8, g.reshape(1, d), b.reshape(1, d))


def _moe_layer(xf, x8, w_router, router_bias, w_gate, w_up, w_down, ws_gate, ws_up, ws_down,
               ln_g, ln_b, tb):
    t, d = xf.shape
    top_e, top_w = _router(xf, w_router, router_bias)
    nblk = t // tb
    per = tb * TOP_K
    e_flat = top_e.T.reshape(nblk, per)
    w_flat = top_w.T.reshape(nblk, per)
    order = jnp.argsort(e_flat, axis=-1, stable=True).astype(jnp.int32)
    cnt = jnp.sum(e_flat[:, :, None] == jnp.arange(N_EXPERTS, dtype=jnp.int32), axis=1,
                  dtype=jnp.int32)
    nch = (cnt + MOE_ROWS - 1) // MOE_ROWS
    cend = jnp.cumsum(nch, axis=-1, dtype=jnp.int32)
    cstart = cend - nch
    n_list = _moe_list_len(tb)
    chunk = jnp.arange(n_list // MOE_ROWS, dtype=jnp.int32) - 1
    before = cend[:, None, :] <= chunk[None, :, None]
    exp_of = jnp.sum(before, axis=-1, dtype=jnp.int32)
    first_chunk = jnp.sum(jnp.where(before, nch[:, None, :], 0), axis=-1)
    first_src = jnp.sum(jnp.where(before, cnt[:, None, :], 0), axis=-1)
    own = jnp.arange(N_EXPERTS, dtype=jnp.int32) == exp_of[..., None]
    n_rows = jnp.sum(jnp.where(own, cnt[:, None, :], 0), axis=-1)
    row0 = (chunk[None, :] - first_chunk) * MOE_ROWS
    lane = jnp.arange(MOE_ROWS, dtype=jnp.int32)
    valid =((chunk[None, :, None] >= 0) & (row0[..., None] + lane < n_rows[..., None])).reshape(nblk, n_list)
    src = lax.optimization_barrier(
        jnp.clip((first_src + row0)[..., None] + lane, 0, per - 1).reshape(nblk, n_list))
    picked = lax.optimization_barrier(jnp.take_along_axis(order, src, axis=-1))
    tile_off = (picked // TOP_K) * SUBLANES
    gt = jnp.where(valid, tile_off, 0)
    st = jnp.where(valid, tile_off, tb * SUBLANES)
    rw_pad = jnp.where(valid, jnp.take_along_axis(w_flat, picked, axis=-1), 0.0)
    wg, wu, wd = w_gate.astype(BF16), w_up.astype(BF16), w_down.astype(BF16)
    sg, su, sd = ws_gate.astype(BF16), ws_up.astype(BF16), ws_down.astype(BF16)
    acc8 = _moe_experts(x8, nch.reshape(-1), cstart.reshape(-1), gt.reshape(-1), st.reshape(-1),
                        rw_pad.reshape(-1), wg, wu, wd, sg, su, sd, tb)
    return _moe_finish(xf, acc8, ln_g, ln_b, tb)


def kernel(x, fox_w_in, fox_b_f, fox_w_out, gla_w_in, gla_w_gate_up, gla_b_gate, gla_norm, gla_w_out, ssd_w_in, ssd_conv_w, ssd_conv_b, ssd_dt_bias, ssd_a_log, ssd_d, ssd_norm, ssd_w_out, ln1_g, ln1_b, moe_router, moe_router_bias, moe_w_gate, moe_w_up, moe_w_down, moe_ws_gate, moe_ws_up, moe_ws_down, ln2_g, ln2_b):
    batch, seq, d = x.shape
    xf = x.reshape(batch * seq, d)
    xb = xf.astype(BF16)
    tb = seq
    for i in range(DEPTH):
        kind, j = i % N_MIXERS, i // N_MIXERS
        if kind == 0:
            xf, xb, x8 = _fox_layer(xf, xb, fox_w_in[j], fox_b_f[j], fox_w_out[j], ln1_g[i], ln1_b[i],
                                    batch, seq)
        elif kind == 1:
            xf, xb, x8 = _gla_layer(xf, xb, gla_w_in[j], gla_w_gate_up[j], gla_b_gate[j], gla_norm[j],
                                    gla_w_out[j], ln1_g[i], ln1_b[i], batch, seq)
        else:
            xf, xb, x8 = _ssd_layer(xf, xb, ssd_w_in[j], ssd_conv_w[j], ssd_conv_b[j], ssd_dt_bias[j],
                                    ssd_a_log[j], ssd_d[j], ssd_norm[j], ssd_w_out[j], ln1_g[i],
                                    ln1_b[i], batch, seq)
        xf, xb = _moe_layer(xf, x8, moe_router[i], moe_router_bias[i], moe_w_gate[i], moe_w_up[i],
                            moe_w_down[i], moe_ws_gate[i], moe_ws_up[i], moe_ws_down[i],
                            ln2_g[i], ln2_b[i], tb)
    return xf.reshape(batch, seq, d)
```

```python
import functools

import jax
import jax.numpy as jnp
from jax import lax
from jax.experimental import pallas as pl
from jax.experimental.pallas import tpu as pltpu

F32 = jnp.float32
BF16 = jnp.bfloat16
HI = lax.Precision.HIGHEST

D_MODEL = 1024
DEPTH = 4
N_MIXERS = 3
DN_ALPHA = (2 * DEPTH) ** 0.25
LN_EPS = 1e-5
RMS_EPS = 1e-6

FOX_HEADS = 16
FOX_HEAD_DIM = 64
GLA_HEADS = 4
GLA_DK = 512
GLA_DV = 1024
GLA_HK = 128
GLA_HV = 256
GLA_GATE_RANK = 16
GLA_GATE_TAU = 16.0
GLA_CHUNK = 64
SSD_D_INNER = 2048
SSD_HEADS = 32
SSD_GROUPS = 8
SSD_HG = 4
SSD_P = 64
SSD_STATE = 128
SSD_CONV = 4
SSD_CHUNK = 128
N_EXPERTS = 64
TOP_K = 8
N_EXPERT_GROUPS = 8
TOPK_GROUPS = 4
EXPERT_FF = 256
ROUTED_SCALE = 2.5

LANES = 128
SUBLANES = 8
VMEM_LIMIT = 56 * 2 ** 20
MOE_ROWS = 128
MOE_LIST_ALIGN = 1024
MOE_SCATTER_GROUP = 8
MOE_EXPERTS_PER_STEP = 4
MOE_LEAD_CHUNKS = 2
SSD_GROUPS_PER_STEP = 2


def _params(sem):
    return pltpu.CompilerParams(dimension_semantics=sem, vmem_limit_bytes=VMEM_LIMIT)


def _sigmoid(x):
    return 1.0 / (1.0 + jnp.exp(-x))


def _silu(x):
    return x * _sigmoid(x)


def _log_sigmoid(x):
    return jnp.minimum(x, 0.0) - jnp.log(1.0 + jnp.exp(-jnp.abs(x)))


def _softplus(x):
    return jnp.maximum(x, 0.0) + jnp.log(1.0 + jnp.exp(-jnp.abs(x)))


def _dot(a, b, precision=None):
    return jnp.dot(a, b, preferred_element_type=F32, precision=precision)


def _dot_nt(a, b, precision=None):
    return lax.dot_general(a, b, (((1,), (1,)), ((), ())), preferred_element_type=F32,
                           precision=precision)


def _dot_tn(a, b, precision=None):
    return lax.dot_general(a, b, (((0,), (0,)), ((), ())), preferred_element_type=F32,
                           precision=precision)


def _tri(n, lower):
    r = lax.broadcasted_iota(jnp.int32, (n, n), 0)
    c = lax.broadcasted_iota(jnp.int32, (n, n), 1)
    return (r >= c) if lower else (r <= c)


def _mm_body(x_ref, w_ref, o_ref, *, precision):
    o_ref[...] = _dot(x_ref[...], w_ref[...], precision).astype(o_ref.dtype)


def _matmul(x, w, out_dtype, name, tm=1024, tn=1024, precision=None):
    m, k = x.shape
    n = w.shape[1]
    tm, tn = min(tm, m), min(tn, n)
    return pl.pallas_call(
        functools.partial(_mm_body, precision=precision),
        out_shape=jax.ShapeDtypeStruct((m, n), out_dtype),
        grid=(n // tn, m // tm),
        in_specs=[pl.BlockSpec((tm, k), lambda j, i: (i, 0)),
                  pl.BlockSpec((k, tn), lambda j, i: (0, j))],
        out_specs=pl.BlockSpec((tm, tn), lambda j, i: (i, j)),
        compiler_params=_params(("parallel", "parallel")),
        name=name,
    )(x, w)


def _layer_norm_rows(z, g, b):
    mu = jnp.mean(z, axis=-1, keepdims=True)
    zc = z - mu
    var = jnp.mean(zc * zc, axis=-1, keepdims=True)
    return zc * lax.rsqrt(var + LN_EPS) * g + b


def _tok_rows(ref, t0, n):
    return jnp.concatenate(
        [ref[pl.ds(t0 * SUBLANES + k, n, stride=SUBLANES), :] for k in range(SUBLANES)], axis=1)


def _tok_store(ref, t0, val):
    for k in range(SUBLANES):
        ref[pl.ds(t0 * SUBLANES + k, val.shape[0], stride=SUBLANES), :] = val[:, k * LANES:(k + 1) * LANES]


def _proj_ln_body(o_ref, w_ref, x_ref, g_ref, b_ref, xf_ref, xb_ref, x8_ref):
    y = _dot(o_ref[...], w_ref[...])
    out = _layer_norm_rows(DN_ALPHA * x_ref[...] + y, g_ref[...], b_ref[...])
    xf_ref[...] = out
    xb_ref[...] = out.astype(BF16)
    _tok_store(x8_ref, 0, out)


def _proj_ln(o, w, x, g, b, name, tm=512):
    m, k = o.shape
    d = w.shape[1]
    assert d == SUBLANES * LANES
    tm = min(tm, m)
    return pl.pallas_call(
        _proj_ln_body,
        out_shape=(jax.ShapeDtypeStruct((m, d), F32), jax.ShapeDtypeStruct((m, d), BF16),
                   jax.ShapeDtypeStruct((m * SUBLANES, LANES), F32)),
        grid=(m // tm,),
        in_specs=[pl.BlockSpec((tm, k), lambda i: (i, 0)),
                  pl.BlockSpec((k, d), lambda i: (0, 0)),
                  pl.BlockSpec((tm, d), lambda i: (i, 0)),
                  pl.BlockSpec((1, d), lambda i: (0, 0)),
                  pl.BlockSpec((1, d), lambda i: (0, 0))],
        out_specs=(pl.BlockSpec((tm, d), lambda i: (i, 0)),
                   pl.BlockSpec((tm, d), lambda i: (i, 0)),
                   pl.BlockSpec((tm * SUBLANES, LANES), lambda i: (i, 0))),
        compiler_params=_params(("parallel",)),
        name=name,
    )(o, w, x, g.reshape(1, d), b.reshape(1, d))


def _fox_gate_body(x_ref, w_ref, b_ref, c_ref, *, seq):
    f = _dot(x_ref[...], w_ref[...], HI)
    z = f.T[:FOX_HEADS, :] + b_ref[...]
    lf = _log_sigmoid(z)
    upper = _tri(LANES, lower=False).astype(F32)
    carry = jnp.zeros((FOX_HEADS, 1), F32)
    for blk in range(seq // LANES):
        seg = _dot(lf[:, blk * LANES:(blk + 1) * LANES], upper, HI) + carry
        c_ref[0, :, blk * LANES:(blk + 1) * LANES] = seg
        carry = seg[:, LANES - 1:LANES]


def _fox_gate(x, w_f, b_f, batch, seq):
    d = x.shape[1]
    return pl.pallas_call(
        functools.partial(_fox_gate_body, seq=seq),
        out_shape=jax.ShapeDtypeStruct((batch, FOX_HEADS, seq), F32),
        grid=(batch,),
        in_specs=[pl.BlockSpec((seq, d), lambda b: (b, 0)),
                  pl.BlockSpec((d, LANES), lambda b: (0, 0)),
                  pl.BlockSpec((FOX_HEADS, 1), lambda b: (0, 0))],
        out_specs=pl.BlockSpec((1, FOX_HEADS, seq), lambda b: (b, 0, 0)),
        compiler_params=_params(("parallel",)),
        name="fox_gate",
    )(x, w_f, b_f)


def _fox_attn_body(q_ref, k_ref, v_ref, c_ref, o_ref, kh_ref, qh_ref, vt_ref, acc_ref, ml_ref, ot_ref,
                   *, seq, t):
    nq = seq // t
    hd = FOX_HEAD_DIM
    lane = lax.broadcasted_iota(jnp.int32, (1, LANES), 1)
    valid = _tri(t, lower=False)
    scale = hd ** -0.5
    for h in range(2):
        own = (lane >= h * hd) & (lane < (h + 1) * hd)
        aug = (1 - h) * hd
        c_col = c_ref[0, 0, :, h:h + 1]
        hi = c_col.astype(BF16).astype(F32)
        r1 = c_col - hi
        mid = r1.astype(BF16).astype(F32)
        lo = (r1 - mid).astype(BF16).astype(F32)
        sel = [(lane == aug + n).astype(F32) for n in range(3)]
        c_lanes = hi * sel[0] + mid * sel[1] + lo * sel[2]
        own_f = own.astype(F32)
        kh_ref[h] = (k_ref[...].astype(F32) * (scale * own_f) + c_lanes).astype(BF16)
        qh_ref[h] = (q_ref[...].astype(F32) * own_f - (sel[0] + sel[1] + sel[2])).astype(BF16)
    vt_ref[...] = v_ref[...].astype(F32).T.astype(BF16)

    def scores(tile):
        j, i, h = tile
        s = _dot_nt(kh_ref[h, j * t:(j + 1) * t, :], qh_ref[h, i * t:(i + 1) * t, :])
        return jnp.where(valid, s, -jnp.inf) if i == j else s

    def update(tile, s):
        j, i, h = tile
        rows = slice(h * hd, (h + 1) * hd)
        m_new = jnp.max(s, axis=0, keepdims=True)
        if j > 0:
            m_old = ml_ref[i, h:h + 1, :]
            m_new = jnp.maximum(m_old, m_new)
        p = jnp.exp(s - m_new)
        l_new = jnp.sum(p, axis=0, keepdims=True)
        acc = _dot(vt_ref[rows, j * t:(j + 1) * t], p.astype(BF16))
        if j > 0:
            a = jnp.exp(m_old - m_new)
            l_new = a * ml_ref[i, 2 + h:3 + h, :] + l_new
            acc = a * acc_ref[i, rows, :] + acc
        if i == j:
            ot_ref[rows, i * t:(i + 1) * t] = acc / l_new
        else:
            ml_ref[i, h:h + 1, :] = m_new
            ml_ref[i, 2 + h:3 + h, :] = l_new
            acc_ref[i, rows, :] = acc

    tiles = [(j, i, h) for j in range(nq) for i in range(j, nq) for h in range(2)]
    ahead = 2
    pending = []
    for idx in range(len(tiles) + ahead):
        if idx < len(tiles):
            pending.append(scores(tiles[idx]))
        if idx >= ahead:
            update(tiles[idx - ahead], pending.pop(0))
    o_ref[...] = ot_ref[...].T.astype(o_ref.dtype)


def _fox_attn(qkv, c, batch, seq, t=256):
    t = min(t, seq)
    pairs = FOX_HEADS // 2
    c_cols = c.reshape(batch, pairs, 2, seq).transpose(0, 1, 3, 2)
    return pl.pallas_call(
        functools.partial(_fox_attn_body, seq=seq, t=t),
        out_shape=jax.ShapeDtypeStruct((batch * seq, D_MODEL), BF16),
        grid=(batch, pairs),
        in_specs=[pl.BlockSpec((seq, LANES), lambda b, p: (b, p)),
                  pl.BlockSpec((seq, LANES), lambda b, p: (b, pairs + p)),
                  pl.BlockSpec((seq, LANES), lambda b, p: (b, 2 * pairs + p)),
                  pl.BlockSpec((1, 1, seq, 2), lambda b, p: (b, p, 0, 0))],
        out_specs=pl.BlockSpec((seq, LANES), lambda b, p: (b, p)),
        scratch_shapes=[pltpu.VMEM((2, seq, LANES), BF16),
                        pltpu.VMEM((2, seq, LANES), BF16),
                        pltpu.VMEM((LANES, seq), BF16),
                        pltpu.VMEM((seq // t, LANES, t), F32),
                        pltpu.VMEM((seq // t, SUBLANES, t), F32),
                        pltpu.VMEM((LANES, seq), F32)],
        compiler_params=_params(("parallel", "parallel")),
        name="fox_attn",
    )(qkv, qkv, qkv, c_cols)


def _fox_layer(xf, xb, w_in, b_f, w_out, ln_g, ln_b, batch, seq):
    d = D_MODEL
    qkv = _matmul(xb, w_in[:, :3 * d].astype(BF16), BF16, "fox_qkv")
    w_f = jnp.pad(w_in[:, 3 * d:], ((0, 0), (0, LANES - FOX_HEADS)))
    c = _fox_gate(xf, w_f, b_f.reshape(FOX_HEADS, 1), batch, seq)
    o = _fox_attn(qkv, c, batch, seq)
    return _proj_ln(o, w_out.astype(BF16), xf, ln_g, ln_b, "fox_out_ln")


def _gla_body(q_ref, k_ref, v_ref, r_ref, gl_ref, wgu_ref, bg_ref, ng_ref, o_ref, bc_ref, st_ref, *, seq):
    ch = GLA_CHUNK
    tile = min(256, seq)
    r_i = lax.broadcasted_iota(jnp.int32, (tile, tile), 0)
    c_i = lax.broadcasted_iota(jnp.int32, (tile, tile), 1)
    same_chunk_lower = ((r_i >= c_i) & (r_i // ch == c_i // ch)).astype(F32)
    for t0 in range(0, seq, tile):
        gate = _dot(gl_ref[t0:t0 + tile, :], wgu_ref[...], HI) + bg_ref[...]
        log_a = _log_sigmoid(gate) * (1.0 / GLA_GATE_TAU)
        bc_ref[t0:t0 + tile, :] = _dot(same_chunk_lower, log_a, HI)
    lower = _tri(ch, lower=True)
    st_ref[...] = jnp.zeros_like(st_ref)
    heads = range(GLA_HEADS)

    def chunk(i, carry):
        rows = pl.ds(pl.multiple_of(i * ch, ch), ch)
        bcum = bc_ref[rows, :]
        b_last = bcum[ch - 1:ch, :]
        q = q_ref[rows, :].astype(F32) * (GLA_HK ** -0.5)
        k = k_ref[rows, :].astype(F32)
        q_dec = (q * jnp.exp(bcum)).astype(BF16)
        k_inv = (k * jnp.exp(-bcum)).astype(BF16)
        k_end = (k * jnp.exp(b_last - bcum)).astype(BF16)
        decay = jnp.exp(b_last)
        kl = [slice(h * GLA_HK, (h + 1) * GLA_HK) for h in heads]
        vl = [slice(h * GLA_HV, (h + 1) * GLA_HV) for h in heads]
        v = [v_ref[rows, vl[h]] for h in heads]
        att = [jnp.where(lower, _dot_nt(q_dec[:, kl[h]], k_inv[:, kl[h]]), 0.0).astype(BF16)
               for h in heads]
        state_t = [st_ref[h] for h in heads]
        o_inter = [_dot_nt(q_dec[:, kl[h]], state_t[h].astype(BF16)) for h in heads]
        o_intra = [_dot(att[h], v[h]) for h in heads]
        kv_t = [_dot_tn(v[h], k_end[:, kl[h]]) for h in heads]
        for h in heads:
            st_ref[h] = state_t[h] * decay[:, kl[h]] + kv_t[h]
            o = o_intra[h] + o_inter[h]
            ms = jnp.mean(o * o, axis=-1, keepdims=True)
            on = o * lax.rsqrt(ms + RMS_EPS) * ng_ref[...]
            o_ref[rows, vl[h]] = (on * _silu(r_ref[rows, vl[h]].astype(F32))).astype(o_ref.dtype)
        return carry

    lax.fori_loop(0, seq // ch, chunk, 0)


def _gla_mix(proj, g_low, w_gu, b_gate, norm_g, batch, seq):
    return pl.pallas_call(
        functools.partial(_gla_body, seq=seq),
        out_shape=jax.ShapeDtypeStruct((batch * seq, GLA_DV), BF16),
        grid=(batch,),
        in_specs=[pl.BlockSpec((seq, GLA_DK), lambda b: (b, 0)),
                  pl.BlockSpec((seq, GLA_DK), lambda b: (b, 1)),
                  pl.BlockSpec((seq, GLA_DV), lambda b: (b, 1)),
                  pl.BlockSpec((seq, GLA_DV), lambda b: (b, 2)),
                  pl.BlockSpec((seq, LANES), lambda b: (b, 0)),
                  pl.BlockSpec((LANES, GLA_DK), lambda b: (0, 0)),
                  pl.BlockSpec((1, GLA_DK), lambda b: (0, 0)),
                  pl.BlockSpec((1, GLA_HV), lambda b: (0, 0))],
        out_specs=pl.BlockSpec((seq, GLA_DV), lambda b: (b, 0)),
        scratch_shapes=[pltpu.VMEM((seq, GLA_DK), F32),
                        pltpu.VMEM((GLA_HEADS, GLA_HV, GLA_HK), F32)],
        compiler_params=_params(("parallel",)),
        name="gla_mix",
    )(proj, proj, proj, proj, g_low, w_gu, b_gate.reshape(1, GLA_DK), norm_g.reshape(1, GLA_HV))


def _gla_layer(xf, xb, w_in, w_gate_up, b_gate, norm_g, w_out, ln_g, ln_b, batch, seq):
    n_main = 2 * GLA_DK + 2 * GLA_DV
    proj = _matmul(xb, w_in[:, :n_main].astype(BF16), BF16, "gla_proj")
    w_g = jnp.pad(w_in[:, n_main:], ((0, 0), (0, LANES - GLA_GATE_RANK)))
    g_low = _matmul(xf, w_g, F32, "gla_gate", tn=LANES, precision=HI)
    w_gu = jnp.pad(w_gate_up, ((0, LANES - GLA_GATE_RANK), (0, 0)))
    y = _gla_mix(proj, g_low, w_gu, b_gate, norm_g, batch, seq)
    return _proj_ln(y, w_out.astype(BF16), xf, ln_g, ln_b, "gla_out_ln")


def _ssd_gate_body(x_ref, w_ref, bias_ref, alog_ref, dt_ref, ac_ref, dtt_ref, act_ref, *, seq):
    ch = SSD_CHUNK
    raw = _dot(x_ref[...], w_ref[...], HI)
    dt = _softplus(raw + bias_ref[...])
    a = dt * (-jnp.exp(alog_ref[...]))
    lower_f = _tri(ch, lower=True).astype(F32)
    dt_ref[...] = dt
    dtt_ref[0] = dt.T
    for c in range(seq // ch):
        acum = _dot(lower_f, a[c * ch:(c + 1) * ch, :], HI)
        ac_ref[c * ch:(c + 1) * ch, :] = acum
        act_ref[0, :, c * ch:(c + 1) * ch] = acum.T


def _ssd_gate(x, w_dt, dt_bias, a_log, batch, seq):
    d = x.shape[1]
    col = jax.ShapeDtypeStruct((batch * seq, LANES), F32)
    row = jax.ShapeDtypeStruct((batch, LANES, seq), F32)
    return pl.pallas_call(
        functools.partial(_ssd_gate_body, seq=seq),
        out_shape=(col, col, row, row),
        grid=(batch,),
        in_specs=[pl.BlockSpec((seq, d), lambda b: (b, 0)),
                  pl.BlockSpec((d, LANES), lambda b: (0, 0)),
                  pl.BlockSpec((1, LANES), lambda b: (0, 0)),
                  pl.BlockSpec((1, LANES), lambda b: (0, 0))],
        out_specs=(pl.BlockSpec((seq, LANES), lambda b: (b, 0)),
                   pl.BlockSpec((seq, LANES), lambda b: (b, 0)),
                   pl.BlockSpec((1, LANES, seq), lambda b: (b, 0, 0)),
                   pl.BlockSpec((1, LANES, seq), lambda b: (b, 0, 0))),
        compiler_params=_params(("parallel",)),
        name="ssd_gate",
    )(x, w_dt, dt_bias, a_log)


def _expand_heads(cols, width):
    rows = cols.shape[0]
    lane = lax.broadcasted_iota(jnp.int32, (rows, SSD_HG * width), 1)
    out = jnp.broadcast_to(cols[:, SSD_HG - 1:SSD_HG], (rows, SSD_HG * width))
    for h in range(SSD_HG - 2, -1, -1):
        out = jnp.where(lane < (h + 1) * width, jnp.broadcast_to(cols[:, h:h + 1], out.shape), out)
    return out


def _ssd_body(z_ref, xr_ref, br_ref, cr_ref, cwx_ref, cwb_ref, cwc_ref, cbx_ref, cbb_ref, cbc_ref,
              dt4_ref, ac4_ref, dtt_ref, act_ref, dskip_ref, ng_ref, o_ref,
              padx_ref, padb_ref, xs_ref, bm_ref, cm_ref, st_ref, *, seq):
    ch = SSD_CHUNK
    hp = SSD_HG * SSD_P
    n = SSD_STATE
    pad = SUBLANES

    def conv(raw_ref, pad_ref, w_ref, b_ref, dst_ref):
        pad_ref[0:pad, :] = jnp.zeros((pad, pad_ref.shape[1]), F32)
        pad_ref[pad:pad + seq, :] = raw_ref[...].astype(F32)
        tile = min(256, seq)
        width = min(256, pad_ref.shape[1])
        for c0 in range(0, pad_ref.shape[1], width):
            cols = slice(c0, c0 + width)
            for t0 in range(0, seq, tile):
                acc = jnp.broadcast_to(b_ref[:, cols], (tile, width))
                for j in range(SSD_CONV):
                    off = t0 + pad - (SSD_CONV - 1) + j
                    acc = acc + w_ref[j:j + 1, cols] * pad_ref[off:off + tile, cols]
                dst_ref[t0:t0 + tile, cols] = _silu(acc)

    conv(xr_ref, padx_ref, cwx_ref, cbx_ref, xs_ref)
    conv(br_ref, padb_ref, cwb_ref, cbb_ref, bm_ref)
    conv(cr_ref, padb_ref, cwc_ref, cbc_ref, cm_ref)

    lower = _tri(ch, lower=True)
    lane = lax.broadcasted_iota(jnp.int32, (1, hp), 1)
    st_ref[...] = jnp.zeros_like(st_ref)

    def group_chunk(gi, c, rows):
        cols = slice(gi * hp, (gi + 1) * hp)
        ncols = slice(gi * n, (gi + 1) * n)
        xs = xs_ref[rows, cols]
        bm = bm_ref[rows, ncols].astype(BF16)
        cm = cm_ref[rows, ncols].astype(BF16)
        dt4 = dt4_ref[0, gi, rows, :]
        ac4 = ac4_ref[0, gi, rows, :]
        cb = _dot_nt(cm, bm)
        yield
        state = st_ref[gi]
        y_off = _dot(cm, state.astype(BF16))
        yield
        to_end = jnp.exp(ac4[ch - 1:ch, :] - ac4)
        w_state = _expand_heads(dt4 * to_end, SSD_P)
        e_acum = _expand_heads(jnp.exp(ac4), SSD_P)
        new_state = _dot_tn(bm, (xs * w_state).astype(BF16))
        yield
        st_ref[gi] = state * e_acum[ch - 1:ch, :] + new_state
        xs_b = xs.astype(BF16)
        y = xs * dskip_ref[:, cols] + y_off * e_acum
        for h in range(SSD_HG):
            a_col = jnp.broadcast_to(ac4[:, h:h + 1], (ch, ch))
            a_row = act_ref[0, gi, h, pl.ds(c, 1), :]
            d_row = dtt_ref[0, gi, h, pl.ds(c, 1), :]
            decay = jnp.exp(jnp.where(lower, a_col - a_row, -jnp.inf))
            m_h = (cb * decay * d_row).astype(BF16)
            head = (lane >= h * SSD_P) & (lane < (h + 1) * SSD_P)
            y = y + _dot(m_h, jnp.where(head, xs_b, jnp.zeros_like(xs_b)))
            yield
        y = y * _silu(z_ref[rows, cols].astype(F32))
        ms = jnp.mean(y * y, axis=-1, keepdims=True)
        o_ref[rows, cols] = (y * lax.rsqrt(ms + RMS_EPS) * ng_ref[:, cols]).astype(o_ref.dtype)

    def chunk(c, carry):
        rows = pl.ds(pl.multiple_of(c * ch, ch), ch)
        running = [group_chunk(gi, c, rows) for gi in range(SSD_GROUPS_PER_STEP)]
        while running:
            for gen in list(running):
                if next(gen, "done") == "done":
                    running.remove(gen)
        return carry

    lax.fori_loop(0, seq // ch, chunk, 0)


def _ssd_mix(proj, conv_w, conv_b, dt4, ac4, dtt, act, d_exp, norm_g, batch, seq):
    per = SSD_GROUPS_PER_STEP
    steps = SSD_GROUPS // per
    hp = per * SSD_HG * SSD_P
    n = per * SSD_STATE
    nz = SSD_D_INNER // hp
    nb = 2 * SSD_D_INNER // n
    nc = nb + steps
    cb0 = SSD_D_INNER // n
    nch = seq // SSD_CHUNK
    return pl.pallas_call(
        functools.partial(_ssd_body, seq=seq),
        out_shape=jax.ShapeDtypeStruct((batch * seq, SSD_D_INNER), BF16),
        grid=(batch, steps),
        in_specs=[pl.BlockSpec((seq, hp), lambda b, i: (b, i)),
                  pl.BlockSpec((seq, hp), lambda b, i: (b, nz + i)),
                  pl.BlockSpec((seq, n), lambda b, i: (b, nb + i)),
                  pl.BlockSpec((seq, n), lambda b, i: (b, nc + i)),
                  pl.BlockSpec((SSD_CONV, hp), lambda b, i: (0, i)),
                  pl.BlockSpec((SSD_CONV, n), lambda b, i: (0, cb0 + i)),
                  pl.BlockSpec((SSD_CONV, n), lambda b, i: (0, cb0 + steps + i)),
                  pl.BlockSpec((1, hp), lambda b, i: (0, i)),
                  pl.BlockSpec((1, n), lambda b, i: (0, cb0 + i)),
                  pl.BlockSpec((1, n), lambda b, i: (0, cb0 + steps + i)),
                  pl.BlockSpec((1, per, seq, SSD_HG), lambda b, i: (b, i, 0, 0)),
                  pl.BlockSpec((1, per, seq, SSD_HG), lambda b, i: (b, i, 0, 0)),
                  pl.BlockSpec((1, per, SSD_HG, nch, SSD_CHUNK), lambda b, i: (b, i, 0, 0, 0)),
                  pl.BlockSpec((1, per, SSD_HG, nch, SSD_CHUNK), lambda b, i: (b, i, 0, 0, 0)),
                  pl.BlockSpec((1, hp), lambda b, i: (0, i)),
                  pl.BlockSpec((1, hp), lambda b, i: (0, i))],
        out_specs=pl.BlockSpec((seq, hp), lambda b, i: (b, i)),
        scratch_shapes=[pltpu.VMEM((seq + SUBLANES, hp), F32),
                        pltpu.VMEM((seq + SUBLANES, n), F32),
                        pltpu.VMEM((seq, hp), F32),
                        pltpu.VMEM((seq, n), F32),
                        pltpu.VMEM((seq, n), F32),
                        pltpu.VMEM((per, SSD_STATE, SSD_HG * SSD_P), F32)],
        compiler_params=_params(("parallel", "parallel")),
        name="ssd_mix",
    )(proj, proj, proj, proj, conv_w, conv_w, conv_w, conv_b, conv_b, conv_b,
      dt4, ac4, dtt, act, d_exp, norm_g)


def _ssd_layer(xf, xb, w_in, conv_w, conv_b, dt_bias, a_log, d_skip, norm_g, w_out, ln_g, ln_b,
               batch, seq):
    n_main = SSD_D_INNER + (SSD_D_INNER + 2 * SSD_GROUPS * SSD_STATE)
    proj = _matmul(xb, w_in[:, :n_main].astype(BF16), BF16, "ssd_proj")
    lane_pad = ((0, 0), (0, LANES - SSD_HEADS))
    w_dt = jnp.pad(w_in[:, n_main:], lane_pad)
    dt_c, ac_c, dt_r, ac_r = _ssd_gate(xf, w_dt, jnp.pad(dt_bias.reshape(1, -1), lane_pad),
                                       jnp.pad(a_log.reshape(1, -1), lane_pad), batch, seq)

    def cols(t):
        return t[:, :SSD_HEADS].reshape(batch, seq, SSD_GROUPS, SSD_HG).transpose(0, 2, 1, 3)

    def rows(t):
        return t[:, :SSD_HEADS].reshape(batch, SSD_GROUPS, SSD_HG, seq // SSD_CHUNK, SSD_CHUNK)

    d_exp = jnp.repeat(d_skip, SSD_P).reshape(1, SSD_D_INNER)
    y = _ssd_mix(proj, conv_w, conv_b.reshape(1, -1), cols(dt_c), cols(ac_c), rows(dt_r), rows(ac_r),
                 d_exp, norm_g.reshape(1, SSD_D_INNER), batch, seq)
    return _proj_ln(y, w_out.astype(BF16), xf, ln_g, ln_b, "ssd_out_ln")


def _router_body(x_ref, w_ref, bias_ref, e_ref, w_out_ref):
    tm = x_ref.shape[0]
    gsz = N_EXPERTS // N_EXPERT_GROUPS
    logits = _dot(x_ref[...], w_ref[...], HI).T[:N_EXPERTS, :]
    scores = _sigmoid(logits)
    sel = scores + bias_ref[...]
    sub = lax.broadcasted_iota(jnp.int32, (gsz, tm), 0).astype(F32)
    neg = jnp.float32(-jnp.inf)

    def top1(v, idx, sentinel):
        m = jnp.max(v, axis=0, keepdims=True)
        i = jnp.min(jnp.where(v == m, idx, sentinel), axis=0, keepdims=True)
        return m, i

    s_g = [sel[g * gsz:(g + 1) * gsz, :] for g in range(N_EXPERT_GROUPS)]
    sc_g = [scores[g * gsz:(g + 1) * gsz, :] for g in range(N_EXPERT_GROUPS)]
    grp_rows = []
    for g in range(N_EXPERT_GROUPS):
        m1, i1 = top1(s_g[g], sub, float(gsz))
        m2 = jnp.max(jnp.where(sub == i1, neg, s_g[g]), axis=0, keepdims=True)
        grp_rows.append(m1 + m2)
    grp = jnp.concatenate(grp_rows, axis=0)
    keep = jnp.zeros((N_EXPERT_GROUPS, tm), F32)
    for _ in range(TOPK_GROUPS):
        _, ig = top1(grp, sub, float(N_EXPERT_GROUPS))
        hit = sub == ig
        keep = jnp.where(hit, 1.0, keep)
        grp = jnp.where(hit, neg, grp)
    cand = [jnp.where(keep[g:g + 1, :] > 0.5, s_g[g], neg) for g in range(N_EXPERT_GROUPS)]
    ids = [sub + float(g * gsz) for g in range(N_EXPERT_GROUPS)]
    e_rows, w_rows = [], []
    for _ in range(TOP_K):
        m = cand[0]
        for g in range(1, N_EXPERT_GROUPS):
            m = jnp.maximum(m, cand[g])
        m = jnp.max(m, axis=0, keepdims=True)
        first = jnp.where(cand[0] == m, ids[0], float(N_EXPERTS))
        for g in range(1, N_EXPERT_GROUPS):
            first = jnp.minimum(first, jnp.where(cand[g] == m, ids[g], float(N_EXPERTS)))
        first = jnp.min(first, axis=0, keepdims=True)
        wsum = jnp.zeros((gsz, tm), F32)
        for g in range(N_EXPERT_GROUPS):
            hit = ids[g] == first
            wsum = wsum + jnp.where(hit, sc_g[g], 0.0)
            cand[g] = jnp.where(hit, neg, cand[g])
        e_rows.append(first)
        w_rows.append(jnp.sum(wsum, axis=0, keepdims=True))
    top_e = jnp.concatenate(e_rows, axis=0)
    top_w = jnp.concatenate(w_rows, axis=0)
    denom = jnp.sum(top_w, axis=0, keepdims=True)
    e_ref[...] = top_e.astype(jnp.int32)
    w_out_ref[...] = top_w / denom * ROUTED_SCALE


def _router(x, w_router, bias, tm=512):
    t, d = x.shape
    tm = min(tm, t)
    return pl.pallas_call(
        _router_body,
        out_shape=(jax.ShapeDtypeStruct((TOP_K, t), jnp.int32),
                   jax.ShapeDtypeStruct((TOP_K, t), F32)),
        grid=(t // tm,),
        in_specs=[pl.BlockSpec((tm, d), lambda i: (i, 0)),
                  pl.BlockSpec((d, LANES), lambda i: (0, 0)),
                  pl.BlockSpec((N_EXPERTS, 1), lambda i: (0, 0))],
        out_specs=(pl.BlockSpec((TOP_K, tm), lambda i: (0, i)),
                   pl.BlockSpec((TOP_K, tm), lambda i: (0, i))),
        compiler_params=_params(("parallel",)),
        name="moe_router",
    )(x, jnp.pad(w_router, ((0, 0), (0, LANES - N_EXPERTS))), bias.reshape(N_EXPERTS, 1))


def _moe_list_len(tb):
    chunks = MOE_LEAD_CHUNKS + -(-tb * TOP_K // MOE_ROWS) + N_EXPERTS + N_EXPERTS // MOE_EXPERTS_PER_STEP + 2
    return -(-chunks * MOE_ROWS // MOE_LIST_ALIGN) * MOE_LIST_ALIGN


def _moe_body(sstart_ref, scount_ref, cexp_ref, gt_hbm, st_hbm,
              rw_ref, x8_ref, wg_ref, wu_ref, wd_ref, sg_ref, su_ref, sd_ref,
              out_ref, gt_ref, st_ref, buf_ref, y2_ref, sem, *, tb):
    blk = pl.program_id(0)
    grp = pl.program_id(1)
    rows_per = MOE_ROWS
    tile = min(256, tb)
    n_list = st_ref.shape[0]
    n_chunks = n_list // rows_per
    acc_ref = out_ref.at[0]

    def tile_at(ref, off):
        return ref.at[pl.ds(pl.multiple_of(off, SUBLANES), SUBLANES), :]

    def gather(chunk, parity):
        src = gt_ref.at[pl.ds((chunk + MOE_LEAD_CHUNKS) * rows_per, rows_per)]
        dst = buf_ref.at[parity]
        for r in range(rows_per):
            tile_at(dst, r * SUBLANES)[...] = tile_at(x8_ref, src[r])[...]

    def scatter(chunk, parity):
        y_ref = y2_ref.at[parity]
        dst = st_ref.at[pl.ds((chunk + MOE_LEAD_CHUNKS) * rows_per, rows_per)]
        for r0 in range(0, rows_per, MOE_SCATTER_GROUP):
            offs = [dst[r0 + u] for u in range(MOE_SCATTER_GROUP)]
            vals = [tile_at(acc_ref, offs[u])[...] + tile_at(y_ref, (r0 + u) * SUBLANES)[...]
                    for u in range(MOE_SCATTER_GROUP)]
            for u in range(MOE_SCATTER_GROUP):
                tile_at(acc_ref, offs[u])[...] = vals[u]

    def row_weights(chunk):
        w_row = rw_ref[0, pl.ds(chunk + MOE_LEAD_CHUNKS, 1), :]
        return jnp.broadcast_to(w_row, (rows_per, rows_per)).T

    @pl.when(grp == 0)
    def _start_block():
        copies = [pltpu.make_async_copy(src.at[pl.ds(pl.multiple_of(blk * n_list, MOE_LIST_ALIGN), n_list)],
                                        dst, sem.at[n])
                  for n, (src, dst) in enumerate(((gt_hbm, gt_ref), (st_hbm, st_ref)))]
        for cp in copies:
            cp.start()
        for t0 in range(0, tb, tile):
            xb = _tok_rows(x8_ref, t0, tile).astype(BF16)
            h = _silu(_dot(xb, sg_ref[...])) * _dot(xb, su_ref[...])
            _tok_store(acc_ref, t0, _dot(h.astype(BF16), sd_ref[...]))
        tile_at(acc_ref, tb * SUBLANES)[...] = jnp.zeros((SUBLANES, LANES), F32)
        y2_ref[...] = jnp.zeros_like(y2_ref)
        for cp in copies:
            cp.wait()
        gather(0, 0)
        gather(1, 1)

    step_id = blk * pl.num_programs(1) + grp
    first = sstart_ref[step_id]
    pairs = scount_ref[step_id] // 2

    def step(i, carry):
        g = first + 2 * i
        xb = [_tok_rows(buf_ref.at[p], 0, rows_per).astype(BF16) for p in range(2)]
        e = [cexp_ref[blk * n_chunks + g + p] for p in range(2)]
        for p in range(2):
            scatter(g - 2 + p, p)
        for p in range(2):
            gather(g + 2 + p, p)
        gate = [_dot(xb[p], wg_ref[e[p]]) for p in range(2)]
        up = [_dot(xb[p], wu_ref[e[p]]) for p in range(2)]
        h = [(_silu(gate[p]) * up[p]).astype(BF16) for p in range(2)]
        y = [_dot(h[p], wd_ref[e[p]]) for p in range(2)]
        for p in range(2):
            w_rows = jnp.concatenate([row_weights(g + p)] * (y[p].shape[1] // rows_per), axis=1)
            _tok_store(y2_ref.at[p], 0, y[p] * w_rows)
        return carry

    lax.fori_loop(0, pairs, step, 0)

    @pl.when(grp == pl.num_programs(1) - 1)
    def _finish_block():
        last = first + 2 * pairs
        for p in range(2):
            scatter(last - 2 + p, p)


def _moe_experts(x8, sstart, scount, cexp, gt, st, rw, wg, wu, wd, sg, su, sd, tb):
    d = wg.shape[1]
    ff = wg.shape[2]
    nblk = x8.shape[0] // (tb * SUBLANES)
    n_list = _moe_list_len(tb)
    per_step = MOE_EXPERTS_PER_STEP
    grid_spec = pltpu.PrefetchScalarGridSpec(
        num_scalar_prefetch=3,
        grid=(nblk, N_EXPERTS // per_step),
        in_specs=[pl.BlockSpec(memory_space=pl.ANY),
                  pl.BlockSpec(memory_space=pl.ANY),
                  pl.BlockSpec((1, n_list // MOE_ROWS, MOE_ROWS), lambda i, e, *_: (i, 0, 0)),
                  pl.BlockSpec((tb * SUBLANES, LANES), lambda i, e, *_: (i, 0)),
                  pl.BlockSpec((per_step, d, ff), lambda i, e, *_: (e, 0, 0)),
                  pl.BlockSpec((per_step, d, ff), lambda i, e, *_: (e, 0, 0)),
                  pl.BlockSpec((per_step, ff, d), lambda i, e, *_: (e, 0, 0)),
                  pl.BlockSpec((d, ff), lambda i, e, *_: (0, 0)),
                  pl.BlockSpec((d, ff), lambda i, e, *_: (0, 0)),
                  pl.BlockSpec((ff, d), lambda i, e, *_: (0, 0))],
        out_specs=pl.BlockSpec((1, (tb + 1) * SUBLANES, LANES), lambda i, e, *_: (i, 0, 0)),
        scratch_shapes=[pltpu.SMEM((n_list,), jnp.int32),
                        pltpu.SMEM((n_list,), jnp.int32),
                        pltpu.VMEM((2, MOE_ROWS * SUBLANES, LANES), F32),
                        pltpu.VMEM((2, MOE_ROWS * SUBLANES, LANES), F32),
                        pltpu.SemaphoreType.DMA((2,))],
    )
    return pl.pallas_call(
        functools.partial(_moe_body, tb=tb),
        out_shape=jax.ShapeDtypeStruct((nblk, (tb + 1) * SUBLANES, LANES), F32),
        grid_spec=grid_spec,
        compiler_params=_params(("arbitrary", "arbitrary")),
        name="moe_experts",
    )(sstart, scount, cexp, gt, st, rw, x8, wg, wu, wd, sg, su, sd)


def _moe_finish_body(x_ref, acc_ref, g_ref, b_ref, xf_ref, xb_ref):
    z = DN_ALPHA * x_ref[...] + _tok_rows(acc_ref.at[0], 0, x_ref.shape[0])
    out = _layer_norm_rows(z, g_ref[...], b_ref[...])
    xf_ref[...] = out
    xb_ref[...] = out.astype(BF16)


def _moe_finish(x, acc8, g, b, tb, tm=512):
    m, d = x.shape
    tm = min(tm, tb)
    per_blk = tb // tm
    return pl.pallas_call(
        _moe_finish_body,
        out_shape=(jax.ShapeDtypeStruct((m, d), F32), jax.ShapeDtypeStruct((m, d), BF16)),
        grid=(m // tm,),
        in_specs=[pl.BlockSpec((tm, d), lambda i: (i, 0)),
                  pl.BlockSpec((1, tm * SUBLANES, LANES), lambda i: (i // per_blk, i % per_blk, 0)),
                  pl.BlockSpec((1, d), lambda i: (0, 0)),
                  pl.BlockSpec((1, d), lambda i: (0, 0))],
        out_specs=(pl.BlockSpec((tm, d), lambda i: (i, 0)),
                   pl.BlockSpec((tm, d), lambda i: (i, 0))),
        compiler_params=_params(("parallel",)),
        name="moe_finish",
    )(x, acc8, g.reshape(1, d), b.reshape(1, d))


def _moe_layer(xf, x8, w_router, router_bias, w_gate, w_up, w_down, ws_gate, ws_up, ws_down,
               ln_g, ln_b, tb):
    t, d = xf.shape
    top_e, top_w = _router(xf, w_router, router_bias)
    nblk = t // tb
    per = tb * TOP_K
    e_flat = top_e.T.reshape(nblk, per)
    w_flat = top_w.T.reshape(nblk, per)
    order = jnp.argsort(e_flat, axis=-1, stable=True).astype(jnp.int32)
    cnt = jnp.sum(e_flat[:, :, None] == jnp.arange(N_EXPERTS, dtype=jnp.int32), axis=1,
                  dtype=jnp.int32)
    nch = (cnt + MOE_ROWS - 1) // MOE_ROWS
    groups = N_EXPERTS // MOE_EXPERTS_PER_STEP
    gpad = jnp.sum(nch.reshape(nblk, groups, MOE_EXPERTS_PER_STEP), axis=-1) % 2
    pads_before = jnp.cumsum(gpad, axis=-1, dtype=jnp.int32) - gpad
    cend = (jnp.cumsum(nch, axis=-1, dtype=jnp.int32)
            + jnp.repeat(pads_before, MOE_EXPERTS_PER_STEP, axis=-1))
    n_list = _moe_list_len(tb)
    n_chunks = n_list // MOE_ROWS
    chunk = jnp.arange(n_chunks, dtype=jnp.int32) - MOE_LEAD_CHUNKS
    before = cend[:, None, :] <= chunk[None, :, None]
    exp_of = jnp.sum(before, axis=-1, dtype=jnp.int32)
    earlier = jnp.arange(groups, dtype=jnp.int32) < (exp_of // MOE_EXPERTS_PER_STEP)[..., None]
    first_chunk = (jnp.sum(jnp.where(before, nch[:, None, :], 0), axis=-1)
                   + jnp.sum(jnp.where(earlier, gpad[:, None, :], 0), axis=-1))
    first_src = jnp.sum(jnp.where(before, cnt[:, None, :], 0), axis=-1)
    own = jnp.arange(N_EXPERTS, dtype=jnp.int32) == exp_of[..., None]
    n_rows = jnp.sum(jnp.where(own, cnt[:, None, :], 0), axis=-1)
    row = ((chunk[None, :] - first_chunk) * MOE_ROWS)[..., None] + jnp.arange(MOE_ROWS, dtype=jnp.int32)
    valid = ((chunk[None, :, None] >= 0) & (row >= 0) & (row < n_rows[..., None])).reshape(nblk, n_list)
    src = lax.optimization_barrier(
        jnp.clip(first_src[..., None] + row, 0, per - 1).reshape(nblk, n_list))
    picked = lax.optimization_barrier(jnp.take_along_axis(order, src, axis=-1))
    tile_off = (picked // TOP_K) * SUBLANES
    gt = jnp.where(valid, tile_off, 0)
    st = jnp.where(valid, tile_off, tb * SUBLANES)
    rw_pad = jnp.where(valid, jnp.take_along_axis(w_flat, picked, axis=-1), 0.0)
    gend = cend.reshape(nblk, groups, MOE_EXPERTS_PER_STEP)[:, :, -1] + gpad
    gstart = jnp.concatenate([jnp.zeros((nblk, 1), jnp.int32), gend[:, :-1]], axis=1)
    cexp = jnp.minimum(exp_of, N_EXPERTS - 1) % MOE_EXPERTS_PER_STEP
    cexp = jnp.roll(cexp, -MOE_LEAD_CHUNKS, axis=1)
    wg, wu, wd = w_gate.astype(BF16), w_up.astype(BF16), w_down.astype(BF16)
    sg, su, sd = ws_gate.astype(BF16), ws_up.astype(BF16), ws_down.astype(BF16)
    acc8 = _moe_experts(x8, gstart.reshape(-1), (gend - gstart).reshape(-1), cexp.reshape(-1),
                        gt.reshape(-1), st.reshape(-1), rw_pad.reshape(nblk, n_chunks, MOE_ROWS),
                        wg, wu, wd, sg, su, sd, tb)
    return _moe_finish(xf, acc8, ln_g, ln_b, tb)


def kernel(x, fox_w_in, fox_b_f, fox_w_out, gla_w_in, gla_w_gate_up, gla_b_gate, gla_norm, gla_w_out, ssd_w_in, ssd_conv_w, ssd_conv_b, ssd_dt_bias, ssd_a_log, ssd_d, ssd_norm, ssd_w_out, ln1_g, ln1_b, moe_router, moe_router_bias, moe_w_gate, moe_w_up, moe_w_down, moe_ws_gate, moe_ws_up, moe_ws_down, ln2_g, ln2_b):
    batch, seq, d = x.shape
    xf = x.reshape(batch * seq, d)
    xb = xf.astype(BF16)
    tb = seq
    for i in range(DEPTH):
        kind, j = i % N_MIXERS, i // N_MIXERS
        if kind == 0:
            xf, xb, x8 = _fox_layer(xf, xb, fox_w_in[j], fox_b_f[j], fox_w_out[j], ln1_g[i], ln1_b[i],
                                    batch, seq)
        elif kind == 1:
            xf, xb, x8 = _gla_layer(xf, xb, gla_w_in[j], gla_w_gate_up[j], gla_b_gate[j], gla_norm[j],
                                    gla_w_out[j], ln1_g[i], ln1_b[i], batch, seq)
        else:
            xf, xb, x8 = _ssd_layer(xf, xb, ssd_w_in[j], ssd_conv_w[j], ssd_conv_b[j], ssd_dt_bias[j],
                                    ssd_a_log[j], ssd_d[j], ssd_norm[j], ssd_w_out[j], ln1_g[i],
                                    ln1_b[i], batch, seq)
        xf, xb = _moe_layer(xf, x8, moe_router[i], moe_router_bias[i], moe_w_gate[i], moe_w_up[i],
                            moe_w_down[i], moe_ws_gate[i], moe_ws_up[i], moe_ws_down[i],
                            ln2_g[i], ln2_b[i], tb)
    return xf.reshape(batch, seq, d)
```

```python
import functools

import jax
import jax.numpy as jnp
from jax import lax
from jax.experimental import pallas as pl
from jax.experimental.pallas import tpu as pltpu

F32 = jnp.float32
BF16 = jnp.bfloat16
HI = lax.Precision.HIGHEST

D_MODEL = 1024
DEPTH = 4
N_MIXERS = 3
DN_ALPHA = (2 * DEPTH) ** 0.25
LN_EPS = 1e-5
RMS_EPS = 1e-6

FOX_HEADS = 16
FOX_HEAD_DIM = 64
GLA_HEADS = 4
GLA_DK = 512
GLA_DV = 1024
GLA_HK = 128
GLA_HV = 256
GLA_GATE_RANK = 16
GLA_GATE_TAU = 16.0
GLA_CHUNK = 64
SSD_D_INNER = 2048
SSD_HEADS = 32
SSD_GROUPS = 8
SSD_HG = 4
SSD_P = 64
SSD_STATE = 128
SSD_CONV = 4
SSD_CHUNK = 128
N_EXPERTS = 64
TOP_K = 8
N_EXPERT_GROUPS = 8
TOPK_GROUPS = 4
EXPERT_FF = 256
ROUTED_SCALE = 2.5

LANES = 128
SUBLANES = 8
VMEM_LIMIT = 56 * 2 ** 20
MOE_ROWS = 128
MOE_LIST_ALIGN = 1024
MOE_SCATTER_GROUP = 8
MOE_EXPERTS_PER_STEP = 4
MOE_LEAD_CHUNKS = 2
SSD_GROUPS_PER_STEP = 2


def _params(sem):
    return pltpu.CompilerParams(dimension_semantics=sem, vmem_limit_bytes=VMEM_LIMIT)


def _sigmoid(x):
    return 1.0 / (1.0 + jnp.exp(-x))


def _silu(x):
    return x * _sigmoid(x)


def _log_sigmoid(x):
    return jnp.minimum(x, 0.0) - jnp.log(1.0 + jnp.exp(-jnp.abs(x)))


def _softplus(x):
    return jnp.maximum(x, 0.0) + jnp.log(1.0 + jnp.exp(-jnp.abs(x)))


def _dot(a, b, precision=None):
    return jnp.dot(a, b, preferred_element_type=F32, precision=precision)


def _dot_nt(a, b, precision=None):
    return lax.dot_general(a, b, (((1,), (1,)), ((), ())), preferred_element_type=F32,
                           precision=precision)


def _dot_tn(a, b, precision=None):
    return lax.dot_general(a, b, (((0,), (0,)), ((), ())), preferred_element_type=F32,
                           precision=precision)


def _tri(n, lower):
    r = lax.broadcasted_iota(jnp.int32, (n, n), 0)
    c = lax.broadcasted_iota(jnp.int32, (n, n), 1)
    return (r >= c) if lower else (r <= c)


def _mm_body(x_ref, w_ref, o_ref, *, precision):
    o_ref[...] = _dot(x_ref[...], w_ref[...], precision).astype(o_ref.dtype)


def _matmul(x, w, out_dtype, name, tm=1024, tn=1024, precision=None):
    m, k = x.shape
    n = w.shape[1]
    tm, tn = min(tm, m), min(tn, n)
    return pl.pallas_call(
        functools.partial(_mm_body, precision=precision),
        out_shape=jax.ShapeDtypeStruct((m, n), out_dtype),
        grid=(n // tn, m // tm),
        in_specs=[pl.BlockSpec((tm, k), lambda j, i: (i, 0)),
                  pl.BlockSpec((k, tn), lambda j, i: (0, j))],
        out_specs=pl.BlockSpec((tm, tn), lambda j, i: (i, j)),
        compiler_params=_params(("parallel", "parallel")),
        name=name,
    )(x, w)


def _layer_norm_rows(z, g, b):
    mu = jnp.mean(z, axis=-1, keepdims=True)
    zc = z - mu
    var = jnp.mean(zc * zc, axis=-1, keepdims=True)
    return zc * lax.rsqrt(var + LN_EPS) * g + b


def _tok_rows(ref, t0, n):
    return jnp.concatenate(
        [ref[pl.ds(t0 * SUBLANES + k, n, stride=SUBLANES), :] for k in range(SUBLANES)], axis=1)


def _tok_store(ref, t0, val):
    for k in range(SUBLANES):
        ref[pl.ds(t0 * SUBLANES + k, val.shape[0], stride=SUBLANES), :] = val[:, k * LANES:(k + 1) * LANES]


def _proj_ln_body(o_ref, w_ref, x_ref, g_ref, b_ref, xf_ref, xb_ref, x8_ref):
    y = _dot(o_ref[...], w_ref[...])
    out = _layer_norm_rows(DN_ALPHA * x_ref[...] + y, g_ref[...], b_ref[...])
    xf_ref[...] = out
    xb_ref[...] = out.astype(BF16)
    _tok_store(x8_ref, 0, out)


def _proj_ln(o, w, x, g, b, name, tm=512):
    m, k = o.shape
    d = w.shape[1]
    assert d == SUBLANES * LANES
    tm = min(tm, m)
    return pl.pallas_call(
        _proj_ln_body,
        out_shape=(jax.ShapeDtypeStruct((m, d), F32), jax.ShapeDtypeStruct((m, d), BF16),
                   jax.ShapeDtypeStruct((m * SUBLANES, LANES), F32)),
        grid=(m // tm,),
        in_specs=[pl.BlockSpec((tm, k), lambda i: (i, 0)),
                  pl.BlockSpec((k, d), lambda i: (0, 0)),
                  pl.BlockSpec((tm, d), lambda i: (i, 0)),
                  pl.BlockSpec((1, d), lambda i: (0, 0)),
                  pl.BlockSpec((1, d), lambda i: (0, 0))],
        out_specs=(pl.BlockSpec((tm, d), lambda i: (i, 0)),
                   pl.BlockSpec((tm, d), lambda i: (i, 0)),
                   pl.BlockSpec((tm * SUBLANES, LANES), lambda i: (i, 0))),
        compiler_params=_params(("parallel",)),
        name=name,
    )(o, w, x, g.reshape(1, d), b.reshape(1, d))


def _fox_gate_body(x_ref, w_ref, b_ref, c_ref, *, seq):
    f = _dot(x_ref[...], w_ref[...], HI)
    z = f.T[:FOX_HEADS, :] + b_ref[...]
    lf = _log_sigmoid(z)
    upper = _tri(LANES, lower=False).astype(F32)
    carry = jnp.zeros((FOX_HEADS, 1), F32)
    for blk in range(seq // LANES):
        seg = _dot(lf[:, blk * LANES:(blk + 1) * LANES], upper, HI) + carry
        c_ref[0, :, blk * LANES:(blk + 1) * LANES] = seg
        carry = seg[:, LANES - 1:LANES]


def _fox_gate(x, w_f, b_f, batch, seq):
    d = x.shape[1]
    return pl.pallas_call(
        functools.partial(_fox_gate_body, seq=seq),
        out_shape=jax.ShapeDtypeStruct((batch, FOX_HEADS, seq), F32),
        grid=(batch,),
        in_specs=[pl.BlockSpec((seq, d), lambda b: (b, 0)),
                  pl.BlockSpec((d, LANES), lambda b: (0, 0)),
                  pl.BlockSpec((FOX_HEADS, 1), lambda b: (0, 0))],
        out_specs=pl.BlockSpec((1, FOX_HEADS, seq), lambda b: (b, 0, 0)),
        compiler_params=_params(("parallel",)),
        name="fox_gate",
    )(x, w_f, b_f)


def _fox_attn_body(q_ref, k_ref, v_ref, c_ref, o_ref, kh_ref, qh_ref, vt_ref, acc_ref, ml_ref, ot_ref,
                   *, seq, t):
    nq = seq // t
    hd = FOX_HEAD_DIM
    lane = lax.broadcasted_iota(jnp.int32, (1, LANES), 1)
    valid = _tri(t, lower=False)
    scale = hd ** -0.5
    for h in range(2):
        own = (lane >= h * hd) & (lane < (h + 1) * hd)
        aug = (1 - h) * hd
        c_col = c_ref[0, 0, :, h:h + 1]
        hi = c_col.astype(BF16).astype(F32)
        r1 = c_col - hi
        mid = r1.astype(BF16).astype(F32)
        lo = (r1 - mid).astype(BF16).astype(F32)
        sel = [(lane == aug + n).astype(F32) for n in range(3)]
        c_lanes = hi * sel[0] + mid * sel[1] + lo * sel[2]
        own_f = own.astype(F32)
        kh_ref[h] = (k_ref[...].astype(F32) * (scale * own_f) + c_lanes).astype(BF16)
        qh_ref[h] = (q_ref[...].astype(F32) * own_f - (sel[0] + sel[1] + sel[2])).astype(BF16)
    vt_ref[...] = v_ref[...].astype(F32).T.astype(BF16)

    def scores(tile):
        j, i, h = tile
        s = _dot_nt(kh_ref[h, j * t:(j + 1) * t, :], qh_ref[h, i * t:(i + 1) * t, :])
        return jnp.where(valid, s, -jnp.inf) if i == j else s

    def update(tile, s):
        j, i, h = tile
        rows = slice(h * hd, (h + 1) * hd)
        m_new = jnp.max(s, axis=0, keepdims=True)
        if j > 0:
            m_old = ml_ref[i, h:h + 1, :]
            m_new = jnp.maximum(m_old, m_new)
        p = jnp.exp(s - m_new)
        l_new = jnp.sum(p, axis=0, keepdims=True)
        acc = _dot(vt_ref[rows, j * t:(j + 1) * t], p.astype(BF16))
        if j > 0:
            a = jnp.exp(m_old - m_new)
            l_new = a * ml_ref[i, 2 + h:3 + h, :] + l_new
            acc = a * acc_ref[i, rows, :] + acc
        if i == j:
            ot_ref[rows, i * t:(i + 1) * t] = acc / l_new
        else:
            ml_ref[i, h:h + 1, :] = m_new
            ml_ref[i, 2 + h:3 + h, :] = l_new
            acc_ref[i, rows, :] = acc

    tiles = [(j, i, h) for j in range(nq) for i in range(j, nq) for h in range(2)]
    ahead = 2
    pending = []
    for idx in range(len(tiles) + ahead):
        if idx < len(tiles):
            pending.append(scores(tiles[idx]))
        if idx >= ahead:
            update(tiles[idx - ahead], pending.pop(0))
    o_ref[...] = ot_ref[...].T.astype(o_ref.dtype)


def _fox_attn(qkv, c, batch, seq, t=256):
    t = min(t, seq)
    pairs = FOX_HEADS // 2
    c_cols = c.reshape(batch, pairs, 2, seq).transpose(0, 1, 3, 2)
    return pl.pallas_call(
        functools.partial(_fox_attn_body, seq=seq, t=t),
        out_shape=jax.ShapeDtypeStruct((batch * seq, D_MODEL), BF16),
        grid=(batch, pairs),
        in_specs=[pl.BlockSpec((seq, LANES), lambda b, p: (b, p)),
                  pl.BlockSpec((seq, LANES), lambda b, p: (b, pairs + p)),
                  pl.BlockSpec((seq, LANES), lambda b, p: (b, 2 * pairs + p)),
                  pl.BlockSpec((1, 1, seq, 2), lambda b, p: (b, p, 0, 0))],
        out_specs=pl.BlockSpec((seq, LANES), lambda b, p: (b, p)),
        scratch_shapes=[pltpu.VMEM((2, seq, LANES), BF16),
                        pltpu.VMEM((2, seq, LANES), BF16),
                        pltpu.VMEM((LANES, seq), BF16),
                        pltpu.VMEM((seq // t, LANES, t), F32),
                        pltpu.VMEM((seq // t, SUBLANES, t), F32),
                        pltpu.VMEM((LANES, seq), F32)],
        compiler_params=_params(("parallel", "parallel")),
        name="fox_attn",
    )(qkv, qkv, qkv, c_cols)


def _fox_layer(xf, xb, w_in, b_f, w_out, ln_g, ln_b, batch, seq):
    d = D_MODEL
    qkv = _matmul(xb, w_in[:, :3 * d].astype(BF16), BF16, "fox_qkv")
    w_f = jnp.pad(w_in[:, 3 * d:], ((0, 0), (0, LANES - FOX_HEADS)))
    c = _fox_gate(xf, w_f, b_f.reshape(FOX_HEADS, 1), batch, seq)
    o = _fox_attn(qkv, c, batch, seq)
    return _proj_ln(o, w_out.astype(BF16), xf, ln_g, ln_b, "fox_out_ln")


def _gla_body(q_ref, k_ref, v_ref, r_ref, gl_ref, wgu_ref, bg_ref, ng_ref, o_ref, bc_ref, st_ref, *, seq):
    ch = GLA_CHUNK
    tile = min(256, seq)
    r_i = lax.broadcasted_iota(jnp.int32, (tile, tile), 0)
    c_i = lax.broadcasted_iota(jnp.int32, (tile, tile), 1)
    same_chunk_lower = ((r_i >= c_i) & (r_i // ch == c_i // ch)).astype(F32)
    for t0 in range(0, seq, tile):
        gate = _dot(gl_ref[t0:t0 + tile, :], wgu_ref[...], HI) + bg_ref[...]
        log_a = _log_sigmoid(gate) * (1.0 / GLA_GATE_TAU)
        bc_ref[t0:t0 + tile, :] = _dot(same_chunk_lower, log_a, HI)
    lower = _tri(ch, lower=True)
    st_ref[...] = jnp.zeros_like(st_ref)
    heads = range(GLA_HEADS)

    def chunk(i, carry):
        rows = pl.ds(pl.multiple_of(i * ch, ch), ch)
        bcum = bc_ref[rows, :]
        b_last = bcum[ch - 1:ch, :]
        q = q_ref[rows, :].astype(F32) * (GLA_HK ** -0.5)
        k = k_ref[rows, :].astype(F32)
        q_dec = (q * jnp.exp(bcum)).astype(BF16)
        k_inv = (k * jnp.exp(-bcum)).astype(BF16)
        k_end = (k * jnp.exp(b_last - bcum)).astype(BF16)
        decay = jnp.exp(b_last)
        kl = [slice(h * GLA_HK, (h + 1) * GLA_HK) for h in heads]
        vl = [slice(h * GLA_HV, (h + 1) * GLA_HV) for h in heads]
        v = [v_ref[rows, vl[h]] for h in heads]
        att = [jnp.where(lower, _dot_nt(q_dec[:, kl[h]], k_inv[:, kl[h]]), 0.0).astype(BF16)
               for h in heads]
        state_t = [st_ref[h] for h in heads]
        o_inter = [_dot_nt(q_dec[:, kl[h]], state_t[h].astype(BF16)) for h in heads]
        o_intra = [_dot(att[h], v[h]) for h in heads]
        kv_t = [_dot_tn(v[h], k_end[:, kl[h]]) for h in heads]
        for h in heads:
            st_ref[h] = state_t[h] * decay[:, kl[h]] + kv_t[h]
            o = o_intra[h] + o_inter[h]
            ms = jnp.mean(o * o, axis=-1, keepdims=True)
            on = o * lax.rsqrt(ms + RMS_EPS) * ng_ref[...]
            o_ref[rows, vl[h]] = (on * _silu(r_ref[rows, vl[h]].astype(F32))).astype(o_ref.dtype)
        return carry

    lax.fori_loop(0, seq // ch, chunk, 0)


def _gla_mix(proj, g_low, w_gu, b_gate, norm_g, batch, seq):
    return pl.pallas_call(
        functools.partial(_gla_body, seq=seq),
        out_shape=jax.ShapeDtypeStruct((batch * seq, GLA_DV), BF16),
        grid=(batch,),
        in_specs=[pl.BlockSpec((seq, GLA_DK), lambda b: (b, 0)),
                  pl.BlockSpec((seq, GLA_DK), lambda b: (b, 1)),
                  pl.BlockSpec((seq, GLA_DV), lambda b: (b, 1)),
                  pl.BlockSpec((seq, GLA_DV), lambda b: (b, 2)),
                  pl.BlockSpec((seq, LANES), lambda b: (b, 0)),
                  pl.BlockSpec((LANES, GLA_DK), lambda b: (0, 0)),
                  pl.BlockSpec((1, GLA_DK), lambda b: (0, 0)),
                  pl.BlockSpec((1, GLA_HV), lambda b: (0, 0))],
        out_specs=pl.BlockSpec((seq, GLA_DV), lambda b: (b, 0)),
        scratch_shapes=[pltpu.VMEM((seq, GLA_DK), F32),
                        pltpu.VMEM((GLA_HEADS, GLA_HV, GLA_HK), F32)],
        compiler_params=_params(("parallel",)),
        name="gla_mix",
    )(proj, proj, proj, proj, g_low, w_gu, b_gate.reshape(1, GLA_DK), norm_g.reshape(1, GLA_HV))


def _gla_layer(xf, xb, w_in, w_gate_up, b_gate, norm_g, w_out, ln_g, ln_b, batch, seq):
    n_main = 2 * GLA_DK + 2 * GLA_DV
    proj = _matmul(xb, w_in[:, :n_main].astype(BF16), BF16, "gla_proj")
    w_g = jnp.pad(w_in[:, n_main:], ((0, 0), (0, LANES - GLA_GATE_RANK)))
    g_low = _matmul(xf, w_g, F32, "gla_gate", tn=LANES, precision=HI)
    w_gu = jnp.pad(w_gate_up, ((0, LANES - GLA_GATE_RANK), (0, 0)))
    y = _gla_mix(proj, g_low, w_gu, b_gate, norm_g, batch, seq)
    return _proj_ln(y, w_out.astype(BF16), xf, ln_g, ln_b, "gla_out_ln")


def _ssd_gate_body(x_ref, w_ref, bias_ref, alog_ref, dt_ref, ac_ref, dtt_ref, act_ref, *, seq):
    ch = SSD_CHUNK
    raw = _dot(x_ref[...], w_ref[...], HI)
    dt = _softplus(raw + bias_ref[...])
    a = dt * (-jnp.exp(alog_ref[...]))
    lower_f = _tri(ch, lower=True).astype(F32)
    dt_ref[...] = dt
    dtt_ref[0] = dt.T
    for c in range(seq // ch):
        acum = _dot(lower_f, a[c * ch:(c + 1) * ch, :], HI)
        ac_ref[c * ch:(c + 1) * ch, :] = acum
        act_ref[0, :, c * ch:(c + 1) * ch] = acum.T


def _ssd_gate(x, w_dt, dt_bias, a_log, batch, seq):
    d = x.shape[1]
    col = jax.ShapeDtypeStruct((batch * seq, LANES), F32)
    row = jax.ShapeDtypeStruct((batch, LANES, seq), F32)
    return pl.pallas_call(
        functools.partial(_ssd_gate_body, seq=seq),
        out_shape=(col, col, row, row),
        grid=(batch,),
        in_specs=[pl.BlockSpec((seq, d), lambda b: (b, 0)),
                  pl.BlockSpec((d, LANES), lambda b: (0, 0)),
                  pl.BlockSpec((1, LANES), lambda b: (0, 0)),
                  pl.BlockSpec((1, LANES), lambda b: (0, 0))],
        out_specs=(pl.BlockSpec((seq, LANES), lambda b: (b, 0)),
                   pl.BlockSpec((seq, LANES), lambda b: (b, 0)),
                   pl.BlockSpec((1, LANES, seq), lambda b: (b, 0, 0)),
                   pl.BlockSpec((1, LANES, seq), lambda b: (b, 0, 0))),
        compiler_params=_params(("parallel",)),
        name="ssd_gate",
    )(x, w_dt, dt_bias, a_log)


def _expand_heads(cols, width):
    rows = cols.shape[0]
    lane = lax.broadcasted_iota(jnp.int32, (rows, SSD_HG * width), 1)
    out = jnp.broadcast_to(cols[:, SSD_HG - 1:SSD_HG], (rows, SSD_HG * width))
    for h in range(SSD_HG - 2, -1, -1):
        out = jnp.where(lane < (h + 1) * width, jnp.broadcast_to(cols[:, h:h + 1], out.shape), out)
    return out


def _ssd_body(z_ref, xr_ref, br_ref, cr_ref, cwx_ref, cwb_ref, cwc_ref, cbx_ref, cbb_ref, cbc_ref,
              dt4_ref, ac4_ref, dtt_ref, act_ref, dskip_ref, ng_ref, o_ref,
              padx_ref, padb_ref, xs_ref, bm_ref, cm_ref, st_ref, *, seq):
    ch = SSD_CHUNK
    hp = SSD_HG * SSD_P
    n = SSD_STATE
    pad = SUBLANES

    def conv(raw_ref, pad_ref, w_ref, b_ref, dst_ref):
        pad_ref[0:pad, :] = jnp.zeros((pad, pad_ref.shape[1]), F32)
        pad_ref[pad:pad + seq, :] = raw_ref[...].astype(F32)
        tile = min(256, seq)
        width = min(256, pad_ref.shape[1])
        for c0 in range(0, pad_ref.shape[1], width):
            cols = slice(c0, c0 + width)
            for t0 in range(0, seq, tile):
                acc = jnp.broadcast_to(b_ref[:, cols], (tile, width))
                for j in range(SSD_CONV):
                    off = t0 + pad - (SSD_CONV - 1) + j
                    acc = acc + w_ref[j:j + 1, cols] * pad_ref[off:off + tile, cols]
                dst_ref[t0:t0 + tile, cols] = _silu(acc)

    conv(xr_ref, padx_ref, cwx_ref, cbx_ref, xs_ref)
    conv(br_ref, padb_ref, cwb_ref, cbb_ref, bm_ref)
    conv(cr_ref, padb_ref, cwc_ref, cbc_ref, cm_ref)

    lower = _tri(ch, lower=True)
    lane = lax.broadcasted_iota(jnp.int32, (1, hp), 1)
    st_ref[...] = jnp.zeros_like(st_ref)

    def group_chunk(gi, c, rows):
        cols = slice(gi * hp, (gi + 1) * hp)
        ncols = slice(gi * n, (gi + 1) * n)
        xs = xs_ref[rows, cols]
        bm = bm_ref[rows, ncols].astype(BF16)
        cm = cm_ref[rows, ncols].astype(BF16)
        dt4 = dt4_ref[0, gi, rows, :]
        ac4 = ac4_ref[0, gi, rows, :]
        cb = _dot_nt(cm, bm)
        yield
        state = st_ref[gi]
        y_off = _dot(cm, state.astype(BF16))
        yield
        to_end = jnp.exp(ac4[ch - 1:ch, :] - ac4)
        w_state = _expand_heads(dt4 * to_end, SSD_P)
        e_acum = _expand_heads(jnp.exp(ac4), SSD_P)
        new_state = _dot_tn(bm, (xs * w_state).astype(BF16))
        yield
        st_ref[gi] = state * e_acum[ch - 1:ch, :] + new_state
        xs_b = xs.astype(BF16)
        y = xs * dskip_ref[:, cols] + y_off * e_acum
        for h in range(SSD_HG):
            a_col = jnp.broadcast_to(ac4[:, h:h + 1], (ch, ch))
            a_row = act_ref[0, gi, h, pl.ds(c, 1), :]
            d_row = dtt_ref[0, gi, h, pl.ds(c, 1), :]
            decay = jnp.exp(jnp.where(lower, a_col - a_row, -jnp.inf))
            m_h = (cb * decay * d_row).astype(BF16)
            head = (lane >= h * SSD_P) & (lane < (h + 1) * SSD_P)
            y = y + _dot(m_h, jnp.where(head, xs_b, jnp.zeros_like(xs_b)))
            yield
        y = y * _silu(z_ref[rows, cols].astype(F32))
        ms = jnp.mean(y * y, axis=-1, keepdims=True)
        o_ref[rows, cols] = (y * lax.rsqrt(ms + RMS_EPS) * ng_ref[:, cols]).astype(o_ref.dtype)

    def chunk(c, carry):
        rows = pl.ds(pl.multiple_of(c * ch, ch), ch)
        running = [group_chunk(gi, c, rows) for gi in range(SSD_GROUPS_PER_STEP)]
        while running:
            for gen in list(running):
                if next(gen, "done") == "done":
                    running.remove(gen)
        return carry

    lax.fori_loop(0, seq // ch, chunk, 0)


def _ssd_mix(proj, conv_w, conv_b, dt4, ac4, dtt, act, d_exp, norm_g, batch, seq):
    per = SSD_GROUPS_PER_STEP
    steps = SSD_GROUPS // per
    hp = per * SSD_HG * SSD_P
    n = per * SSD_STATE
    nz = SSD_D_INNER // hp
    nb = 2 * SSD_D_INNER // n
    nc = nb + steps
    cb0 = SSD_D_INNER // n
    nch = seq // SSD_CHUNK
    return pl.pallas_call(
        functools.partial(_ssd_body, seq=seq),
        out_shape=jax.ShapeDtypeStruct((batch * seq, SSD_D_INNER), BF16),
        grid=(batch, steps),
        in_specs=[pl.BlockSpec((seq, hp), lambda b, i: (b, i)),
                  pl.BlockSpec((seq, hp), lambda b, i: (b, nz + i)),
                  pl.BlockSpec((seq, n), lambda b, i: (b, nb + i)),
                  pl.BlockSpec((seq, n), lambda b, i: (b, nc + i)),
                  pl.BlockSpec((SSD_CONV, hp), lambda b, i: (0, i)),
                  pl.BlockSpec((SSD_CONV, n), lambda b, i: (0, cb0 + i)),
                  pl.BlockSpec((SSD_CONV, n), lambda b, i: (0, cb0 + steps + i)),
                  pl.BlockSpec((1, hp), lambda b, i: (0, i)),
                  pl.BlockSpec((1, n), lambda b, i: (0, cb0 + i)),
                  pl.BlockSpec((1, n), lambda b, i: (0, cb0 + steps + i)),
                  pl.BlockSpec((1, per, seq, SSD_HG), lambda b, i: (b, i, 0, 0)),
                  pl.BlockSpec((1, per, seq, SSD_HG), lambda b, i: (b, i, 0, 0)),
                  pl.BlockSpec((1, per, SSD_HG, nch, SSD_CHUNK), lambda b, i: (b, i, 0, 0, 0)),
                  pl.BlockSpec((1, per, SSD_HG, nch, SSD_CHUNK), lambda b, i: (b, i, 0, 0, 0)),
                  pl.BlockSpec((1, hp), lambda b, i: (0, i)),
                  pl.BlockSpec((1, hp), lambda b, i: (0, i))],
        out_specs=pl.BlockSpec((seq, hp), lambda b, i: (b, i)),
        scratch_shapes=[pltpu.VMEM((seq + SUBLANES, hp), F32),
                        pltpu.VMEM((seq + SUBLANES, n), F32),
                        pltpu.VMEM((seq, hp), F32),
                        pltpu.VMEM((seq, n), F32),
                        pltpu.VMEM((seq, n), F32),
                        pltpu.VMEM((per, SSD_STATE, SSD_HG * SSD_P), F32)],
        compiler_params=_params(("parallel", "parallel")),
        name="ssd_mix",
    )(proj, proj, proj, proj, conv_w, conv_w, conv_w, conv_b, conv_b, conv_b,
      dt4, ac4, dtt, act, d_exp, norm_g)


def _ssd_layer(xf, xb, w_in, conv_w, conv_b, dt_bias, a_log, d_skip, norm_g, w_out, ln_g, ln_b,
               batch, seq):
    n_main = SSD_D_INNER + (SSD_D_INNER + 2 * SSD_GROUPS * SSD_STATE)
    proj = _matmul(xb, w_in[:, :n_main].astype(BF16), BF16, "ssd_proj")
    lane_pad = ((0, 0), (0, LANES - SSD_HEADS))
    w_dt = jnp.pad(w_in[:, n_main:], lane_pad)
    dt_c, ac_c, dt_r, ac_r = _ssd_gate(xf, w_dt, jnp.pad(dt_bias.reshape(1, -1), lane_pad),
                                       jnp.pad(a_log.reshape(1, -1), lane_pad), batch, seq)

    def cols(t):
        return t[:, :SSD_HEADS].reshape(batch, seq, SSD_GROUPS, SSD_HG).transpose(0, 2, 1, 3)

    def rows(t):
        return t[:, :SSD_HEADS].reshape(batch, SSD_GROUPS, SSD_HG, seq // SSD_CHUNK, SSD_CHUNK)

    d_exp = jnp.repeat(d_skip, SSD_P).reshape(1, SSD_D_INNER)
    y = _ssd_mix(proj, conv_w, conv_b.reshape(1, -1), cols(dt_c), cols(ac_c), rows(dt_r), rows(ac_r),
                 d_exp, norm_g.reshape(1, SSD_D_INNER), batch, seq)
    return _proj_ln(y, w_out.astype(BF16), xf, ln_g, ln_b, "ssd_out_ln")


def _router_body(x_ref, w_ref, bias_ref, e_ref, w_out_ref):
    tm = x_ref.shape[0]
    gsz = N_EXPERTS // N_EXPERT_GROUPS
    logits = _dot(x_ref[...], w_ref[...], HI).T[:N_EXPERTS, :]
    scores = _sigmoid(logits)
    sel = scores + bias_ref[...]
    sub = lax.broadcasted_iota(jnp.int32, (gsz, tm), 0).astype(F32)
    neg = jnp.float32(-jnp.inf)

    def top1(v, idx, sentinel):
        m = jnp.max(v, axis=0, keepdims=True)
        i = jnp.min(jnp.where(v == m, idx, sentinel), axis=0, keepdims=True)
        return m, i

    s_g = [sel[g * gsz:(g + 1) * gsz, :] for g in range(N_EXPERT_GROUPS)]
    sc_g = [scores[g * gsz:(g + 1) * gsz, :] for g in range(N_EXPERT_GROUPS)]
    grp_rows = []
    for g in range(N_EXPERT_GROUPS):
        m1, i1 = top1(s_g[g], sub, float(gsz))
        m2 = jnp.max(jnp.where(sub == i1, neg, s_g[g]), axis=0, keepdims=True)
        grp_rows.append(m1 + m2)
    grp = jnp.concatenate(grp_rows, axis=0)
    keep = jnp.zeros((N_EXPERT_GROUPS, tm), F32)
    for _ in range(TOPK_GROUPS):
        _, ig = top1(grp, sub, float(N_EXPERT_GROUPS))
        hit = sub == ig
        keep = jnp.where(hit, 1.0, keep)
        grp = jnp.where(hit, neg, grp)
    cand = [jnp.where(keep[g:g + 1, :] > 0.5, s_g[g], neg) for g in range(N_EXPERT_GROUPS)]
    ids = [sub + float(g * gsz) for g in range(N_EXPERT_GROUPS)]
    e_rows, w_rows = [], []
    for _ in range(TOP_K):
        m = cand[0]
        for g in range(1, N_EXPERT_GROUPS):
            m = jnp.maximum(m, cand[g])
        m = jnp.max(m, axis=0, keepdims=True)
        first = jnp.where(cand[0] == m, ids[0], float(N_EXPERTS))
        for g in range(1, N_EXPERT_GROUPS):
            first = jnp.minimum(first, jnp.where(cand[g] == m, ids[g], float(N_EXPERTS)))
        first = jnp.min(first, axis=0, keepdims=True)
        wsum = jnp.zeros((gsz, tm), F32)
        for g in range(N_EXPERT_GROUPS):
            hit = ids[g] == first
            wsum = wsum + jnp.where(hit, sc_g[g], 0.0)
            cand[g] = jnp.where(hit, neg, cand[g])
        e_rows.append(first)
        w_rows.append(jnp.sum(wsum, axis=0, keepdims=True))
    top_e = jnp.concatenate(e_rows, axis=0)
    top_w = jnp.concatenate(w_rows, axis=0)
    denom = jnp.sum(top_w, axis=0, keepdims=True)
    e_ref[...] = top_e.astype(jnp.int32)
    w_out_ref[...] = top_w / denom * ROUTED_SCALE


def _router(x, w_router, bias, tm=512):
    t, d = x.shape
    tm = min(tm, t)
    return pl.pallas_call(
        _router_body,
        out_shape=(jax.ShapeDtypeStruct((TOP_K, t), jnp.int32),
                   jax.ShapeDtypeStruct((TOP_K, t), F32)),
        grid=(t // tm,),
        in_specs=[pl.BlockSpec((tm, d), lambda i: (i, 0)),
                  pl.BlockSpec((d, LANES), lambda i: (0, 0)),
                  pl.BlockSpec((N_EXPERTS, 1), lambda i: (0, 0))],
        out_specs=(pl.BlockSpec((TOP_K, tm), lambda i: (0, i)),
                   pl.BlockSpec((TOP_K, tm), lambda i: (0, i))),
        compiler_params=_params(("parallel",)),
        name="moe_router",
    )(x, jnp.pad(w_router, ((0, 0), (0, LANES - N_EXPERTS))), bias.reshape(N_EXPERTS, 1))


def _moe_list_len(tb):
    chunks = MOE_LEAD_CHUNKS + -(-tb * TOP_K // MOE_ROWS) + N_EXPERTS + N_EXPERTS // MOE_EXPERTS_PER_STEP + 2
    return -(-chunks * MOE_ROWS // MOE_LIST_ALIGN) * MOE_LIST_ALIGN


def _moe_body(sstart_ref, scount_ref, cexp_ref, gt_hbm, st_hbm,
              rw_ref, x8_ref, wg_ref, wu_ref, wd_ref, sg_ref, su_ref, sd_ref,
              out_ref, gt_ref, st_ref, buf_ref, y2_ref, sem, *, tb):
    blk = pl.program_id(0)
    grp = pl.program_id(1)
    rows_per = MOE_ROWS
    tile = min(256, tb)
    n_list = st_ref.shape[0]
    n_chunks = n_list // rows_per
    acc_ref = out_ref.at[0]

    def tile_at(ref, off):
        return ref.at[pl.ds(pl.multiple_of(off, SUBLANES), SUBLANES), :]

    def gather(chunk, parity):
        src = gt_ref.at[pl.ds((chunk + MOE_LEAD_CHUNKS) * rows_per, rows_per)]
        dst = buf_ref.at[parity]
        for r in range(rows_per):
            tile_at(dst, r * SUBLANES)[...] = tile_at(x8_ref, src[r])[...]

    def scatter(chunk, parity):
        y_ref = y2_ref.at[parity]
        dst = st_ref.at[pl.ds((chunk + MOE_LEAD_CHUNKS) * rows_per, rows_per)]
        for r0 in range(0, rows_per, MOE_SCATTER_GROUP):
            offs = [dst[r0 + u] for u in range(MOE_SCATTER_GROUP)]
            vals = [tile_at(acc_ref, offs[u])[...] + tile_at(y_ref, (r0 + u) * SUBLANES)[...]
                    for u in range(MOE_SCATTER_GROUP)]
            for u in range(MOE_SCATTER_GROUP):
                tile_at(acc_ref, offs[u])[...] = vals[u]

    def row_weights(chunk):
        w_row = rw_ref[0, pl.ds(chunk + MOE_LEAD_CHUNKS, 1), :]
        return jnp.broadcast_to(w_row, (rows_per, rows_per)).T

    @pl.when(grp == 0)
    def _start_block():
        copies = [pltpu.make_async_copy(src.at[pl.ds(pl.multiple_of(blk * n_list, MOE_LIST_ALIGN), n_list)],
                                        dst, sem.at[n])
                  for n, (src, dst) in enumerate(((gt_hbm, gt_ref), (st_hbm, st_ref)))]
        for cp in copies:
            cp.start()
        for t0 in range(0, tb, tile):
            xb = _tok_rows(x8_ref, t0, tile).astype(BF16)
            h = _silu(_dot(xb, sg_ref[...])) * _dot(xb, su_ref[...])
            _tok_store(acc_ref, t0, _dot(h.astype(BF16), sd_ref[...]))
        tile_at(acc_ref, tb * SUBLANES)[...] = jnp.zeros((SUBLANES, LANES), F32)
        y2_ref[...] = jnp.zeros_like(y2_ref)
        for cp in copies:
            cp.wait()
        gather(0, 0)
        gather(1, 1)

    step_id = blk * pl.num_programs(1) + grp
    first = sstart_ref[step_id]
    pairs = scount_ref[step_id] // 2

    def step(i, carry):
        g = first + 2 * i
        xb = [_tok_rows(buf_ref.at[p], 0, rows_per).astype(BF16) for p in range(2)]
        e = [cexp_ref[blk * n_chunks + g + p] for p in range(2)]
        for p in range(2):
            scatter(g - 2 + p, p)
        for p in range(2):
            gather(g + 2 + p, p)
        gate = [_dot(xb[p], wg_ref[e[p]]) for p in range(2)]
        up = [_dot(xb[p], wu_ref[e[p]]) for p in range(2)]
        h = [(_silu(gate[p]) * up[p]).astype(BF16) for p in range(2)]
        y = [_dot(h[p], wd_ref[e[p]]) for p in range(2)]
        for p in range(2):
            w_rows = jnp.concatenate([row_weights(g + p)] * (y[p].shape[1] // rows_per), axis=1)
            _tok_store(y2_ref.at[p], 0, y[p] * w_rows)
        return carry

    lax.fori_loop(0, pairs, step, 0)

    @pl.when(grp == pl.num_programs(1) - 1)
    def _finish_block():
        last = first + 2 * pairs
        for p in range(2):
            scatter(last - 2 + p, p)


def _moe_experts(x8, sstart, scount, cexp, gt, st, rw, wg, wu, wd, layer, sg, su, sd, tb):
    d = wg.shape[2]
    ff = wg.shape[3]
    nblk = x8.shape[0] // (tb * SUBLANES)
    n_list = _moe_list_len(tb)
    per_step = MOE_EXPERTS_PER_STEP
    grid_spec = pltpu.PrefetchScalarGridSpec(
        num_scalar_prefetch=3,
        grid=(nblk, N_EXPERTS // per_step),
        in_specs=[pl.BlockSpec(memory_space=pl.ANY),
                  pl.BlockSpec(memory_space=pl.ANY),
                  pl.BlockSpec((1, n_list // MOE_ROWS, MOE_ROWS), lambda i, e, *_: (i, 0, 0)),
                  pl.BlockSpec((tb * SUBLANES, LANES), lambda i, e, *_: (i, 0)),
                  pl.BlockSpec((None, per_step, d, ff), lambda i, e, *_: (layer, e, 0, 0)),
                  pl.BlockSpec((None, per_step, d, ff), lambda i, e, *_: (layer, e, 0, 0)),
                  pl.BlockSpec((None, per_step, ff, d), lambda i, e, *_: (layer, e, 0, 0)),
                  pl.BlockSpec((d, ff), lambda i, e, *_: (0, 0)),
                  pl.BlockSpec((d, ff), lambda i, e, *_: (0, 0)),
                  pl.BlockSpec((ff, d), lambda i, e, *_: (0, 0))],
        out_specs=pl.BlockSpec((1, (tb + 1) * SUBLANES, LANES), lambda i, e, *_: (i, 0, 0)),
        scratch_shapes=[pltpu.SMEM((n_list,), jnp.int32),
                        pltpu.SMEM((n_list,), jnp.int32),
                        pltpu.VMEM((2, MOE_ROWS * SUBLANES, LANES), F32),
                        pltpu.VMEM((2, MOE_ROWS * SUBLANES, LANES), F32),
                        pltpu.SemaphoreType.DMA((2,))],
    )
    return pl.pallas_call(
        functools.partial(_moe_body, tb=tb),
        out_shape=jax.ShapeDtypeStruct((nblk, (tb + 1) * SUBLANES, LANES), F32),
        grid_spec=grid_spec,
        compiler_params=_params(("arbitrary", "arbitrary")),
        name="moe_experts",
    )(sstart, scount, cexp, gt, st, rw, x8, wg, wu, wd, sg, su, sd)


def _moe_finish_body(x_ref, acc_ref, g_ref, b_ref, xf_ref, xb_ref):
    z = DN_ALPHA * x_ref[...] + _tok_rows(acc_ref.at[0], 0, x_ref.shape[0])
    out = _layer_norm_rows(z, g_ref[...], b_ref[...])
    xf_ref[...] = out
    xb_ref[...] = out.astype(BF16)


def _moe_finish(x, acc8, g, b, tb, tm=512):
    m, d = x.shape
    tm = min(tm, tb)
    per_blk = tb // tm
    return pl.pallas_call(
        _moe_finish_body,
        out_shape=(jax.ShapeDtypeStruct((m, d), F32), jax.ShapeDtypeStruct((m, d), BF16)),
        grid=(m // tm,),
        in_specs=[pl.BlockSpec((tm, d), lambda i: (i, 0)),
                  pl.BlockSpec((1, tm * SUBLANES, LANES), lambda i: (i // per_blk, i % per_blk, 0)),
                  pl.BlockSpec((1, d), lambda i: (0, 0)),
                  pl.BlockSpec((1, d), lambda i: (0, 0))],
        out_specs=(pl.BlockSpec((tm, d), lambda i: (i, 0)),
                   pl.BlockSpec((tm, d), lambda i: (i, 0))),
        compiler_params=_params(("parallel",)),
        name="moe_finish",
    )(x, acc8, g.reshape(1, d), b.reshape(1, d))


def _moe_layer(xf, x8, w_router, router_bias, wg, wu, wd, layer, ws_gate, ws_up, ws_down,
               ln_g, ln_b, tb):
    t, d = xf.shape
    top_e, top_w = _router(xf, w_router, router_bias)
    nblk = t // tb
    per = tb * TOP_K
    e_flat = top_e.T.reshape(nblk, per)
    w_flat = top_w.T.reshape(nblk, per)
    order = jnp.argsort(e_flat, axis=-1, stable=True).astype(jnp.int32)
    cnt = jnp.sum(e_flat[:, :, None] == jnp.arange(N_EXPERTS, dtype=jnp.int32), axis=1,
                  dtype=jnp.int32)
    nch = (cnt + MOE_ROWS - 1) // MOE_ROWS
    groups = N_EXPERTS // MOE_EXPERTS_PER_STEP
    gpad = jnp.sum(nch.reshape(nblk, groups, MOE_EXPERTS_PER_STEP), axis=-1) % 2
    pads_before = jnp.cumsum(gpad, axis=-1, dtype=jnp.int32) - gpad
    cend = (jnp.cumsum(nch, axis=-1, dtype=jnp.int32)
            + jnp.repeat(pads_before, MOE_EXPERTS_PER_STEP, axis=-1))
    n_list = _moe_list_len(tb)
    n_chunks = n_list // MOE_ROWS
    chunk = jnp.arange(n_chunks, dtype=jnp.int32) - MOE_LEAD_CHUNKS
    before = cend[:, None, :] <= chunk[None, :, None]
    exp_of = jnp.sum(before, axis=-1, dtype=jnp.int32)
    earlier = jnp.arange(groups, dtype=jnp.int32) < (exp_of // MOE_EXPERTS_PER_STEP)[..., None]
    first_chunk = (jnp.sum(jnp.where(before, nch[:, None, :], 0), axis=-1)
                   + jnp.sum(jnp.where(earlier, gpad[:, None, :], 0), axis=-1))
    first_src = jnp.sum(jnp.where(before, cnt[:, None, :], 0), axis=-1)
    own = jnp.arange(N_EXPERTS, dtype=jnp.int32) == exp_of[..., None]
    n_rows = jnp.sum(jnp.where(own, cnt[:, None, :], 0), axis=-1)
    row = ((chunk[None, :] - first_chunk) * MOE_ROWS)[..., None] + jnp.arange(MOE_ROWS, dtype=jnp.int32)
    valid = ((chunk[None, :, None] >= 0) & (row >= 0) & (row < n_rows[..., None])).reshape(nblk, n_list)
    src = lax.optimization_barrier(
        jnp.clip(first_src[..., None] + row, 0, per - 1).reshape(nblk, n_list))
    picked = lax.optimization_barrier(jnp.take_along_axis(order, src, axis=-1))
    tile_off = (picked // TOP_K) * SUBLANES
    gt = jnp.where(valid, tile_off, 0)
    st = jnp.where(valid, tile_off, tb * SUBLANES)
    rw_pad = jnp.where(valid, jnp.take_along_axis(w_flat, picked, axis=-1), 0.0)
    gend = cend.reshape(nblk, groups, MOE_EXPERTS_PER_STEP)[:, :, -1] + gpad
    gstart = jnp.concatenate([jnp.zeros((nblk, 1), jnp.int32), gend[:, :-1]], axis=1)
    cexp = jnp.minimum(exp_of, N_EXPERTS - 1) % MOE_EXPERTS_PER_STEP
    cexp = jnp.roll(cexp, -MOE_LEAD_CHUNKS, axis=1)
    sg, su, sd = ws_gate.astype(BF16), ws_up.astype(BF16), ws_down.astype(BF16)
    acc8 = _moe_experts(x8, gstart.reshape(-1), (gend - gstart).reshape(-1), cexp.reshape(-1),
                        gt.reshape(-1), st.reshape(-1), rw_pad.reshape(nblk, n_chunks, MOE_ROWS),
                        wg, wu, wd, layer, sg, su, sd, tb)
    return _moe_finish(xf, acc8, ln_g, ln_b, tb)


def kernel(x, fox_w_in, fox_b_f, fox_w_out, gla_w_in, gla_w_gate_up, gla_b_gate, gla_norm, gla_w_out, ssd_w_in, ssd_conv_w, ssd_conv_b, ssd_dt_bias, ssd_a_log, ssd_d, ssd_norm, ssd_w_out, ln1_g, ln1_b, moe_router, moe_router_bias, moe_w_gate, moe_w_up, moe_w_down, moe_ws_gate, moe_ws_up, moe_ws_down, ln2_g, ln2_b):
    batch, seq, d = x.shape
    xf = x.reshape(batch * seq, d)
    xb = xf.astype(BF16)
    tb = seq
    wg_all, wu_all, wd_all = moe_w_gate.astype(BF16), moe_w_up.astype(BF16), moe_w_down.astype(BF16)
    for i in range(DEPTH):
        kind, j = i % N_MIXERS, i // N_MIXERS
        if kind == 0:
            xf, xb, x8 = _fox_layer(xf, xb, fox_w_in[j], fox_b_f[j], fox_w_out[j], ln1_g[i], ln1_b[i],
                                    batch, seq)
        elif kind == 1:
            xf, xb, x8 = _gla_layer(xf, xb, gla_w_in[j], gla_w_gate_up[j], gla_b_gate[j], gla_norm[j],
                                    gla_w_out[j], ln1_g[i], ln1_b[i], batch, seq)
        else:
            xf, xb, x8 = _ssd_layer(xf, xb, ssd_w_in[j], ssd_conv_w[j], ssd_conv_b[j], ssd_dt_bias[j],
                                    ssd_a_log[j], ssd_d[j], ssd_norm[j], ssd_w_out[j], ln1_g[i],
                                    ln1_b[i], batch, seq)
        xf, xb = _moe_layer(xf, x8, moe_router[i], moe_router_bias[i], wg_all, wu_all, wd_all, i,
                            moe_ws_gate[i], moe_ws_up[i], moe_ws_down[i], ln2_g[i], ln2_b[i], tb)
    return xf.reshape(batch, seq, d)
```

```python
import functools

import jax
import jax.numpy as jnp
from jax import lax
from jax.experimental import pallas as pl
from jax.experimental.pallas import tpu as pltpu

F32 = jnp.float32
BF16 = jnp.bfloat16
HI = lax.Precision.HIGHEST

D_MODEL = 1024
DEPTH = 4
N_MIXERS = 3
DN_ALPHA = (2 * DEPTH) ** 0.25
LN_EPS = 1e-5
RMS_EPS = 1e-6

FOX_HEADS = 16
FOX_HEAD_DIM = 64
GLA_HEADS = 4
GLA_DK = 512
GLA_DV = 1024
GLA_HK = 128
GLA_HV = 256
GLA_GATE_RANK = 16
GLA_GATE_TAU = 16.0
GLA_CHUNK = 64
SSD_D_INNER = 2048
SSD_HEADS = 32
SSD_GROUPS = 8
SSD_HG = 4
SSD_P = 64
SSD_STATE = 128
SSD_CONV = 4
SSD_CHUNK = 128
N_EXPERTS = 64
TOP_K = 8
N_EXPERT_GROUPS = 8
TOPK_GROUPS = 4
EXPERT_FF = 256
ROUTED_SCALE = 2.5

LANES = 128
SUBLANES = 8
VMEM_LIMIT = 56 * 2 ** 20
MOE_ROWS = 128
MOE_LIST_ALIGN = 1024
MOE_SCATTER_GROUP = 8
MOE_EXPERTS_PER_STEP = 4
MOE_LEAD_CHUNKS = 2
SSD_GROUPS_PER_STEP = 2
PROJ_TILE = 1024
ROW_BLOCK = 512
SUB_TILE = 256
FOX_TILE = 256


def _params(sem):
    return pltpu.CompilerParams(dimension_semantics=sem, vmem_limit_bytes=VMEM_LIMIT)


def _sigmoid(x):
    return 1.0 / (1.0 + jnp.exp(-x))


def _silu(x):
    return x * _sigmoid(x)


def _log_sigmoid(x):
    return jnp.minimum(x, 0.0) - jnp.log(1.0 + jnp.exp(-jnp.abs(x)))


def _softplus(x):
    return jnp.maximum(x, 0.0) + jnp.log(1.0 + jnp.exp(-jnp.abs(x)))


def _dot(a, b, precision=None):
    return jnp.dot(a, b, preferred_element_type=F32, precision=precision)


def _dot_nt(a, b, precision=None):
    return lax.dot_general(a, b, (((1,), (1,)), ((), ())), preferred_element_type=F32,
                           precision=precision)


def _dot_tn(a, b, precision=None):
    return lax.dot_general(a, b, (((0,), (0,)), ((), ())), preferred_element_type=F32,
                           precision=precision)


def _tri(n, lower):
    r = lax.broadcasted_iota(jnp.int32, (n, n), 0)
    c = lax.broadcasted_iota(jnp.int32, (n, n), 1)
    return (r >= c) if lower else (r <= c)


def _mm_body(x_ref, w_ref, o_ref, *, precision):
    o_ref[...] = _dot(x_ref[...], w_ref[...], precision).astype(o_ref.dtype)


def _matmul(x, w, out_dtype, name, tm=PROJ_TILE, tn=PROJ_TILE, precision=None):
    m, k = x.shape
    n = w.shape[1]
    tm, tn = min(tm, m), min(tn, n)
    return pl.pallas_call(
        functools.partial(_mm_body, precision=precision),
        out_shape=jax.ShapeDtypeStruct((m, n), out_dtype),
        grid=(n // tn, m // tm),
        in_specs=[pl.BlockSpec((tm, k), lambda j, i: (i, 0)),
                  pl.BlockSpec((k, tn), lambda j, i: (0, j))],
        out_specs=pl.BlockSpec((tm, tn), lambda j, i: (i, j)),
        compiler_params=_params(("parallel", "parallel")),
        name=name,
    )(x, w)


def _layer_norm_rows(z, g, b):
    mu = jnp.mean(z, axis=-1, keepdims=True)
    zc = z - mu
    var = jnp.mean(zc * zc, axis=-1, keepdims=True)
    return zc * lax.rsqrt(var + LN_EPS) * g + b


def _tok_rows(ref, t0, n):
    return jnp.concatenate(
        [ref[pl.ds(t0 * SUBLANES + k, n, stride=SUBLANES), :] for k in range(SUBLANES)], axis=1)


def _tok_store(ref, t0, val):
    for k in range(SUBLANES):
        ref[pl.ds(t0 * SUBLANES + k, val.shape[0], stride=SUBLANES), :] = val[:, k * LANES:(k + 1) * LANES]


def _proj_ln_body(o_ref, w_ref, x_ref, g_ref, b_ref, xf_ref, xb_ref, x8_ref):
    y = _dot(o_ref[...], w_ref[...])
    out = _layer_norm_rows(DN_ALPHA * x_ref[...] + y, g_ref[...], b_ref[...])
    xf_ref[...] = out
    xb_ref[...] = out.astype(BF16)
    _tok_store(x8_ref, 0, out)


def _proj_ln(o, w, x, g, b, name, tm=ROW_BLOCK):
    m, k = o.shape
    d = w.shape[1]
    assert d == SUBLANES * LANES
    tm = min(tm, m)
    return pl.pallas_call(
        _proj_ln_body,
        out_shape=(jax.ShapeDtypeStruct((m, d), F32), jax.ShapeDtypeStruct((m, d), BF16),
                   jax.ShapeDtypeStruct((m * SUBLANES, LANES), F32)),
        grid=(m // tm,),
        in_specs=[pl.BlockSpec((tm, k), lambda i: (i, 0)),
                  pl.BlockSpec((k, d), lambda i: (0, 0)),
                  pl.BlockSpec((tm, d), lambda i: (i, 0)),
                  pl.BlockSpec((1, d), lambda i: (0, 0)),
                  pl.BlockSpec((1, d), lambda i: (0, 0))],
        out_specs=(pl.BlockSpec((tm, d), lambda i: (i, 0)),
                   pl.BlockSpec((tm, d), lambda i: (i, 0)),
                   pl.BlockSpec((tm * SUBLANES, LANES), lambda i: (i, 0))),
        compiler_params=_params(("parallel",)),
        name=name,
    )(o, w, x, g.reshape(1, d), b.reshape(1, d))


def _fox_gate_body(x_ref, w_ref, b_ref, c_ref, *, seq):
    f = _dot(x_ref[...], w_ref[...], HI)
    z = f.T[:FOX_HEADS, :] + b_ref[...]
    lf = _log_sigmoid(z)
    upper = _tri(LANES, lower=False).astype(F32)
    carry = jnp.zeros((FOX_HEADS, 1), F32)
    for blk in range(seq // LANES):
        seg = _dot(lf[:, blk * LANES:(blk + 1) * LANES], upper, HI) + carry
        c_ref[0, :, blk * LANES:(blk + 1) * LANES] = seg
        carry = seg[:, LANES - 1:LANES]


def _fox_gate(x, w_f, b_f, batch, seq):
    d = x.shape[1]
    return pl.pallas_call(
        functools.partial(_fox_gate_body, seq=seq),
        out_shape=jax.ShapeDtypeStruct((batch, FOX_HEADS, seq), F32),
        grid=(batch,),
        in_specs=[pl.BlockSpec((seq, d), lambda b: (b, 0)),
                  pl.BlockSpec((d, LANES), lambda b: (0, 0)),
                  pl.BlockSpec((FOX_HEADS, 1), lambda b: (0, 0))],
        out_specs=pl.BlockSpec((1, FOX_HEADS, seq), lambda b: (b, 0, 0)),
        compiler_params=_params(("parallel",)),
        name="fox_gate",
    )(x, w_f, b_f)


def _fox_attn_body(q_ref, k_ref, v_ref, c_ref, o_ref, kh_ref, qh_ref, vt_ref, acc_ref, ml_ref, ot_ref,
                   *, seq, t):
    nq = seq // t
    hd = FOX_HEAD_DIM
    lane = lax.broadcasted_iota(jnp.int32, (1, LANES), 1)
    valid = _tri(t, lower=False)
    scale = hd ** -0.5
    for h in range(2):
        own = (lane >= h * hd) & (lane < (h + 1) * hd)
        aug = (1 - h) * hd
        c_col = c_ref[0, 0, :, h:h + 1]
        hi = c_col.astype(BF16).astype(F32)
        r1 = c_col - hi
        mid = r1.astype(BF16).astype(F32)
        lo = (r1 - mid).astype(BF16).astype(F32)
        sel = [(lane == aug + n).astype(F32) for n in range(3)]
        c_lanes = hi * sel[0] + mid * sel[1] + lo * sel[2]
        own_f = own.astype(F32)
        kh_ref[h] = (k_ref[...].astype(F32) * (scale * own_f) + c_lanes).astype(BF16)
        qh_ref[h] = (q_ref[...].astype(F32) * own_f - (sel[0] + sel[1] + sel[2])).astype(BF16)
    vt_ref[...] = v_ref[...].astype(F32).T.astype(BF16)

    def scores(tile):
        j, i, h = tile
        s = _dot_nt(kh_ref[h, j * t:(j + 1) * t, :], qh_ref[h, i * t:(i + 1) * t, :])
        return jnp.where(valid, s, -jnp.inf) if i == j else s

    def update(tile, s):
        j, i, h = tile
        rows = slice(h * hd, (h + 1) * hd)
        m_new = jnp.max(s, axis=0, keepdims=True)
        if j > 0:
            m_old = ml_ref[i, h:h + 1, :]
            m_new = jnp.maximum(m_old, m_new)
        p = jnp.exp(s - m_new)
        l_new = jnp.sum(p, axis=0, keepdims=True)
        acc = _dot(vt_ref[rows, j * t:(j + 1) * t], p.astype(BF16))
        if j > 0:
            a = jnp.exp(m_old - m_new)
            l_new = a * ml_ref[i, 2 + h:3 + h, :] + l_new
            acc = a * acc_ref[i, rows, :] + acc
        if i == j:
            ot_ref[rows, i * t:(i + 1) * t] = acc / l_new
        else:
            ml_ref[i, h:h + 1, :] = m_new
            ml_ref[i, 2 + h:3 + h, :] = l_new
            acc_ref[i, rows, :] = acc

    tiles = [(j, i, h) for j in range(nq) for i in range(j, nq) for h in range(2)]
    ahead = 2
    pending = []
    for idx in range(len(tiles) + ahead):
        if idx < len(tiles):
            pending.append(scores(tiles[idx]))
        if idx >= ahead:
            update(tiles[idx - ahead], pending.pop(0))
    o_ref[...] = ot_ref[...].T.astype(o_ref.dtype)


def _fox_attn(qkv, c, batch, seq, t=FOX_TILE):
    t = min(t, seq)
    pairs = FOX_HEADS // 2
    c_cols = c.reshape(batch, pairs, 2, seq).transpose(0, 1, 3, 2)
    return pl.pallas_call(
        functools.partial(_fox_attn_body, seq=seq, t=t),
        out_shape=jax.ShapeDtypeStruct((batch * seq, D_MODEL), BF16),
        grid=(batch, pairs),
        in_specs=[pl.BlockSpec((seq, LANES), lambda b, p: (b, p)),
                  pl.BlockSpec((seq, LANES), lambda b, p: (b, pairs + p)),
                  pl.BlockSpec((seq, LANES), lambda b, p: (b, 2 * pairs + p)),
                  pl.BlockSpec((1, 1, seq, 2), lambda b, p: (b, p, 0, 0))],
        out_specs=pl.BlockSpec((seq, LANES), lambda b, p: (b, p)),
        scratch_shapes=[pltpu.VMEM((2, seq, LANES), BF16),
                        pltpu.VMEM((2, seq, LANES), BF16),
                        pltpu.VMEM((LANES, seq), BF16),
                        pltpu.VMEM((seq // t, LANES, t), F32),
                        pltpu.VMEM((seq // t, SUBLANES, t), F32),
                        pltpu.VMEM((LANES, seq), F32)],
        compiler_params=_params(("parallel", "parallel")),
        name="fox_attn",
    )(qkv, qkv, qkv, c_cols)


def _fox_layer(xf, xb, w_in, b_f, w_out, ln_g, ln_b, batch, seq):
    d = D_MODEL
    qkv = _matmul(xb, w_in[:, :3 * d].astype(BF16), BF16, "fox_qkv")
    w_f = jnp.pad(w_in[:, 3 * d:], ((0, 0), (0, LANES - FOX_HEADS)))
    c = _fox_gate(xf, w_f, b_f.reshape(FOX_HEADS, 1), batch, seq)
    o = _fox_attn(qkv, c, batch, seq)
    return _proj_ln(o, w_out.astype(BF16), xf, ln_g, ln_b, "fox_out_ln")


def _gla_body(q_ref, k_ref, v_ref, r_ref, gl_ref, wgu_ref, bg_ref, ng_ref, o_ref, bc_ref, st_ref, *, seq):
    ch = GLA_CHUNK
    tile = min(SUB_TILE, seq)
    r_i = lax.broadcasted_iota(jnp.int32, (tile, tile), 0)
    c_i = lax.broadcasted_iota(jnp.int32, (tile, tile), 1)
    same_chunk_lower = ((r_i >= c_i) & (r_i // ch == c_i // ch)).astype(F32)
    for t0 in range(0, seq, tile):
        gate = _dot(gl_ref[t0:t0 + tile, :], wgu_ref[...], HI) + bg_ref[...]
        log_a = _log_sigmoid(gate) * (1.0 / GLA_GATE_TAU)
        bc_ref[t0:t0 + tile, :] = _dot(same_chunk_lower, log_a, HI)
    lower = _tri(ch, lower=True)
    st_ref[...] = jnp.zeros_like(st_ref)
    heads = range(GLA_HEADS)

    def chunk(i, carry):
        rows = pl.ds(pl.multiple_of(i * ch, ch), ch)
        bcum = bc_ref[rows, :]
        b_last = bcum[ch - 1:ch, :]
        q = q_ref[rows, :].astype(F32) * (GLA_HK ** -0.5)
        k = k_ref[rows, :].astype(F32)
        q_dec = (q * jnp.exp(bcum)).astype(BF16)
        k_inv = (k * jnp.exp(-bcum)).astype(BF16)
        k_end = (k * jnp.exp(b_last - bcum)).astype(BF16)
        decay = jnp.exp(b_last)
        kl = [slice(h * GLA_HK, (h + 1) * GLA_HK) for h in heads]
        vl = [slice(h * GLA_HV, (h + 1) * GLA_HV) for h in heads]
        v = [v_ref[rows, vl[h]] for h in heads]
        att = [jnp.where(lower, _dot_nt(q_dec[:, kl[h]], k_inv[:, kl[h]]), 0.0).astype(BF16)
               for h in heads]
        state_t = [st_ref[h] for h in heads]
        o_inter = [_dot_nt(q_dec[:, kl[h]], state_t[h].astype(BF16)) for h in heads]
        o_intra = [_dot(att[h], v[h]) for h in heads]
        kv_t = [_dot_tn(v[h], k_end[:, kl[h]]) for h in heads]
        for h in heads:
            st_ref[h] = state_t[h] * decay[:, kl[h]] + kv_t[h]
            o = o_intra[h] + o_inter[h]
            ms = jnp.mean(o * o, axis=-1, keepdims=True)
            on = o * lax.rsqrt(ms + RMS_EPS) * ng_ref[...]
            o_ref[rows, vl[h]] = (on * _silu(r_ref[rows, vl[h]].astype(F32))).astype(o_ref.dtype)
        return carry

    lax.fori_loop(0, seq // ch, chunk, 0)


def _gla_mix(proj, g_low, w_gu, b_gate, norm_g, batch, seq):
    return pl.pallas_call(
        functools.partial(_gla_body, seq=seq),
        out_shape=jax.ShapeDtypeStruct((batch * seq, GLA_DV), BF16),
        grid=(batch,),
        in_specs=[pl.BlockSpec((seq, GLA_DK), lambda b: (b, 0)),
                  pl.BlockSpec((seq, GLA_DK), lambda b: (b, 1)),
                  pl.BlockSpec((seq, GLA_DV), lambda b: (b, 1)),
                  pl.BlockSpec((seq, GLA_DV), lambda b: (b, 2)),
                  pl.BlockSpec((seq, LANES), lambda b: (b, 0)),
                  pl.BlockSpec((LANES, GLA_DK), lambda b: (0, 0)),
                  pl.BlockSpec((1, GLA_DK), lambda b: (0, 0)),
                  pl.BlockSpec((1, GLA_HV), lambda b: (0, 0))],
        out_specs=pl.BlockSpec((seq, GLA_DV), lambda b: (b, 0)),
        scratch_shapes=[pltpu.VMEM((seq, GLA_DK), F32),
                        pltpu.VMEM((GLA_HEADS, GLA_HV, GLA_HK), F32)],
        compiler_params=_params(("parallel",)),
        name="gla_mix",
    )(proj, proj, proj, proj, g_low, w_gu, b_gate.reshape(1, GLA_DK), norm_g.reshape(1, GLA_HV))


def _gla_layer(xf, xb, w_in, w_gate_up, b_gate, norm_g, w_out, ln_g, ln_b, batch, seq):
    n_main = 2 * GLA_DK + 2 * GLA_DV
    proj = _matmul(xb, w_in[:, :n_main].astype(BF16), BF16, "gla_proj")
    w_g = jnp.pad(w_in[:, n_main:], ((0, 0), (0, LANES - GLA_GATE_RANK)))
    g_low = _matmul(xf, w_g, F32, "gla_gate", tn=LANES, precision=HI)
    w_gu = jnp.pad(w_gate_up, ((0, LANES - GLA_GATE_RANK), (0, 0)))
    y = _gla_mix(proj, g_low, w_gu, b_gate, norm_g, batch, seq)
    return _proj_ln(y, w_out.astype(BF16), xf, ln_g, ln_b, "gla_out_ln")


def _ssd_gate_body(x_ref, w_ref, bias_ref, alog_ref, dt_ref, ac_ref, dtt_ref, act_ref, *, seq):
    ch = SSD_CHUNK
    raw = _dot(x_ref[...], w_ref[...], HI)
    dt = _softplus(raw + bias_ref[...])
    a = dt * (-jnp.exp(alog_ref[...]))
    lower_f = _tri(ch, lower=True).astype(F32)
    dt_ref[...] = dt
    dtt_ref[0] = dt.T
    for c in range(seq // ch):
        acum = _dot(lower_f, a[c * ch:(c + 1) * ch, :], HI)
        ac_ref[c * ch:(c + 1) * ch, :] = acum
        act_ref[0, :, c * ch:(c + 1) * ch] = acum.T


def _ssd_gate(x, w_dt, dt_bias, a_log, batch, seq):
    d = x.shape[1]
    col = jax.ShapeDtypeStruct((batch * seq, LANES), F32)
    row = jax.ShapeDtypeStruct((batch, LANES, seq), F32)
    return pl.pallas_call(
        functools.partial(_ssd_gate_body, seq=seq),
        out_shape=(col, col, row, row),
        grid=(batch,),
        in_specs=[pl.BlockSpec((seq, d), lambda b: (b, 0)),
                  pl.BlockSpec((d, LANES), lambda b: (0, 0)),
                  pl.BlockSpec((1, LANES), lambda b: (0, 0)),
                  pl.BlockSpec((1, LANES), lambda b: (0, 0))],
        out_specs=(pl.BlockSpec((seq, LANES), lambda b: (b, 0)),
                   pl.BlockSpec((seq, LANES), lambda b: (b, 0)),
                   pl.BlockSpec((1, LANES, seq), lambda b: (b, 0, 0)),
                   pl.BlockSpec((1, LANES, seq), lambda b: (b, 0, 0))),
        compiler_params=_params(("parallel",)),
        name="ssd_gate",
    )(x, w_dt, dt_bias, a_log)


def _expand_heads(cols, width):
    rows = cols.shape[0]
    lane = lax.broadcasted_iota(jnp.int32, (rows, SSD_HG * width), 1)
    out = jnp.broadcast_to(cols[:, SSD_HG - 1:SSD_HG], (rows, SSD_HG * width))
    for h in range(SSD_HG - 2, -1, -1):
        out = jnp.where(lane < (h + 1) * width, jnp.broadcast_to(cols[:, h:h + 1], out.shape), out)
    return out


def _ssd_body(z_ref, xr_ref, br_ref, cr_ref, cwx_ref, cwb_ref, cwc_ref, cbx_ref, cbb_ref, cbc_ref,
              dt4_ref, ac4_ref, dtt_ref, act_ref, dskip_ref, ng_ref, o_ref,
              padx_ref, padb_ref, xs_ref, bm_ref, cm_ref, st_ref, *, seq):
    ch = SSD_CHUNK
    hp = SSD_HG * SSD_P
    n = SSD_STATE
    pad = SUBLANES

    def conv(raw_ref, pad_ref, w_ref, b_ref, dst_ref):
        pad_ref[0:pad, :] = jnp.zeros((pad, pad_ref.shape[1]), F32)
        pad_ref[pad:pad + seq, :] = raw_ref[...].astype(F32)
        tile = min(SUB_TILE, seq)
        width = min(SUB_TILE, pad_ref.shape[1])
        for c0 in range(0, pad_ref.shape[1], width):
            cols = slice(c0, c0 + width)
            for t0 in range(0, seq, tile):
                acc = jnp.broadcast_to(b_ref[:, cols], (tile, width))
                for j in range(SSD_CONV):
                    off = t0 + pad - (SSD_CONV - 1) + j
                    acc = acc + w_ref[j:j + 1, cols] * pad_ref[off:off + tile, cols]
                dst_ref[t0:t0 + tile, cols] = _silu(acc)

    conv(xr_ref, padx_ref, cwx_ref, cbx_ref, xs_ref)
    conv(br_ref, padb_ref, cwb_ref, cbb_ref, bm_ref)
    conv(cr_ref, padb_ref, cwc_ref, cbc_ref, cm_ref)

    lower = _tri(ch, lower=True)
    lane = lax.broadcasted_iota(jnp.int32, (1, hp), 1)
    st_ref[...] = jnp.zeros_like(st_ref)

    def group_chunk(gi, c, rows):
        cols = slice(gi * hp, (gi + 1) * hp)
        ncols = slice(gi * n, (gi + 1) * n)
        xs = xs_ref[rows, cols]
        bm = bm_ref[rows, ncols].astype(BF16)
        cm = cm_ref[rows, ncols].astype(BF16)
        dt4 = dt4_ref[0, gi, rows, :]
        ac4 = ac4_ref[0, gi, rows, :]
        cb = _dot_nt(cm, bm)
        yield
        state = st_ref[gi]
        y_off = _dot(cm, state.astype(BF16))
        yield
        to_end = jnp.exp(ac4[ch - 1:ch, :] - ac4)
        w_state = _expand_heads(dt4 * to_end, SSD_P)
        e_acum = _expand_heads(jnp.exp(ac4), SSD_P)
        new_state = _dot_tn(bm, (xs * w_state).astype(BF16))
        yield
        st_ref[gi] = state * e_acum[ch - 1:ch, :] + new_state
        xs_b = xs.astype(BF16)
        y = xs * dskip_ref[:, cols] + y_off * e_acum
        for h in range(SSD_HG):
            a_col = jnp.broadcast_to(ac4[:, h:h + 1], (ch, ch))
            a_row = act_ref[0, gi, h, pl.ds(c, 1), :]
            d_row = dtt_ref[0, gi, h, pl.ds(c, 1), :]
            decay = jnp.exp(jnp.where(lower, a_col - a_row, -jnp.inf))
            m_h = (cb * decay * d_row).astype(BF16)
            head = (lane >= h * SSD_P) & (lane < (h + 1) * SSD_P)
            y = y + _dot(m_h, jnp.where(head, xs_b, jnp.zeros_like(xs_b)))
            yield
        y = y * _silu(z_ref[rows, cols].astype(F32))
        ms = jnp.mean(y * y, axis=-1, keepdims=True)
        o_ref[rows, cols] = (y * lax.rsqrt(ms + RMS_EPS) * ng_ref[:, cols]).astype(o_ref.dtype)

    def chunk(c, carry):
        rows = pl.ds(pl.multiple_of(c * ch, ch), ch)
        running = [group_chunk(gi, c, rows) for gi in range(SSD_GROUPS_PER_STEP)]
        while running:
            for gen in list(running):
                if next(gen, "done") == "done":
                    running.remove(gen)
        return carry

    lax.fori_loop(0, seq // ch, chunk, 0)


def _ssd_mix(proj, conv_w, conv_b, dt4, ac4, dtt, act, d_exp, norm_g, batch, seq):
    per = SSD_GROUPS_PER_STEP
    steps = SSD_GROUPS // per
    hp = per * SSD_HG * SSD_P
    n = per * SSD_STATE
    nz = SSD_D_INNER // hp
    nb = 2 * SSD_D_INNER // n
    nc = nb + steps
    cb0 = SSD_D_INNER // n
    nch = seq // SSD_CHUNK
    return pl.pallas_call(
        functools.partial(_ssd_body, seq=seq),
        out_shape=jax.ShapeDtypeStruct((batch * seq, SSD_D_INNER), BF16),
        grid=(batch, steps),
        in_specs=[pl.BlockSpec((seq, hp), lambda b, i: (b, i)),
                  pl.BlockSpec((seq, hp), lambda b, i: (b, nz + i)),
                  pl.BlockSpec((seq, n), lambda b, i: (b, nb + i)),
                  pl.BlockSpec((seq, n), lambda b, i: (b, nc + i)),
                  pl.BlockSpec((SSD_CONV, hp), lambda b, i: (0, i)),
                  pl.BlockSpec((SSD_CONV, n), lambda b, i: (0, cb0 + i)),
                  pl.BlockSpec((SSD_CONV, n), lambda b, i: (0, cb0 + steps + i)),
                  pl.BlockSpec((1, hp), lambda b, i: (0, i)),
                  pl.BlockSpec((1, n), lambda b, i: (0, cb0 + i)),
                  pl.BlockSpec((1, n), lambda b, i: (0, cb0 + steps + i)),
                  pl.BlockSpec((1, per, seq, SSD_HG), lambda b, i: (b, i, 0, 0)),
                  pl.BlockSpec((1, per, seq, SSD_HG), lambda b, i: (b, i, 0, 0)),
                  pl.BlockSpec((1, per, SSD_HG, nch, SSD_CHUNK), lambda b, i: (b, i, 0, 0, 0)),
                  pl.BlockSpec((1, per, SSD_HG, nch, SSD_CHUNK), lambda b, i: (b, i, 0, 0, 0)),
                  pl.BlockSpec((1, hp), lambda b, i: (0, i)),
                  pl.BlockSpec((1, hp), lambda b, i: (0, i))],
        out_specs=pl.BlockSpec((seq, hp), lambda b, i: (b, i)),
        scratch_shapes=[pltpu.VMEM((seq + SUBLANES, hp), F32),
                        pltpu.VMEM((seq + SUBLANES, n), F32),
                        pltpu.VMEM((seq, hp), F32),
                        pltpu.VMEM((seq, n), F32),
                        pltpu.VMEM((seq, n), F32),
                        pltpu.VMEM((per, SSD_STATE, SSD_HG * SSD_P), F32)],
        compiler_params=_params(("parallel", "parallel")),
        name="ssd_mix",
    )(proj, proj, proj, proj, conv_w, conv_w, conv_w, conv_b, conv_b, conv_b,
      dt4, ac4, dtt, act, d_exp, norm_g)


def _ssd_layer(xf, xb, w_in, conv_w, conv_b, dt_bias, a_log, d_skip, norm_g, w_out, ln_g, ln_b,
               batch, seq):
    n_main = SSD_D_INNER + (SSD_D_INNER + 2 * SSD_GROUPS * SSD_STATE)
    proj = _matmul(xb, w_in[:, :n_main].astype(BF16), BF16, "ssd_proj")
    lane_pad = ((0, 0), (0, LANES - SSD_HEADS))
    w_dt = jnp.pad(w_in[:, n_main:], lane_pad)
    dt_c, ac_c, dt_r, ac_r = _ssd_gate(xf, w_dt, jnp.pad(dt_bias.reshape(1, -1), lane_pad),
                                       jnp.pad(a_log.reshape(1, -1), lane_pad), batch, seq)

    def cols(t):
        return t[:, :SSD_HEADS].reshape(batch, seq, SSD_GROUPS, SSD_HG).transpose(0, 2, 1, 3)

    def rows(t):
        return t[:, :SSD_HEADS].reshape(batch, SSD_GROUPS, SSD_HG, seq // SSD_CHUNK, SSD_CHUNK)

    d_exp = jnp.repeat(d_skip, SSD_P).reshape(1, SSD_D_INNER)
    y = _ssd_mix(proj, conv_w, conv_b.reshape(1, -1), cols(dt_c), cols(ac_c), rows(dt_r), rows(ac_r),
                 d_exp, norm_g.reshape(1, SSD_D_INNER), batch, seq)
    return _proj_ln(y, w_out.astype(BF16), xf, ln_g, ln_b, "ssd_out_ln")


def _router_body(x_ref, w_ref, bias_ref, e_ref, w_out_ref):
    tm = x_ref.shape[0]
    gsz = N_EXPERTS // N_EXPERT_GROUPS
    logits = _dot(x_ref[...], w_ref[...], HI).T[:N_EXPERTS, :]
    scores = _sigmoid(logits)
    sel = scores + bias_ref[...]
    sub = lax.broadcasted_iota(jnp.int32, (gsz, tm), 0).astype(F32)
    neg = jnp.float32(-jnp.inf)

    def top1(v, idx, sentinel):
        m = jnp.max(v, axis=0, keepdims=True)
        i = jnp.min(jnp.where(v == m, idx, sentinel), axis=0, keepdims=True)
        return m, i

    s_g = [sel[g * gsz:(g + 1) * gsz, :] for g in range(N_EXPERT_GROUPS)]
    sc_g = [scores[g * gsz:(g + 1) * gsz, :] for g in range(N_EXPERT_GROUPS)]
    grp_rows = []
    for g in range(N_EXPERT_GROUPS):
        m1, i1 = top1(s_g[g], sub, float(gsz))
        m2 = jnp.max(jnp.where(sub == i1, neg, s_g[g]), axis=0, keepdims=True)
        grp_rows.append(m1 + m2)
    grp = jnp.concatenate(grp_rows, axis=0)
    keep = jnp.zeros((N_EXPERT_GROUPS, tm), F32)
    for _ in range(TOPK_GROUPS):
        _, ig = top1(grp, sub, float(N_EXPERT_GROUPS))
        hit = sub == ig
        keep = jnp.where(hit, 1.0, keep)
        grp = jnp.where(hit, neg, grp)
    cand = [jnp.where(keep[g:g + 1, :] > 0.5, s_g[g], neg) for g in range(N_EXPERT_GROUPS)]
    ids = [sub + float(g * gsz) for g in range(N_EXPERT_GROUPS)]
    e_rows, w_rows = [], []
    for _ in range(TOP_K):
        m = cand[0]
        for g in range(1, N_EXPERT_GROUPS):
            m = jnp.maximum(m, cand[g])
        m = jnp.max(m, axis=0, keepdims=True)
        first = jnp.where(cand[0] == m, ids[0], float(N_EXPERTS))
        for g in range(1, N_EXPERT_GROUPS):
            first = jnp.minimum(first, jnp.where(cand[g] == m, ids[g], float(N_EXPERTS)))
        first = jnp.min(first, axis=0, keepdims=True)
        wsum = jnp.zeros((gsz, tm), F32)
        for g in range(N_EXPERT_GROUPS):
            hit = ids[g] == first
            wsum = wsum + jnp.where(hit, sc_g[g], 0.0)
            cand[g] = jnp.where(hit, neg, cand[g])
        e_rows.append(first)
        w_rows.append(jnp.sum(wsum, axis=0, keepdims=True))
    top_e = jnp.concatenate(e_rows, axis=0)
    top_w = jnp.concatenate(w_rows, axis=0)
    denom = jnp.sum(top_w, axis=0, keepdims=True)
    e_ref[...] = top_e.astype(jnp.int32)
    w_out_ref[...] = top_w / denom * ROUTED_SCALE


def _router(x, w_router, bias, tm=ROW_BLOCK):
    t, d = x.shape
    tm = min(tm, t)
    return pl.pallas_call(
        _router_body,
        out_shape=(jax.ShapeDtypeStruct((TOP_K, t), jnp.int32),
                   jax.ShapeDtypeStruct((TOP_K, t), F32)),
        grid=(t // tm,),
        in_specs=[pl.BlockSpec((tm, d), lambda i: (i, 0)),
                  pl.BlockSpec((d, LANES), lambda i: (0, 0)),
                  pl.BlockSpec((N_EXPERTS, 1), lambda i: (0, 0))],
        out_specs=(pl.BlockSpec((TOP_K, tm), lambda i: (0, i)),
                   pl.BlockSpec((TOP_K, tm), lambda i: (0, i))),
        compiler_params=_params(("parallel",)),
        name="moe_router",
    )(x, jnp.pad(w_router, ((0, 0), (0, LANES - N_EXPERTS))), bias.reshape(N_EXPERTS, 1))


def _moe_list_len(tb):
    chunks = MOE_LEAD_CHUNKS + -(-tb * TOP_K // MOE_ROWS) + N_EXPERTS + N_EXPERTS // MOE_EXPERTS_PER_STEP + 2
    return -(-chunks * MOE_ROWS // MOE_LIST_ALIGN) * MOE_LIST_ALIGN


def _moe_body(sstart_ref, scount_ref, cexp_ref, gt_hbm, st_hbm,
              rw_ref, x8_ref, wg_ref, wu_ref, wd_ref, sg_ref, su_ref, sd_ref,
              out_ref, gt_ref, st_ref, buf_ref, y2_ref, sem, *, tb):
    blk = pl.program_id(0)
    grp = pl.program_id(1)
    rows_per = MOE_ROWS
    tile = min(SUB_TILE, tb)
    n_list = st_ref.shape[0]
    n_chunks = n_list // rows_per
    acc_ref = out_ref.at[0]

    def tile_at(ref, off):
        return ref.at[pl.ds(pl.multiple_of(off, SUBLANES), SUBLANES), :]

    def gather(chunk, parity):
        src = gt_ref.at[pl.ds((chunk + MOE_LEAD_CHUNKS) * rows_per, rows_per)]
        dst = buf_ref.at[parity]
        for r in range(rows_per):
            tile_at(dst, r * SUBLANES)[...] = tile_at(x8_ref, src[r])[...]

    def scatter(chunk, parity):
        y_ref = y2_ref.at[parity]
        dst = st_ref.at[pl.ds((chunk + MOE_LEAD_CHUNKS) * rows_per, rows_per)]
        for r0 in range(0, rows_per, MOE_SCATTER_GROUP):
            offs = [dst[r0 + u] for u in range(MOE_SCATTER_GROUP)]
            vals = [tile_at(acc_ref, offs[u])[...] + tile_at(y_ref, (r0 + u) * SUBLANES)[...]
                    for u in range(MOE_SCATTER_GROUP)]
            for u in range(MOE_SCATTER_GROUP):
                tile_at(acc_ref, offs[u])[...] = vals[u]

    def row_weights(chunk):
        w_row = rw_ref[0, pl.ds(chunk + MOE_LEAD_CHUNKS, 1), :]
        return jnp.broadcast_to(w_row, (rows_per, rows_per)).T

    @pl.when(grp == 0)
    def _start_block():
        copies = [pltpu.make_async_copy(src.at[pl.ds(pl.multiple_of(blk * n_list, MOE_LIST_ALIGN), n_list)],
                                        dst, sem.at[n])
                  for n, (src, dst) in enumerate(((gt_hbm, gt_ref), (st_hbm, st_ref)))]
        for cp in copies:
            cp.start()
        for t0 in range(0, tb, tile):
            xb = _tok_rows(x8_ref, t0, tile).astype(BF16)
            h = _silu(_dot(xb, sg_ref[...])) * _dot(xb, su_ref[...])
            _tok_store(acc_ref, t0, _dot(h.astype(BF16), sd_ref[...]))
        tile_at(acc_ref, tb * SUBLANES)[...] = jnp.zeros((SUBLANES, LANES), F32)
        y2_ref[...] = jnp.zeros_like(y2_ref)
        for cp in copies:
            cp.wait()
        gather(0, 0)
        gather(1, 1)

    step_id = blk * pl.num_programs(1) + grp
    first = sstart_ref[step_id]
    pairs = scount_ref[step_id] // 2

    def step(i, carry):
        g = first + 2 * i
        xb = [_tok_rows(buf_ref.at[p], 0, rows_per).astype(BF16) for p in range(2)]
        e = [cexp_ref[blk * n_chunks + g + p] for p in range(2)]
        for p in range(2):
            scatter(g - 2 + p, p)
        for p in range(2):
            gather(g + 2 + p, p)
        gate = [_dot(xb[p], wg_ref[e[p]]) for p in range(2)]
        up = [_dot(xb[p], wu_ref[e[p]]) for p in range(2)]
        h = [(_silu(gate[p]) * up[p]).astype(BF16) for p in range(2)]
        y = [_dot(h[p], wd_ref[e[p]]) for p in range(2)]
        for p in range(2):
            w_rows = jnp.concatenate([row_weights(g + p)] * (y[p].shape[1] // rows_per), axis=1)
            _tok_store(y2_ref.at[p], 0, y[p] * w_rows)
        return carry

    lax.fori_loop(0, pairs, step, 0)

    @pl.when(grp == pl.num_programs(1) - 1)
    def _finish_block():
        last = first + 2 * pairs
        for p in range(2):
            scatter(last - 2 + p, p)


def _moe_experts(x8, sstart, scount, cexp, gt, st, rw, wg, wu, wd, layer, sg, su, sd, tb):
    d = wg.shape[2]
    ff = wg.shape[3]
    nblk = x8.shape[0] // (tb * SUBLANES)
    n_list = _moe_list_len(tb)
    per_step = MOE_EXPERTS_PER_STEP
    grid_spec = pltpu.PrefetchScalarGridSpec(
        num_scalar_prefetch=3,
        grid=(nblk, N_EXPERTS // per_step),
        in_specs=[pl.BlockSpec(memory_space=pl.ANY),
                  pl.BlockSpec(memory_space=pl.ANY),
                  pl.BlockSpec((1, n_list // MOE_ROWS, MOE_ROWS), lambda i, e, *_: (i, 0, 0)),
                  pl.BlockSpec((tb * SUBLANES, LANES), lambda i, e, *_: (i, 0)),
                  pl.BlockSpec((None, per_step, d, ff), lambda i, e, *_: (layer, e, 0, 0)),
                  pl.BlockSpec((None, per_step, d, ff), lambda i, e, *_: (layer, e, 0, 0)),
                  pl.BlockSpec((None, per_step, ff, d), lambda i, e, *_: (layer, e, 0, 0)),
                  pl.BlockSpec((d, ff), lambda i, e, *_: (0, 0)),
                  pl.BlockSpec((d, ff), lambda i, e, *_: (0, 0)),
                  pl.BlockSpec((ff, d), lambda i, e, *_: (0, 0))],
        out_specs=pl.BlockSpec((1, (tb + 1) * SUBLANES, LANES), lambda i, e, *_: (i, 0, 0)),
        scratch_shapes=[pltpu.SMEM((n_list,), jnp.int32),
                        pltpu.SMEM((n_list,), jnp.int32),
                        pltpu.VMEM((2, MOE_ROWS * SUBLANES, LANES), F32),
                        pltpu.VMEM((2, MOE_ROWS * SUBLANES, LANES), F32),
                        pltpu.SemaphoreType.DMA((2,))],
    )
    return pl.pallas_call(
        functools.partial(_moe_body, tb=tb),
        out_shape=jax.ShapeDtypeStruct((nblk, (tb + 1) * SUBLANES, LANES), F32),
        grid_spec=grid_spec,
        compiler_params=_params(("arbitrary", "arbitrary")),
        name="moe_experts",
    )(sstart, scount, cexp, gt, st, rw, x8, wg, wu, wd, sg, su, sd)


def _moe_finish_body(x_ref, acc_ref, g_ref, b_ref, xf_ref, xb_ref):
    z = DN_ALPHA * x_ref[...] + _tok_rows(acc_ref.at[0], 0, x_ref.shape[0])
    out = _layer_norm_rows(z, g_ref[...], b_ref[...])
    xf_ref[...] = out
    xb_ref[...] = out.astype(BF16)


def _moe_finish(x, acc8, g, b, tb, tm=ROW_BLOCK):
    m, d = x.shape
    tm = min(tm, tb)
    per_blk = tb // tm
    return pl.pallas_call(
        _moe_finish_body,
        out_shape=(jax.ShapeDtypeStruct((m, d), F32), jax.ShapeDtypeStruct((m, d), BF16)),
        grid=(m // tm,),
        in_specs=[pl.BlockSpec((tm, d), lambda i: (i, 0)),
                  pl.BlockSpec((1, tm * SUBLANES, LANES), lambda i: (i // per_blk, i % per_blk, 0)),
                  pl.BlockSpec((1, d), lambda i: (0, 0)),
                  pl.BlockSpec((1, d), lambda i: (0, 0))],
        out_specs=(pl.BlockSpec((tm, d), lambda i: (i, 0)),
                   pl.BlockSpec((tm, d), lambda i: (i, 0))),
        compiler_params=_params(("parallel",)),
        name="moe_finish",
    )(x, acc8, g.reshape(1, d), b.reshape(1, d))


def _moe_layer(xf, x8, w_router, router_bias, wg, wu, wd, layer, ws_gate, ws_up, ws_down,
               ln_g, ln_b, tb):
    t, d = xf.shape
    top_e, top_w = _router(xf, w_router, router_bias)
    nblk = t // tb
    per = tb * TOP_K
    e_flat = top_e.T.reshape(nblk, per)
    w_flat = top_w.T.reshape(nblk, per)
    assert N_EXPERTS * per < 2 ** 31
    packed = jnp.sort(e_flat * per + jnp.arange(per, dtype=jnp.int32), axis=-1)
    order = packed % per
    cnt = jnp.sum(e_flat[:, :, None] == jnp.arange(N_EXPERTS, dtype=jnp.int32), axis=1,
                  dtype=jnp.int32)
    nch = (cnt + MOE_ROWS - 1) // MOE_ROWS
    groups = N_EXPERTS // MOE_EXPERTS_PER_STEP
    gpad = jnp.sum(nch.reshape(nblk, groups, MOE_EXPERTS_PER_STEP), axis=-1) % 2
    pads_before = jnp.cumsum(gpad, axis=-1, dtype=jnp.int32) - gpad
    cend = (jnp.cumsum(nch, axis=-1, dtype=jnp.int32)
            + jnp.repeat(pads_before, MOE_EXPERTS_PER_STEP, axis=-1))
    n_list = _moe_list_len(tb)
    n_chunks = n_list // MOE_ROWS
    chunk = jnp.arange(n_chunks, dtype=jnp.int32) - MOE_LEAD_CHUNKS
    before = cend[:, None, :] <= chunk[None, :, None]
    exp_of = jnp.sum(before, axis=-1, dtype=jnp.int32)
    earlier = jnp.arange(groups, dtype=jnp.int32) < (exp_of // MOE_EXPERTS_PER_STEP)[..., None]
    first_chunk = (jnp.sum(jnp.where(before, nch[:, None, :], 0), axis=-1)
                   + jnp.sum(jnp.where(earlier, gpad[:, None, :], 0), axis=-1))
    first_src = jnp.sum(jnp.where(before, cnt[:, None, :], 0), axis=-1)
    own = jnp.arange(N_EXPERTS, dtype=jnp.int32) == exp_of[..., None]
    n_rows = jnp.sum(jnp.where(own, cnt[:, None, :], 0), axis=-1)
    row = ((chunk[None, :] - first_chunk) * MOE_ROWS)[..., None] + jnp.arange(MOE_ROWS, dtype=jnp.int32)
    valid = ((chunk[None, :, None] >= 0) & (row >= 0) & (row < n_rows[..., None])).reshape(nblk, n_list)
    src = lax.optimization_barrier(
        jnp.clip(first_src[..., None] + row, 0, per - 1).reshape(nblk, n_list))
    picked = lax.optimization_barrier(jnp.take_along_axis(order, src, axis=-1))
    tile_off = (picked // TOP_K) * SUBLANES
    gt = jnp.where(valid, tile_off, 0)
    st = jnp.where(valid, tile_off, tb * SUBLANES)
    rw_pad = jnp.where(valid, jnp.take_along_axis(w_flat, picked, axis=-1), 0.0)
    gend = cend.reshape(nblk, groups, MOE_EXPERTS_PER_STEP)[:, :, -1] + gpad
    gstart = jnp.concatenate([jnp.zeros((nblk, 1), jnp.int32), gend[:, :-1]], axis=1)
    cexp = jnp.minimum(exp_of, N_EXPERTS - 1) % MOE_EXPERTS_PER_STEP
    cexp = jnp.roll(cexp, -MOE_LEAD_CHUNKS, axis=1)
    sg, su, sd = ws_gate.astype(BF16), ws_up.astype(BF16), ws_down.astype(BF16)
    acc8 = _moe_experts(x8, gstart.reshape(-1), (gend - gstart).reshape(-1), cexp.reshape(-1),
                        gt.reshape(-1), st.reshape(-1), rw_pad.reshape(nblk, n_chunks, MOE_ROWS),
                        wg, wu, wd, layer, sg, su, sd, tb)
    return _moe_finish(xf, acc8, ln_g, ln_b, tb)


def kernel(x, fox_w_in, fox_b_f, fox_w_out, gla_w_in, gla_w_gate_up, gla_b_gate, gla_norm, gla_w_out, ssd_w_in, ssd_conv_w, ssd_conv_b, ssd_dt_bias, ssd_a_log, ssd_d, ssd_norm, ssd_w_out, ln1_g, ln1_b, moe_router, moe_router_bias, moe_w_gate, moe_w_up, moe_w_down, moe_ws_gate, moe_ws_up, moe_ws_down, ln2_g, ln2_b):
    batch, seq, d = x.shape
    xf = x.reshape(batch * seq, d)
    xb = xf.astype(BF16)
    tb = seq
    wg_all, wu_all, wd_all = moe_w_gate.astype(BF16), moe_w_up.astype(BF16), moe_w_down.astype(BF16)
    for i in range(DEPTH):
        kind, j = i % N_MIXERS, i // N_MIXERS
        if kind == 0:
            xf, xb, x8 = _fox_layer(xf, xb, fox_w_in[j], fox_b_f[j], fox_w_out[j], ln1_g[i], ln1_b[i],
                                    batch, seq)
        elif kind == 1:
            xf, xb, x8 = _gla_layer(xf, xb, gla_w_in[j], gla_w_gate_up[j], gla_b_gate[j], gla_norm[j],
                                    gla_w_out[j], ln1_g[i], ln1_b[i], batch, seq)
        else:
            xf, xb, x8 = _ssd_layer(xf, xb, ssd_w_in[j], ssd_conv_w[j], ssd_conv_b[j], ssd_dt_bias[j],
                                    ssd_a_log[j], ssd_d[j], ssd_norm[j], ssd_w_out[j], ln1_g[i],
                                    ln1_b[i], batch, seq)
        xf, xb = _moe_layer(xf, x8, moe_router[i], moe_router_bias[i], wg_all, wu_all, wd_all, i,
                            moe_ws_gate[i], moe_ws_up[i], moe_ws_down[i], ln2_g[i], ln2_b[i], tb)
    return xf.reshape(batch, seq, d)
```

```python
import functools

import jax
import jax.numpy as jnp
from jax import lax
from jax.experimental import pallas as pl
from jax.experimental.pallas import tpu as pltpu

F32 = jnp.float32
BF16 = jnp.bfloat16
HI = lax.Precision.HIGHEST

D_MODEL = 1024
DEPTH = 4
N_MIXERS = 3
DN_ALPHA = (2 * DEPTH) ** 0.25
LN_EPS = 1e-5
RMS_EPS = 1e-6

FOX_HEADS = 16
FOX_HEAD_DIM = 64
GLA_HEADS = 4
GLA_DK = 512
GLA_DV = 1024
GLA_HK = 128
GLA_HV = 256
GLA_GATE_RANK = 16
GLA_GATE_TAU = 16.0
GLA_CHUNK = 64
SSD_D_INNER = 2048
SSD_HEADS = 32
SSD_GROUPS = 8
SSD_HG = 4
SSD_P = 64
SSD_STATE = 128
SSD_CONV = 4
SSD_CHUNK = 128
N_EXPERTS = 64
TOP_K = 8
N_EXPERT_GROUPS = 8
TOPK_GROUPS = 4
EXPERT_FF = 256
ROUTED_SCALE = 2.5

LANES = 128
SUBLANES = 8
VMEM_LIMIT = 56 * 2 ** 20
MOE_ROWS = 128
MOE_LIST_ALIGN = 1024
MOE_SCATTER_GROUP = 8
MOE_EXPERTS_PER_STEP = 4
MOE_LEAD_CHUNKS = 2
SSD_GROUPS_PER_STEP = 2
PROJ_TILE = 1024
ROW_BLOCK = 512
SUB_TILE = 256
FOX_TILE = 256


def _params(sem):
    return pltpu.CompilerParams(dimension_semantics=sem, vmem_limit_bytes=VMEM_LIMIT)


def _sigmoid(x):
    return 1.0 / (1.0 + jnp.exp(-x))


def _silu(x):
    return x * _sigmoid(x)


def _log_sigmoid(x):
    return jnp.minimum(x, 0.0) - jnp.log(1.0 + jnp.exp(-jnp.abs(x)))


def _softplus(x):
    return jnp.maximum(x, 0.0) + jnp.log(1.0 + jnp.exp(-jnp.abs(x)))


def _dot(a, b, precision=None):
    return jnp.dot(a, b, preferred_element_type=F32, precision=precision)


def _dot_nt(a, b, precision=None):
    return lax.dot_general(a, b, (((1,), (1,)), ((), ())), preferred_element_type=F32,
                           precision=precision)


def _dot_tn(a, b, precision=None):
    return lax.dot_general(a, b, (((0,), (0,)), ((), ())), preferred_element_type=F32,
                           precision=precision)


def _tri(n, lower):
    r = lax.broadcasted_iota(jnp.int32, (n, n), 0)
    c = lax.broadcasted_iota(jnp.int32, (n, n), 1)
    return (r >= c) if lower else (r <= c)


def _mm_body(x_ref, w_ref, o_ref, *, precision):
    o_ref[...] = _dot(x_ref[...], w_ref[...], precision).astype(o_ref.dtype)


def _matmul(x, w, out_dtype, name, tm=PROJ_TILE, tn=PROJ_TILE, precision=None):
    m, k = x.shape
    n = w.shape[1]
    tm, tn = min(tm, m), min(tn, n)
    return pl.pallas_call(
        functools.partial(_mm_body, precision=precision),
        out_shape=jax.ShapeDtypeStruct((m, n), out_dtype),
        grid=(n // tn, m // tm),
        in_specs=[pl.BlockSpec((tm, k), lambda j, i: (i, 0)),
                  pl.BlockSpec((k, tn), lambda j, i: (0, j))],
        out_specs=pl.BlockSpec((tm, tn), lambda j, i: (i, j)),
        compiler_params=_params(("parallel", "parallel")),
        name=name,
    )(x, w)


def _layer_norm_rows(z, g, b):
    mu = jnp.mean(z, axis=-1, keepdims=True)
    zc = z - mu
    var = jnp.mean(zc * zc, axis=-1, keepdims=True)
    return zc * lax.rsqrt(var + LN_EPS) * g + b


def _tok_rows(ref, t0, n):
    return jnp.concatenate(
        [ref[pl.ds(t0 * SUBLANES + k, n, stride=SUBLANES), :] for k in range(SUBLANES)], axis=1)


def _tok_store(ref, t0, val):
    for k in range(SUBLANES):
        ref[pl.ds(t0 * SUBLANES + k, val.shape[0], stride=SUBLANES), :] = val[:, k * LANES:(k + 1) * LANES]


def _proj_ln_body(o_ref, w_ref, x_ref, g_ref, b_ref, xf_ref, xb_ref, x8_ref):
    y = _dot(o_ref[...], w_ref[...])
    out = _layer_norm_rows(DN_ALPHA * x_ref[...] + y, g_ref[...], b_ref[...])
    xf_ref[...] = out
    xb_ref[...] = out.astype(BF16)
    _tok_store(x8_ref, 0, out)


def _proj_ln(o, w, x, g, b, name, tm=ROW_BLOCK):
    m, k = o.shape
    d = w.shape[1]
    assert d == SUBLANES * LANES
    tm = min(tm, m)
    return pl.pallas_call(
        _proj_ln_body,
        out_shape=(jax.ShapeDtypeStruct((m, d), F32), jax.ShapeDtypeStruct((m, d), BF16),
                   jax.ShapeDtypeStruct((m * SUBLANES, LANES), F32)),
        grid=(m // tm,),
        in_specs=[pl.BlockSpec((tm, k), lambda i: (i, 0)),
                  pl.BlockSpec((k, d), lambda i: (0, 0)),
                  pl.BlockSpec((tm, d), lambda i: (i, 0)),
                  pl.BlockSpec((1, d), lambda i: (0, 0)),
                  pl.BlockSpec((1, d), lambda i: (0, 0))],
        out_specs=(pl.BlockSpec((tm, d), lambda i: (i, 0)),
                   pl.BlockSpec((tm, d), lambda i: (i, 0)),
                   pl.BlockSpec((tm * SUBLANES, LANES), lambda i: (i, 0))),
        compiler_params=_params(("parallel",)),
        name=name,
    )(o, w, x, g.reshape(1, d), b.reshape(1, d))


def _fox_gate_body(x_ref, w_ref, b_ref, c_ref, *, seq):
    f = _dot(x_ref[...], w_ref[...], HI)
    z = f.T[:FOX_HEADS, :] + b_ref[...]
    lf = _log_sigmoid(z)
    upper = _tri(LANES, lower=False).astype(F32)
    carry = jnp.zeros((FOX_HEADS, 1), F32)
    for blk in range(seq // LANES):
        seg = _dot(lf[:, blk * LANES:(blk + 1) * LANES], upper, HI) + carry
        c_ref[0, :, blk * LANES:(blk + 1) * LANES] = seg
        carry = seg[:, LANES - 1:LANES]


def _fox_gate(x, w_f, b_f, batch, seq):
    d = x.shape[1]
    return pl.pallas_call(
        functools.partial(_fox_gate_body, seq=seq),
        out_shape=jax.ShapeDtypeStruct((batch, FOX_HEADS, seq), F32),
        grid=(batch,),
        in_specs=[pl.BlockSpec((seq, d), lambda b: (b, 0)),
                  pl.BlockSpec((d, LANES), lambda b: (0, 0)),
                  pl.BlockSpec((FOX_HEADS, 1), lambda b: (0, 0))],
        out_specs=pl.BlockSpec((1, FOX_HEADS, seq), lambda b: (b, 0, 0)),
        compiler_params=_params(("parallel",)),
        name="fox_gate",
    )(x, w_f, b_f)


def _fox_attn_body(q_ref, k_ref, v_ref, c_ref, o_ref, kh_ref, qh_ref, vt_ref, acc_ref, ml_ref, ot_ref,
                   *, seq, t):
    nq = seq // t
    hd = FOX_HEAD_DIM
    lane = lax.broadcasted_iota(jnp.int32, (1, LANES), 1)
    valid = _tri(t, lower=False)
    scale = hd ** -0.5
    for h in range(2):
        own = (lane >= h * hd) & (lane < (h + 1) * hd)
        aug = (1 - h) * hd
        c_col = c_ref[0, 0, :, h:h + 1]
        hi = c_col.astype(BF16).astype(F32)
        r1 = c_col - hi
        mid = r1.astype(BF16).astype(F32)
        lo = (r1 - mid).astype(BF16).astype(F32)
        sel = [(lane == aug + n).astype(F32) for n in range(3)]
        c_lanes = hi * sel[0] + mid * sel[1] + lo * sel[2]
        own_f = own.astype(F32)
        kh_ref[h] = (k_ref[...].astype(F32) * (scale * own_f) + c_lanes).astype(BF16)
        qh_ref[h] = (q_ref[...].astype(F32) * own_f - (sel[0] + sel[1] + sel[2])).astype(BF16)
    vt_ref[...] = v_ref[...].astype(F32).T.astype(BF16)

    def scores(tile):
        j, i, h = tile
        s = _dot_nt(kh_ref[h, j * t:(j + 1) * t, :], qh_ref[h, i * t:(i + 1) * t, :])
        return jnp.where(valid, s, -jnp.inf) if i == j else s

    def update(tile, s):
        j, i, h = tile
        rows = slice(h * hd, (h + 1) * hd)
        m_new = jnp.max(s, axis=0, keepdims=True)
        if j > 0:
            m_old = ml_ref[i, h:h + 1, :]
            m_new = jnp.maximum(m_old, m_new)
        p = jnp.exp(s - m_new)
        l_new = jnp.sum(p, axis=0, keepdims=True)
        acc = _dot(vt_ref[rows, j * t:(j + 1) * t], p.astype(BF16))
        if j > 0:
            a = jnp.exp(m_old - m_new)
            l_new = a * ml_ref[i, 2 + h:3 + h, :] + l_new
            acc = a * acc_ref[i, rows, :] + acc
        if i == j:
            ot_ref[rows, i * t:(i + 1) * t] = acc / l_new
        else:
            ml_ref[i, h:h + 1, :] = m_new
            ml_ref[i, 2 + h:3 + h, :] = l_new
            acc_ref[i, rows, :] = acc

    tiles = [(j, i, h) for j in range(nq) for i in range(j, nq) for h in range(2)]
    ahead = 4
    pending = []
    for idx in range(len(tiles) + ahead):
        if idx < len(tiles):
            pending.append(scores(tiles[idx]))
        if idx >= ahead:
            update(tiles[idx - ahead], pending.pop(0))
    o_ref[...] = ot_ref[...].T.astype(o_ref.dtype)


def _fox_attn(qkv, c, batch, seq, t=FOX_TILE):
    t = min(t, seq)
    pairs = FOX_HEADS // 2
    c_cols = c.reshape(batch, pairs, 2, seq).transpose(0, 1, 3, 2)
    return pl.pallas_call(
        functools.partial(_fox_attn_body, seq=seq, t=t),
        out_shape=jax.ShapeDtypeStruct((batch * seq, D_MODEL), BF16),
        grid=(batch, pairs),
        in_specs=[pl.BlockSpec((seq, LANES), lambda b, p: (b, p)),
                  pl.BlockSpec((seq, LANES), lambda b, p: (b, pairs + p)),
                  pl.BlockSpec((seq, LANES), lambda b, p: (b, 2 * pairs + p)),
                  pl.BlockSpec((1, 1, seq, 2), lambda b, p: (b, p, 0, 0))],
        out_specs=pl.BlockSpec((seq, LANES), lambda b, p: (b, p)),
        scratch_shapes=[pltpu.VMEM((2, seq, LANES), BF16),
                        pltpu.VMEM((2, seq, LANES), BF16),
                        pltpu.VMEM((LANES, seq), BF16),
                        pltpu.VMEM((seq // t, LANES, t), F32),
                        pltpu.VMEM((seq // t, SUBLANES, t), F32),
                        pltpu.VMEM((LANES, seq), F32)],
        compiler_params=_params(("parallel", "parallel")),
        name="fox_attn",
    )(qkv, qkv, qkv, c_cols)


def _fox_layer(xf, xb, w_in, b_f, w_out, ln_g, ln_b, batch, seq):
    d = D_MODEL
    qkv = _matmul(xb, w_in[:, :3 * d].astype(BF16), BF16, "fox_qkv")
    w_f = jnp.pad(w_in[:, 3 * d:], ((0, 0), (0, LANES - FOX_HEADS)))
    c = _fox_gate(xf, w_f, b_f.reshape(FOX_HEADS, 1), batch, seq)
    o = _fox_attn(qkv, c, batch, seq)
    return _proj_ln(o, w_out.astype(BF16), xf, ln_g, ln_b, "fox_out_ln")


def _gla_body(q_ref, k_ref, v_ref, r_ref, gl_ref, wgu_ref, bg_ref, ng_ref, o_ref, bc_ref, st_ref, *, seq):
    ch = GLA_CHUNK
    tile = min(SUB_TILE, seq)
    r_i = lax.broadcasted_iota(jnp.int32, (tile, tile), 0)
    c_i = lax.broadcasted_iota(jnp.int32, (tile, tile), 1)
    same_chunk_lower = ((r_i >= c_i) & (r_i // ch == c_i // ch)).astype(F32)
    for t0 in range(0, seq, tile):
        gate = _dot(gl_ref[t0:t0 + tile, :], wgu_ref[...], HI) + bg_ref[...]
        log_a = _log_sigmoid(gate) * (1.0 / GLA_GATE_TAU)
        bc_ref[t0:t0 + tile, :] = _dot(same_chunk_lower, log_a, HI)
    lower = _tri(ch, lower=True)
    st_ref[...] = jnp.zeros_like(st_ref)
    heads = range(GLA_HEADS)

    def chunk(i, carry):
        rows = pl.ds(pl.multiple_of(i * ch, ch), ch)
        bcum = bc_ref[rows, :]
        b_last = bcum[ch - 1:ch, :]
        q = q_ref[rows, :].astype(F32) * (GLA_HK ** -0.5)
        k = k_ref[rows, :].astype(F32)
        q_dec = (q * jnp.exp(bcum)).astype(BF16)
        k_inv = (k * jnp.exp(-bcum)).astype(BF16)
        k_end = (k * jnp.exp(b_last - bcum)).astype(BF16)
        decay = jnp.exp(b_last)
        kl = [slice(h * GLA_HK, (h + 1) * GLA_HK) for h in heads]
        vl = [slice(h * GLA_HV, (h + 1) * GLA_HV) for h in heads]
        v = [v_ref[rows, vl[h]] for h in heads]
        att = [jnp.where(lower, _dot_nt(q_dec[:, kl[h]], k_inv[:, kl[h]]), 0.0).astype(BF16)
               for h in heads]
        state_t = [st_ref[h] for h in heads]
        o_inter = [_dot_nt(q_dec[:, kl[h]], state_t[h].astype(BF16)) for h in heads]
        o_intra = [_dot(att[h], v[h]) for h in heads]
        kv_t = [_dot_tn(v[h], k_end[:, kl[h]]) for h in heads]
        for h in heads:
            st_ref[h] = state_t[h] * decay[:, kl[h]] + kv_t[h]
            o = o_intra[h] + o_inter[h]
            ms = jnp.mean(o * o, axis=-1, keepdims=True)
            on = o * lax.rsqrt(ms + RMS_EPS) * ng_ref[...]
            o_ref[rows, vl[h]] = (on * _silu(r_ref[rows, vl[h]].astype(F32))).astype(o_ref.dtype)
        return carry

    lax.fori_loop(0, seq // ch, chunk, 0)


def _gla_mix(proj, g_low, w_gu, b_gate, norm_g, batch, seq):
    return pl.pallas_call(
        functools.partial(_gla_body, seq=seq),
        out_shape=jax.ShapeDtypeStruct((batch * seq, GLA_DV), BF16),
        grid=(batch,),
        in_specs=[pl.BlockSpec((seq, GLA_DK), lambda b: (b, 0)),
                  pl.BlockSpec((seq, GLA_DK), lambda b: (b, 1)),
                  pl.BlockSpec((seq, GLA_DV), lambda b: (b, 1)),
                  pl.BlockSpec((seq, GLA_DV), lambda b: (b, 2)),
                  pl.BlockSpec((seq, LANES), lambda b: (b, 0)),
                  pl.BlockSpec((LANES, GLA_DK), lambda b: (0, 0)),
                  pl.BlockSpec((1, GLA_DK), lambda b: (0, 0)),
                  pl.BlockSpec((1, GLA_HV), lambda b: (0, 0))],
        out_specs=pl.BlockSpec((seq, GLA_DV), lambda b: (b, 0)),
        scratch_shapes=[pltpu.VMEM((seq, GLA_DK), F32),
                        pltpu.VMEM((GLA_HEADS, GLA_HV, GLA_HK), F32)],
        compiler_params=_params(("parallel",)),
        name="gla_mix",
    )(proj, proj, proj, proj, g_low, w_gu, b_gate.reshape(1, GLA_DK), norm_g.reshape(1, GLA_HV))


def _gla_layer(xf, xb, w_in, w_gate_up, b_gate, norm_g, w_out, ln_g, ln_b, batch, seq):
    n_main = 2 * GLA_DK + 2 * GLA_DV
    proj = _matmul(xb, w_in[:, :n_main].astype(BF16), BF16, "gla_proj")
    w_g = jnp.pad(w_in[:, n_main:], ((0, 0), (0, LANES - GLA_GATE_RANK)))
    g_low = _matmul(xf, w_g, F32, "gla_gate", tn=LANES, precision=HI)
    w_gu = jnp.pad(w_gate_up, ((0, LANES - GLA_GATE_RANK), (0, 0)))
    y = _gla_mix(proj, g_low, w_gu, b_gate, norm_g, batch, seq)
    return _proj_ln(y, w_out.astype(BF16), xf, ln_g, ln_b, "gla_out_ln")


def _ssd_gate_body(x_ref, w_ref, bias_ref, alog_ref, dt_ref, ac_ref, dtt_ref, act_ref, *, seq):
    ch = SSD_CHUNK
    raw = _dot(x_ref[...], w_ref[...], HI)
    dt = _softplus(raw + bias_ref[...])
    a = dt * (-jnp.exp(alog_ref[...]))
    lower_f = _tri(ch, lower=True).astype(F32)
    dt_ref[...] = dt
    dtt_ref[0] = dt.T
    for c in range(seq // ch):
        acum = _dot(lower_f, a[c * ch:(c + 1) * ch, :], HI)
        ac_ref[c * ch:(c + 1) * ch, :] = acum
        act_ref[0, :, c * ch:(c + 1) * ch] = acum.T


def _ssd_gate(x, w_dt, dt_bias, a_log, batch, seq):
    d = x.shape[1]
    col = jax.ShapeDtypeStruct((batch * seq, LANES), F32)
    row = jax.ShapeDtypeStruct((batch, LANES, seq), F32)
    return pl.pallas_call(
        functools.partial(_ssd_gate_body, seq=seq),
        out_shape=(col, col, row, row),
        grid=(batch,),
        in_specs=[pl.BlockSpec((seq, d), lambda b: (b, 0)),
                  pl.BlockSpec((d, LANES), lambda b: (0, 0)),
                  pl.BlockSpec((1, LANES), lambda b: (0, 0)),
                  pl.BlockSpec((1, LANES), lambda b: (0, 0))],
        out_specs=(pl.BlockSpec((seq, LANES), lambda b: (b, 0)),
                   pl.BlockSpec((seq, LANES), lambda b: (b, 0)),
                   pl.BlockSpec((1, LANES, seq), lambda b: (b, 0, 0)),
                   pl.BlockSpec((1, LANES, seq), lambda b: (b, 0, 0))),
        compiler_params=_params(("parallel",)),
        name="ssd_gate",
    )(x, w_dt, dt_bias, a_log)


def _expand_heads(cols, width):
    rows = cols.shape[0]
    lane = lax.broadcasted_iota(jnp.int32, (rows, SSD_HG * width), 1)
    out = jnp.broadcast_to(cols[:, SSD_HG - 1:SSD_HG], (rows, SSD_HG * width))
    for h in range(SSD_HG - 2, -1, -1):
        out = jnp.where(lane < (h + 1) * width, jnp.broadcast_to(cols[:, h:h + 1], out.shape), out)
    return out


def _ssd_body(z_ref, xr_ref, br_ref, cr_ref, cwx_ref, cwb_ref, cwc_ref, cbx_ref, cbb_ref, cbc_ref,
              dt4_ref, ac4_ref, dtt_ref, act_ref, dskip_ref, ng_ref, o_ref,
              padx_ref, padb_ref, xs_ref, bm_ref, cm_ref, st_ref, *, seq):
    ch = SSD_CHUNK
    hp = SSD_HG * SSD_P
    n = SSD_STATE
    pad = SUBLANES

    def conv(raw_ref, pad_ref, w_ref, b_ref, dst_ref):
        pad_ref[0:pad, :] = jnp.zeros((pad, pad_ref.shape[1]), F32)
        pad_ref[pad:pad + seq, :] = raw_ref[...].astype(F32)
        tile = min(SUB_TILE, seq)
        width = min(SUB_TILE, pad_ref.shape[1])
        for c0 in range(0, pad_ref.shape[1], width):
            cols = slice(c0, c0 + width)
            for t0 in range(0, seq, tile):
                acc = jnp.broadcast_to(b_ref[:, cols], (tile, width))
                for j in range(SSD_CONV):
                    off = t0 + pad - (SSD_CONV - 1) + j
                    acc = acc + w_ref[j:j + 1, cols] * pad_ref[off:off + tile, cols]
                dst_ref[t0:t0 + tile, cols] = _silu(acc)

    conv(xr_ref, padx_ref, cwx_ref, cbx_ref, xs_ref)
    conv(br_ref, padb_ref, cwb_ref, cbb_ref, bm_ref)
    conv(cr_ref, padb_ref, cwc_ref, cbc_ref, cm_ref)

    lower = _tri(ch, lower=True)
    lane = lax.broadcasted_iota(jnp.int32, (1, hp), 1)
    st_ref[...] = jnp.zeros_like(st_ref)

    def group_chunk(gi, c, rows):
        cols = slice(gi * hp, (gi + 1) * hp)
        ncols = slice(gi * n, (gi + 1) * n)
        xs = xs_ref[rows, cols]
        bm = bm_ref[rows, ncols].astype(BF16)
        cm = cm_ref[rows, ncols].astype(BF16)
        dt4 = dt4_ref[0, gi, rows, :]
        ac4 = ac4_ref[0, gi, rows, :]
        cb = _dot_nt(cm, bm)
        yield
        state = st_ref[gi]
        y_off = _dot(cm, state.astype(BF16))
        yield
        to_end = jnp.exp(ac4[ch - 1:ch, :] - ac4)
        w_state = _expand_heads(dt4 * to_end, SSD_P)
        e_acum = _expand_heads(jnp.exp(ac4), SSD_P)
        new_state = _dot_tn(bm, (xs * w_state).astype(BF16))
        yield
        st_ref[gi] = state * e_acum[ch - 1:ch, :] + new_state
        xs_b = xs.astype(BF16)
        y = xs * dskip_ref[:, cols] + y_off * e_acum
        for h in range(SSD_HG):
            a_col = jnp.broadcast_to(ac4[:, h:h + 1], (ch, ch))
            a_row = act_ref[0, gi, h, pl.ds(c, 1), :]
            d_row = dtt_ref[0, gi, h, pl.ds(c, 1), :]
            decay = jnp.exp(jnp.where(lower, a_col - a_row, -jnp.inf))
            m_h = (cb * decay * d_row).astype(BF16)
            head = (lane >= h * SSD_P) & (lane < (h + 1) * SSD_P)
            y = y + _dot(m_h, jnp.where(head, xs_b, jnp.zeros_like(xs_b)))
            yield
        y = y * _silu(z_ref[rows, cols].astype(F32))
        ms = jnp.mean(y * y, axis=-1, keepdims=True)
        o_ref[rows, cols] = (y * lax.rsqrt(ms + RMS_EPS) * ng_ref[:, cols]).astype(o_ref.dtype)

    def chunk(c, carry):
        rows = pl.ds(pl.multiple_of(c * ch, ch), ch)
        running = [group_chunk(gi, c, rows) for gi in range(SSD_GROUPS_PER_STEP)]
        while running:
            for gen in list(running):
                if next(gen, "done") == "done":
                    running.remove(gen)
        return carry

    lax.fori_loop(0, seq // ch, chunk, 0)


def _ssd_mix(proj, conv_w, conv_b, dt4, ac4, dtt, act, d_exp, norm_g, batch, seq):
    per = SSD_GROUPS_PER_STEP
    steps = SSD_GROUPS // per
    hp = per * SSD_HG * SSD_P
    n = per * SSD_STATE
    nz = SSD_D_INNER // hp
    nb = 2 * SSD_D_INNER // n
    nc = nb + steps
    cb0 = SSD_D_INNER // n
    nch = seq // SSD_CHUNK
    return pl.pallas_call(
        functools.partial(_ssd_body, seq=seq),
        out_shape=jax.ShapeDtypeStruct((batch * seq, SSD_D_INNER), BF16),
        grid=(batch, steps),
        in_specs=[pl.BlockSpec((seq, hp), lambda b, i: (b, i)),
                  pl.BlockSpec((seq, hp), lambda b, i: (b, nz + i)),
                  pl.BlockSpec((seq, n), lambda b, i: (b, nb + i)),
                  pl.BlockSpec((seq, n), lambda b, i: (b, nc + i)),
                  pl.BlockSpec((SSD_CONV, hp), lambda b, i: (0, i)),
                  pl.BlockSpec((SSD_CONV, n), lambda b, i: (0, cb0 + i)),
                  pl.BlockSpec((SSD_CONV, n), lambda b, i: (0, cb0 + steps + i)),
                  pl.BlockSpec((1, hp), lambda b, i: (0, i)),
                  pl.BlockSpec((1, n), lambda b, i: (0, cb0 + i)),
                  pl.BlockSpec((1, n), lambda b, i: (0, cb0 + steps + i)),
                  pl.BlockSpec((1, per, seq, SSD_HG), lambda b, i: (b, i, 0, 0)),
                  pl.BlockSpec((1, per, seq, SSD_HG), lambda b, i: (b, i, 0, 0)),
                  pl.BlockSpec((1, per, SSD_HG, nch, SSD_CHUNK), lambda b, i: (b, i, 0, 0, 0)),
                  pl.BlockSpec((1, per, SSD_HG, nch, SSD_CHUNK), lambda b, i: (b, i, 0, 0, 0)),
                  pl.BlockSpec((1, hp), lambda b, i: (0, i)),
                  pl.BlockSpec((1, hp), lambda b, i: (0, i))],
        out_specs=pl.BlockSpec((seq, hp), lambda b, i: (b, i)),
        scratch_shapes=[pltpu.VMEM((seq + SUBLANES, hp), F32),
                        pltpu.VMEM((seq + SUBLANES, n), F32),
                        pltpu.VMEM((seq, hp), F32),
                        pltpu.VMEM((seq, n), F32),
                        pltpu.VMEM((seq, n), F32),
                        pltpu.VMEM((per, SSD_STATE, SSD_HG * SSD_P), F32)],
        compiler_params=_params(("parallel", "parallel")),
        name="ssd_mix",
    )(proj, proj, proj, proj, conv_w, conv_w, conv_w, conv_b, conv_b, conv_b,
      dt4, ac4, dtt, act, d_exp, norm_g)


def _ssd_layer(xf, xb, w_in, conv_w, conv_b, dt_bias, a_log, d_skip, norm_g, w_out, ln_g, ln_b,
               batch, seq):
    n_main = SSD_D_INNER + (SSD_D_INNER + 2 * SSD_GROUPS * SSD_STATE)
    proj = _matmul(xb, w_in[:, :n_main].astype(BF16), BF16, "ssd_proj")
    lane_pad = ((0, 0), (0, LANES - SSD_HEADS))
    w_dt = jnp.pad(w_in[:, n_main:], lane_pad)
    dt_c, ac_c, dt_r, ac_r = _ssd_gate(xf, w_dt, jnp.pad(dt_bias.reshape(1, -1), lane_pad),
                                       jnp.pad(a_log.reshape(1, -1), lane_pad), batch, seq)

    def cols(t):
        return t[:, :SSD_HEADS].reshape(batch, seq, SSD_GROUPS, SSD_HG).transpose(0, 2, 1, 3)

    def rows(t):
        return t[:, :SSD_HEADS].reshape(batch, SSD_GROUPS, SSD_HG, seq // SSD_CHUNK, SSD_CHUNK)

    d_exp = jnp.repeat(d_skip, SSD_P).reshape(1, SSD_D_INNER)
    y = _ssd_mix(proj, conv_w, conv_b.reshape(1, -1), cols(dt_c), cols(ac_c), rows(dt_r), rows(ac_r),
                 d_exp, norm_g.reshape(1, SSD_D_INNER), batch, seq)
    return _proj_ln(y, w_out.astype(BF16), xf, ln_g, ln_b, "ssd_out_ln")


def _router_body(x_ref, w_ref, bias_ref, e_ref, w_out_ref):
    tm = x_ref.shape[0]
    gsz = N_EXPERTS // N_EXPERT_GROUPS
    logits = _dot(x_ref[...], w_ref[...], HI).T[:N_EXPERTS, :]
    scores = _sigmoid(logits)
    sel = scores + bias_ref[...]
    sub = lax.broadcasted_iota(jnp.int32, (gsz, tm), 0).astype(F32)
    neg = jnp.float32(-jnp.inf)

    def top1(v, idx, sentinel):
        m = jnp.max(v, axis=0, keepdims=True)
        i = jnp.min(jnp.where(v == m, idx, sentinel), axis=0, keepdims=True)
        return m, i

    s_g = [sel[g * gsz:(g + 1) * gsz, :] for g in range(N_EXPERT_GROUPS)]
    sc_g = [scores[g * gsz:(g + 1) * gsz, :] for g in range(N_EXPERT_GROUPS)]
    grp_rows = []
    for g in range(N_EXPERT_GROUPS):
        m1, i1 = top1(s_g[g], sub, float(gsz))
        m2 = jnp.max(jnp.where(sub == i1, neg, s_g[g]), axis=0, keepdims=True)
        grp_rows.append(m1 + m2)
    grp = jnp.concatenate(grp_rows, axis=0)
    keep = jnp.zeros((N_EXPERT_GROUPS, tm), F32)
    for _ in range(TOPK_GROUPS):
        _, ig = top1(grp, sub, float(N_EXPERT_GROUPS))
        hit = sub == ig
        keep = jnp.where(hit, 1.0, keep)
        grp = jnp.where(hit, neg, grp)
    cand = [jnp.where(keep[g:g + 1, :] > 0.5, s_g[g], neg) for g in range(N_EXPERT_GROUPS)]
    ids = [sub + float(g * gsz) for g in range(N_EXPERT_GROUPS)]
    e_rows, w_rows = [], []
    for _ in range(TOP_K):
        m = cand[0]
        for g in range(1, N_EXPERT_GROUPS):
            m = jnp.maximum(m, cand[g])
        m = jnp.max(m, axis=0, keepdims=True)
        first = jnp.where(cand[0] == m, ids[0], float(N_EXPERTS))
        for g in range(1, N_EXPERT_GROUPS):
            first = jnp.minimum(first, jnp.where(cand[g] == m, ids[g], float(N_EXPERTS)))
        first = jnp.min(first, axis=0, keepdims=True)
        wsum = jnp.zeros((gsz, tm), F32)
        for g in range(N_EXPERT_GROUPS):
            hit = ids[g] == first
            wsum = wsum + jnp.where(hit, sc_g[g], 0.0)
            cand[g] = jnp.where(hit, neg, cand[g])
        e_rows.append(first)
        w_rows.append(jnp.sum(wsum, axis=0, keepdims=True))
    top_e = jnp.concatenate(e_rows, axis=0)
    top_w = jnp.concatenate(w_rows, axis=0)
    denom = jnp.sum(top_w, axis=0, keepdims=True)
    e_ref[...] = top_e.astype(jnp.int32)
    w_out_ref[...] = top_w / denom * ROUTED_SCALE


def _router(x, w_router, bias, tm=ROW_BLOCK):
    t, d = x.shape
    tm = min(tm, t)
    return pl.pallas_call(
        _router_body,
        out_shape=(jax.ShapeDtypeStruct((TOP_K, t), jnp.int32),
                   jax.ShapeDtypeStruct((TOP_K, t), F32)),
        grid=(t // tm,),
        in_specs=[pl.BlockSpec((tm, d), lambda i: (i, 0)),
                  pl.BlockSpec((d, LANES), lambda i: (0, 0)),
                  pl.BlockSpec((N_EXPERTS, 1), lambda i: (0, 0))],
        out_specs=(pl.BlockSpec((TOP_K, tm), lambda i: (0, i)),
                   pl.BlockSpec((TOP_K, tm), lambda i: (0, i))),
        compiler_params=_params(("parallel",)),
        name="moe_router",
    )(x, jnp.pad(w_router, ((0, 0), (0, LANES - N_EXPERTS))), bias.reshape(N_EXPERTS, 1))


def _moe_list_len(tb):
    chunks = MOE_LEAD_CHUNKS + -(-tb * TOP_K // MOE_ROWS) + N_EXPERTS + N_EXPERTS // MOE_EXPERTS_PER_STEP + 2
    return -(-chunks * MOE_ROWS // MOE_LIST_ALIGN) * MOE_LIST_ALIGN


def _moe_body(sstart_ref, scount_ref, cexp_ref, gt_hbm, st_hbm,
              rw_ref, x8_ref, wg_ref, wu_ref, wd_ref, sg_ref, su_ref, sd_ref,
              out_ref, gt_ref, st_ref, buf_ref, y2_ref, sem, *, tb):
    blk = pl.program_id(0)
    grp = pl.program_id(1)
    rows_per = MOE_ROWS
    tile = min(SUB_TILE, tb)
    n_list = st_ref.shape[0]
    n_chunks = n_list // rows_per
    acc_ref = out_ref.at[0]

    def tile_at(ref, off):
        return ref.at[pl.ds(pl.multiple_of(off, SUBLANES), SUBLANES), :]

    def gather(chunk, parity):
        src = gt_ref.at[pl.ds((chunk + MOE_LEAD_CHUNKS) * rows_per, rows_per)]
        dst = buf_ref.at[parity]
        for r in range(rows_per):
            tile_at(dst, r * SUBLANES)[...] = tile_at(x8_ref, src[r])[...]

    def scatter(chunk, parity):
        y_ref = y2_ref.at[parity]
        dst = st_ref.at[pl.ds((chunk + MOE_LEAD_CHUNKS) * rows_per, rows_per)]
        for r0 in range(0, rows_per, MOE_SCATTER_GROUP):
            offs = [dst[r0 + u] for u in range(MOE_SCATTER_GROUP)]
            vals = [tile_at(acc_ref, offs[u])[...] + tile_at(y_ref, (r0 + u) * SUBLANES)[...]
                    for u in range(MOE_SCATTER_GROUP)]
            for u in range(MOE_SCATTER_GROUP):
                tile_at(acc_ref, offs[u])[...] = vals[u]

    def row_weights(chunk):
        w_row = rw_ref[0, pl.ds(chunk + MOE_LEAD_CHUNKS, 1), :]
        return jnp.broadcast_to(w_row, (rows_per, rows_per)).T

    @pl.when(grp == 0)
    def _start_block():
        copies = [pltpu.make_async_copy(src.at[pl.ds(pl.multiple_of(blk * n_list, MOE_LIST_ALIGN), n_list)],
                                        dst, sem.at[n])
                  for n, (src, dst) in enumerate(((gt_hbm, gt_ref), (st_hbm, st_ref)))]
        for cp in copies:
            cp.start()
        for t0 in range(0, tb, tile):
            xb = _tok_rows(x8_ref, t0, tile).astype(BF16)
            h = _silu(_dot(xb, sg_ref[...])) * _dot(xb, su_ref[...])
            _tok_store(acc_ref, t0, _dot(h.astype(BF16), sd_ref[...]))
        tile_at(acc_ref, tb * SUBLANES)[...] = jnp.zeros((SUBLANES, LANES), F32)
        y2_ref[...] = jnp.zeros_like(y2_ref)
        for cp in copies:
            cp.wait()
        gather(0, 0)
        gather(1, 1)

    step_id = blk * pl.num_programs(1) + grp
    first = sstart_ref[step_id]
    pairs = scount_ref[step_id] // 2

    def step(i, carry):
        g = first + 2 * i
        xb = [_tok_rows(buf_ref.at[p], 0, rows_per).astype(BF16) for p in range(2)]
        e = [cexp_ref[blk * n_chunks + g + p] for p in range(2)]
        for p in range(2):
            scatter(g - 2 + p, p)
        for p in range(2):
            gather(g + 2 + p, p)
        gate = [_dot(xb[p], wg_ref[e[p]]) for p in range(2)]
        up = [_dot(xb[p], wu_ref[e[p]]) for p in range(2)]
        h = [(_silu(gate[p]) * up[p]).astype(BF16) for p in range(2)]
        y = [_dot(h[p], wd_ref[e[p]]) for p in range(2)]
        for p in range(2):
            w_rows = jnp.concatenate([row_weights(g + p)] * (y[p].shape[1] // rows_per), axis=1)
            _tok_store(y2_ref.at[p], 0, y[p] * w_rows)
        return carry

    lax.fori_loop(0, pairs, step, 0)

    @pl.when(grp == pl.num_programs(1) - 1)
    def _finish_block():
        last = first + 2 * pairs
        for p in range(2):
            scatter(last - 2 + p, p)


def _moe_experts(x8, sstart, scount, cexp, gt, st, rw, wg, wu, wd, layer, sg, su, sd, tb):
    d = wg.shape[2]
    ff = wg.shape[3]
    nblk = x8.shape[0] // (tb * SUBLANES)
    n_list = _moe_list_len(tb)
    per_step = MOE_EXPERTS_PER_STEP
    grid_spec = pltpu.PrefetchScalarGridSpec(
        num_scalar_prefetch=3,
        grid=(nblk, N_EXPERTS // per_step),
        in_specs=[pl.BlockSpec(memory_space=pl.ANY),
                  pl.BlockSpec(memory_space=pl.ANY),
                  pl.BlockSpec((1, n_list // MOE_ROWS, MOE_ROWS), lambda i, e, *_: (i, 0, 0)),
                  pl.BlockSpec((tb * SUBLANES, LANES), lambda i, e, *_: (i, 0)),
                  pl.BlockSpec((None, per_step, d, ff), lambda i, e, *_: (layer, e, 0, 0)),
                  pl.BlockSpec((None, per_step, d, ff), lambda i, e, *_: (layer, e, 0, 0)),
                  pl.BlockSpec((None, per_step, ff, d), lambda i, e, *_: (layer, e, 0, 0)),
                  pl.BlockSpec((d, ff), lambda i, e, *_: (0, 0)),
                  pl.BlockSpec((d, ff), lambda i, e, *_: (0, 0)),
                  pl.BlockSpec((ff, d), lambda i, e, *_: (0, 0))],
        out_specs=pl.BlockSpec((1, (tb + 1) * SUBLANES, LANES), lambda i, e, *_: (i, 0, 0)),
        scratch_shapes=[pltpu.SMEM((n_list,), jnp.int32),
                        pltpu.SMEM((n_list,), jnp.int32),
                        pltpu.VMEM((2, MOE_ROWS * SUBLANES, LANES), F32),
                        pltpu.VMEM((2, MOE_ROWS * SUBLANES, LANES), F32),
                        pltpu.SemaphoreType.DMA((2,))],
    )
    return pl.pallas_call(
        functools.partial(_moe_body, tb=tb),
        out_shape=jax.ShapeDtypeStruct((nblk, (tb + 1) * SUBLANES, LANES), F32),
        grid_spec=grid_spec,
        compiler_params=_params(("arbitrary", "arbitrary")),
        name="moe_experts",
    )(sstart, scount, cexp, gt, st, rw, x8, wg, wu, wd, sg, su, sd)


def _moe_finish_body(x_ref, acc_ref, g_ref, b_ref, xf_ref, xb_ref):
    z = DN_ALPHA * x_ref[...] + _tok_rows(acc_ref.at[0], 0, x_ref.shape[0])
    out = _layer_norm_rows(z, g_ref[...], b_ref[...])
    xf_ref[...] = out
    xb_ref[...] = out.astype(BF16)


def _moe_finish(x, acc8, g, b, tb, tm=ROW_BLOCK):
    m, d = x.shape
    tm = min(tm, tb)
    per_blk = tb // tm
    return pl.pallas_call(
        _moe_finish_body,
        out_shape=(jax.ShapeDtypeStruct((m, d), F32), jax.ShapeDtypeStruct((m, d), BF16)),
        grid=(m // tm,),
        in_specs=[pl.BlockSpec((tm, d), lambda i: (i, 0)),
                  pl.BlockSpec((1, tm * SUBLANES, LANES), lambda i: (i // per_blk, i % per_blk, 0)),
                  pl.BlockSpec((1, d), lambda i: (0, 0)),
                  pl.BlockSpec((1, d), lambda i: (0, 0))],
        out_specs=(pl.BlockSpec((tm, d), lambda i: (i, 0)),
                   pl.BlockSpec((tm, d), lambda i: (i, 0))),
        compiler_params=_params(("parallel",)),
        name="moe_finish",
    )(x, acc8, g.reshape(1, d), b.reshape(1, d))


def _moe_layer(xf, x8, w_router, router_bias, wg, wu, wd, layer, ws_gate, ws_up, ws_down,
               ln_g, ln_b, tb):
    t, d = xf.shape
    top_e, top_w = _router(xf, w_router, router_bias)
    nblk = t // tb
    per = tb * TOP_K
    e_flat = top_e.T.reshape(nblk, per)
    w_flat = top_w.T.reshape(nblk, per)
    assert N_EXPERTS * per < 2 ** 31
    packed = jnp.sort(e_flat * per + jnp.arange(per, dtype=jnp.int32), axis=-1)
    order = packed % per
    cnt = jnp.sum(e_flat[:, :, None] == jnp.arange(N_EXPERTS, dtype=jnp.int32), axis=1,
                  dtype=jnp.int32)
    nch = (cnt + MOE_ROWS - 1) // MOE_ROWS
    groups = N_EXPERTS // MOE_EXPERTS_PER_STEP
    gpad = jnp.sum(nch.reshape(nblk, groups, MOE_EXPERTS_PER_STEP), axis=-1) % 2
    pads_before = jnp.cumsum(gpad, axis=-1, dtype=jnp.int32) - gpad
    cend = (jnp.cumsum(nch, axis=-1, dtype=jnp.int32)
            + jnp.repeat(pads_before, MOE_EXPERTS_PER_STEP, axis=-1))
    n_list = _moe_list_len(tb)
    n_chunks = n_list // MOE_ROWS
    chunk = jnp.arange(n_chunks, dtype=jnp.int32) - MOE_LEAD_CHUNKS
    before = cend[:, None, :] <= chunk[None, :, None]
    exp_of = jnp.sum(before, axis=-1, dtype=jnp.int32)
    earlier = jnp.arange(groups, dtype=jnp.int32) < (exp_of // MOE_EXPERTS_PER_STEP)[..., None]
    first_chunk = (jnp.sum(jnp.where(before, nch[:, None, :], 0), axis=-1)
                   + jnp.sum(jnp.where(earlier, gpad[:, None, :], 0), axis=-1))
    first_src = jnp.sum(jnp.where(before, cnt[:, None, :], 0), axis=-1)
    own = jnp.arange(N_EXPERTS, dtype=jnp.int32) == exp_of[..., None]
    n_rows = jnp.sum(jnp.where(own, cnt[:, None, :], 0), axis=-1)
    row = ((chunk[None, :] - first_chunk) * MOE_ROWS)[..., None] + jnp.arange(MOE_ROWS, dtype=jnp.int32)
    valid = ((chunk[None, :, None] >= 0) & (row >= 0) & (row < n_rows[..., None])).reshape(nblk, n_list)
    src = lax.optimization_barrier(
        jnp.clip(first_src[..., None] + row, 0, per - 1).reshape(nblk, n_list))
    picked = lax.optimization_barrier(jnp.take_along_axis(order, src, axis=-1))
    tile_off = (picked // TOP_K) * SUBLANES
    gt = jnp.where(valid, tile_off, 0)
    st = jnp.where(valid, tile_off, tb * SUBLANES)
    rw_pad = jnp.where(valid, jnp.take_along_axis(w_flat, picked, axis=-1), 0.0)
    gend = cend.reshape(nblk, groups, MOE_EXPERTS_PER_STEP)[:, :, -1] + gpad
    gstart = jnp.concatenate([jnp.zeros((nblk, 1), jnp.int32), gend[:, :-1]], axis=1)
    cexp = jnp.minimum(exp_of, N_EXPERTS - 1) % MOE_EXPERTS_PER_STEP
    cexp = jnp.roll(cexp, -MOE_LEAD_CHUNKS, axis=1)
    sg, su, sd = ws_gate.astype(BF16), ws_up.astype(BF16), ws_down.astype(BF16)
    acc8 = _moe_experts(x8, gstart.reshape(-1), (gend - gstart).reshape(-1), cexp.reshape(-1),
                        gt.reshape(-1), st.reshape(-1), rw_pad.reshape(nblk, n_chunks, MOE_ROWS),
                        wg, wu, wd, layer, sg, su, sd, tb)
    return _moe_finish(xf, acc8, ln_g, ln_b, tb)


def kernel(x, fox_w_in, fox_b_f, fox_w_out, gla_w_in, gla_w_gate_up, gla_b_gate, gla_norm, gla_w_out, ssd_w_in, ssd_conv_w, ssd_conv_b, ssd_dt_bias, ssd_a_log, ssd_d, ssd_norm, ssd_w_out, ln1_g, ln1_b, moe_router, moe_router_bias, moe_w_gate, moe_w_up, moe_w_down, moe_ws_gate, moe_ws_up, moe_ws_down, ln2_g, ln2_b):
    batch, seq, d = x.shape
    xf = x.reshape(batch * seq, d)
    xb = xf.astype(BF16)
    tb = seq
    wg_all, wu_all, wd_all = moe_w_gate.astype(BF16), moe_w_up.astype(BF16), moe_w_down.astype(BF16)
    for i in range(DEPTH):
        kind, j = i % N_MIXERS, i // N_MIXERS
        if kind == 0:
            xf, xb, x8 = _fox_layer(xf, xb, fox_w_in[j], fox_b_f[j], fox_w_out[j], ln1_g[i], ln1_b[i],
                                    batch, seq)
        elif kind == 1:
            xf, xb, x8 = _gla_layer(xf, xb, gla_w_in[j], gla_w_gate_up[j], gla_b_gate[j], gla_norm[j],
                                    gla_w_out[j], ln1_g[i], ln1_b[i], batch, seq)
        else:
            xf, xb, x8 = _ssd_layer(xf, xb, ssd_w_in[j], ssd_conv_w[j], ssd_conv_b[j], ssd_dt_bias[j],
                                    ssd_a_log[j], ssd_d[j], ssd_norm[j], ssd_w_out[j], ln1_g[i],
                                    ln1_b[i], batch, seq)
        xf, xb = _moe_layer(xf, x8, moe_router[i], moe_router_bias[i], wg_all, wu_all, wd_all, i,
                            moe_ws_gate[i], moe_ws_up[i], moe_ws_down[i], ln2_g[i], ln2_b[i], tb)
    return xf.reshape(batch, seq, d)
```
